```python
import jax, jax.numpy as jnp
from jax import lax
import numpy as np

D_MODEL = 2048
BATCH = 4
SEQ = 8192
DEPTH = 2
DEC_BATCH = 16
DEC_SEQ = 64
PAST_LEN = 2048

CHUNK = 64
QBLK = 128
DH = 128
MIX = D_MODEL
N_HEADS = MIX // DH
H_A = N_HEADS // 4
H_C = (N_HEADS - H_A) // 2
H_B = N_HEADS - H_A - H_C
C_PREV = 8
C_PAST = C_PREV * CHUNK
C_BAND = C_PAST + CHUNK
REL_CLIP = 128
N_GROUPS = 4
EXP_PER_GROUP = 8
N_EXPERTS = N_GROUPS * EXP_PER_GROUP
TOP_K = 2
D_EXPERT = D_MODEL // 4
MOE_BLK = 128
EPS = 1e-6
NEG_INF = -1e30
SCALE = DH ** -0.5

kernel_name = 'hybrid_stickbreak_fox_chunkband_hmoe_step'


def rmsnorm(x, g):
    xf = x.astype(jnp.float32)
    y = xf * lax.rsqrt(jnp.mean(xf * xf, axis=-1, keepdims=True) + EPS)
    return (y * g.astype(jnp.float32)).astype(x.dtype)


def to_blocks(a):
    return jnp.moveaxis(a.reshape(a.shape[0], -1, QBLK, *a.shape[2:]), 1, 0)


def from_blocks(a):
    a = jnp.moveaxis(a, 0, 1)
    return a.reshape(a.shape[0], -1, *a.shape[3:])


def stick_breaking_block(q, k, v, q_pos, k_pos):
    z = jnp.einsum('bqhd,bkhd->bhqk', q, k).astype(jnp.float32) * SCALE
    mask = k_pos[None, :] < q_pos[:, None]
    log_1m = jnp.where(mask, jax.nn.log_sigmoid(-z), 0.0)
    suffix = lax.cumsum(log_1m, axis=3, reverse=True) - log_1m
    w = jnp.where(mask, jnp.exp(jax.nn.log_sigmoid(z) + suffix), 0.0)
    return jnp.einsum('bhqk,bkhd->bqhd', w.astype(v.dtype), v)


def forgetting_block(q, k, v, f_q, f_k, q_pos, k_pos):
    s = jnp.einsum('bqhd,bkhd->bhqk', q, k).astype(jnp.float32) * SCALE
    s = s + jnp.swapaxes(f_q, 1, 2)[..., :, None] - jnp.swapaxes(f_k, 1, 2)[..., None, :]
    mask = k_pos[None, :] <= q_pos[:, None]
    p = jax.nn.softmax(jnp.where(mask, s, NEG_INF), axis=-1)
    return jnp.einsum('bhqk,bkhd->bqhd', p.astype(v.dtype), v)


def chunk_band_block(q, k, v, q_pos, k_pos, rel_bias):
    s = jnp.einsum('bqhd,bkhd->bhqk', q, k).astype(jnp.float32) * SCALE
    rel = jnp.clip(q_pos[:, None] - k_pos[None, :], -REL_CLIP, REL_CLIP) + REL_CLIP
    s = s + rel_bias.astype(jnp.float32)[:, rel][None]
    cq = q_pos[:, None] // CHUNK
    ck = k_pos[None, :] // CHUNK
    mask = (ck <= cq) & (ck >= cq - C_PREV) & (k_pos[None, :] >= 0)
    p = jax.nn.softmax(jnp.where(mask, s, NEG_INF), axis=-1)
    return jnp.einsum('bhqk,bkhd->bqhd', p.astype(v.dtype), v)


def project(hn, w_in, b_f):
    bn, t, _ = hn.shape
    proj = hn @ w_in
    qkv = proj[..., :3 * MIX].reshape(bn, t, 3, N_HEADS, DH)
    logf = jax.nn.log_sigmoid(proj[..., 3 * MIX:].astype(jnp.float32) + b_f.astype(jnp.float32))
    return qkv[:, :, 0], qkv[:, :, 1], qkv[:, :, 2], logf


def split_heads(t):
    return t[:, :, :H_A], t[:, :, H_A:H_A + H_B], t[:, :, H_A + H_B:]


def mixer_prompt(hn, w_in, b_f, rel_bias):
    bn, s_len, _ = hn.shape
    q, k, v, logf = project(hn, w_in, b_f)
    qa, qb, qc = split_heads(q)
    ka, kb, kc = split_heads(k)
    va, vb, vc = split_heads(v)
    pos = jnp.arange(s_len)
    pos_blocks = pos.reshape(-1, QBLK)
    oa = from_blocks(lax.map(lambda a: stick_breaking_block(a[0], ka, va, a[1], pos),
                             (to_blocks(qa), pos_blocks)))
    f_cum = jnp.cumsum(logf, axis=1)
    ob = from_blocks(lax.map(lambda a: forgetting_block(a[0], kb, vb, a[2], f_cum, a[1], pos),
                             (to_blocks(qb), pos_blocks, to_blocks(f_cum))))
    pad = ((0, 0), (C_PAST, 0), (0, 0), (0, 0))
    kpad = jnp.pad(kc, pad)
    vpad = jnp.pad(vc, pad)

    def chunk_fn(c):
        s0 = c * CHUNK
        qq = lax.dynamic_slice_in_dim(qc, s0, CHUNK, axis=1)
        kk = lax.dynamic_slice_in_dim(kpad, s0, C_BAND, axis=1)
        vv = lax.dynamic_slice_in_dim(vpad, s0, C_BAND, axis=1)
        q_pos = s0 + jnp.arange(CHUNK)
        k_pos = s0 - C_PAST + jnp.arange(C_BAND)
        return chunk_band_block(qq, kk, vv, q_pos, k_pos, rel_bias)

    oc = from_blocks(lax.map(chunk_fn, jnp.arange(s_len // CHUNK)))
    out = jnp.concatenate([oa, ob, oc], axis=2).reshape(bn, s_len, MIX)
    keep = min(C_PAST, s_len)
    state = (ka, va, kb, vb, logf, kc[:, s_len - keep:], vc[:, s_len - keep:])
    return out, state


def mixer_sample(hn, ca_k, ca_v, cb_k, cb_v, cb_logf, cc_k, cc_v, w_in, b_f, rel_bias):
    bn, t, _ = hn.shape
    past = ca_k.shape[1]
    keep = cc_k.shape[1]
    q, k, v, logf = project(hn, w_in, b_f)
    qa, qb, qc = split_heads(q)
    ka, kb, kc = split_heads(k)
    va, vb, vc = split_heads(v)
    q_pos = past + jnp.arange(t)
    k_pos = jnp.arange(past + t)
    oa = stick_breaking_block(qa, jnp.concatenate([ca_k, ka], 1), jnp.concatenate([ca_v, va], 1),
                              q_pos, k_pos)
    f_cum = jnp.cumsum(jnp.concatenate([cb_logf.astype(jnp.float32), logf], 1), axis=1)
    ob = forgetting_block(qb, jnp.concatenate([cb_k, kb], 1), jnp.concatenate([cb_v, vb], 1),
                          f_cum[:, past:], f_cum, q_pos, k_pos)
    kc_pos = jnp.concatenate([past - keep + jnp.arange(keep), q_pos])
    oc = chunk_band_block(qc, jnp.concatenate([cc_k, kc], 1), jnp.concatenate([cc_v, vc], 1),
                          q_pos, kc_pos, rel_bias)
    out = jnp.concatenate([oa, ob, oc], axis=2).reshape(bn, t, MIX)
    state = (ka, va, kb, vb, logf, kc, vc)
    return out, state


def hier_moe(h, w_rg, b_rg, w_re, b_re, w_gate, w_up, w_down):
    bn, t, d = h.shape
    xt = h.reshape(-1, d)
    n = xt.shape[0]
    lg = (xt @ w_rg).astype(jnp.float32) + b_rg.astype(jnp.float32)
    pg = jax.nn.softmax(lg, axis=-1)
    g = jnp.argmax(lg, axis=-1)
    le = ((xt @ w_re).astype(jnp.float32) + b_re.astype(jnp.float32)).reshape(n, N_GROUPS, EXP_PER_GROUP)
    le_sel = jnp.take_along_axis(le, g[:, None, None], axis=1)[:, 0]
    top_l, top_i = lax.top_k(le_sel, TOP_K)
    gate = jnp.take_along_axis(pg, g[:, None], axis=1) * jax.nn.softmax(top_l, axis=-1)
    eid = (g[:, None] * EXP_PER_GROUP + top_i).reshape(-1)
    tok = jnp.repeat(jnp.arange(n), TOP_K)
    wt = gate.reshape(-1)
    p_len = n * TOP_K
    order = jnp.argsort(eid)
    e_s, tok_s, w_s = eid[order], tok[order], wt[order]
    counts = jnp.bincount(eid, length=N_EXPERTS)
    padded = (counts + MOE_BLK - 1) // MOE_BLK * MOE_BLK
    start = jnp.cumsum(counts) - counts
    pend = jnp.cumsum(padded)
    pstart = pend - padded
    dest = pstart[e_s] + jnp.arange(p_len) - start[e_s]
    n_blocks = -(-p_len // MOE_BLK) + N_EXPERTS
    buf = jnp.zeros((n_blocks * MOE_BLK, d), h.dtype).at[dest].set(xt[tok_s])
    block_e = jnp.minimum(jnp.searchsorted(pend, jnp.arange(n_blocks) * MOE_BLK, side='right'),
                          N_EXPERTS - 1)

    def run(a):
        xb, e = a
        return (jax.nn.silu(xb @ w_gate[e]) * (xb @ w_up[e])) @ w_down[e]

    yb = lax.map(run, (buf.reshape(n_blocks, MOE_BLK, d), block_e)).reshape(-1, d)
    y_s = (yb[dest] * w_s[:, None]).astype(h.dtype)
    y = jnp.zeros((n, d), h.dtype).at[tok_s].add(y_s)
    return y.reshape(bn, t, d)


def setup_inputs(seed: int = 0) -> dict:
    key = jax.random.key(seed)
    ks = jax.random.split(key, 24)
    f32 = jnp.float32

    def nrm(k, shape, s):
        return jax.random.normal(k, shape, f32) * s

    c_keep = min(C_PAST, PAST_LEN)
    n_proj = 3 * MIX + H_B
    return {
        'x_prompt': nrm(ks[0], (BATCH, SEQ, D_MODEL), 1.0),
        'x_sample': nrm(ks[1], (DEC_BATCH, DEC_SEQ, D_MODEL), 1.0),
        'cache_a_k': nrm(ks[2], (DEPTH, DEC_BATCH, PAST_LEN, H_A, DH), 1.0),
        'cache_a_v': nrm(ks[3], (DEPTH, DEC_BATCH, PAST_LEN, H_A, DH), 1.0),
        'cache_b_k': nrm(ks[4], (DEPTH, DEC_BATCH, PAST_LEN, H_B, DH), 1.0),
        'cache_b_v': nrm(ks[5], (DEPTH, DEC_BATCH, PAST_LEN, H_B, DH), 1.0),
        'cache_b_logf': jax.nn.log_sigmoid(3.0 + nrm(ks[6], (DEPTH, DEC_BATCH, PAST_LEN, H_B), 1.0)),
        'cache_c_k': nrm(ks[7], (DEPTH, DEC_BATCH, c_keep, H_C, DH), 1.0),
        'cache_c_v': nrm(ks[8], (DEPTH, DEC_BATCH, c_keep, H_C, DH), 1.0),
        'norm1_g': 1.0 + nrm(ks[9], (DEPTH, D_MODEL), 0.05),
        'norm2_g': 1.0 + nrm(ks[10], (DEPTH, D_MODEL), 0.05),
        'w_in': nrm(ks[11], (DEPTH, D_MODEL, n_proj), D_MODEL ** -0.5),
        'b_f': jnp.linspace(1.0, 6.0, H_B, dtype=f32)[None, :] + nrm(ks[12], (DEPTH, H_B), 0.1),
        'rel_bias': nrm(ks[13], (DEPTH, H_C, 2 * REL_CLIP + 1), 0.5),
        'w_o': nrm(ks[14], (DEPTH, MIX, D_MODEL), MIX ** -0.5),
        'w_rg': nrm(ks[15], (DEPTH, D_MODEL, N_GROUPS), D_MODEL ** -0.5),
        'b_rg': nrm(ks[16], (DEPTH, N_GROUPS), 0.01),
        'w_re': nrm(ks[17], (DEPTH, D_MODEL, N_EXPERTS), D_MODEL ** -0.5),
        'b_re': nrm(ks[18], (DEPTH, N_EXPERTS), 0.01),
        'w_gate': nrm(ks[19], (DEPTH, N_EXPERTS, D_MODEL, D_EXPERT), D_MODEL ** -0.5),
        'w_up': nrm(ks[20], (DEPTH, N_EXPERTS, D_MODEL, D_EXPERT), D_MODEL ** -0.5),
        'w_down': nrm(ks[21], (DEPTH, N_EXPERTS, D_EXPERT, D_MODEL), D_EXPERT ** -0.5),
        'final_g': 1.0 + nrm(ks[22], (D_MODEL,), 0.05),
    }


def reference(x_prompt, x_sample, cache_a_k, cache_a_v, cache_b_k, cache_b_v, cache_b_logf,
              cache_c_k, cache_c_v, norm1_g, norm2_g, w_in, b_f, rel_bias, w_o,
              w_rg, b_rg, w_re, b_re, w_gate, w_up, w_down, final_g):
    xp, xs = x_prompt, x_sample
    sp, ss = [], []
    for l in range(DEPTH):
        mp, st_p = mixer_prompt(rmsnorm(xp, norm1_g[l]), w_in[l], b_f[l], rel_bias[l])
        ms, st_s = mixer_sample(rmsnorm(xs, norm1_g[l]), cache_a_k[l], cache_a_v[l], cache_b_k[l],
                                cache_b_v[l], cache_b_logf[l], cache_c_k[l], cache_c_v[l],
                                w_in[l], b_f[l], rel_bias[l])
        xp = xp + mp @ w_o[l]
        xs = xs + ms @ w_o[l]
        xp = xp + hier_moe(rmsnorm(xp, norm2_g[l]), w_rg[l], b_rg[l], w_re[l], b_re[l],
                           w_gate[l], w_up[l], w_down[l])
        xs = xs + hier_moe(rmsnorm(xs, norm2_g[l]), w_rg[l], b_rg[l], w_re[l], b_re[l],
                           w_gate[l], w_up[l], w_down[l])
        sp.append(st_p)
        ss.append(st_s)
    y_prompt = rmsnorm(xp, final_g)
    y_sample = rmsnorm(xs, final_g)

    def field(states, i):
        return jnp.stack([s[i] for s in states], axis=0)

    pa_k, pa_v, pb_k, pb_v = field(sp, 0), field(sp, 1), field(sp, 2), field(sp, 3)
    pb_logf, pc_k, pc_v = field(sp, 4), field(sp, 5), field(sp, 6)
    sa_k, sa_v, sb_k, sb_v = field(ss, 0), field(ss, 1), field(ss, 2), field(ss, 3)
    sb_logf, sc_k, sc_v = field(ss, 4), field(ss, 5), field(ss, 6)
    return (y_prompt, y_sample, pa_k, pa_v, pb_k, pb_v, pb_logf, pc_k, pc_v,
            sa_k, sa_v, sb_k, sb_v, sb_logf, sc_k, sc_v)
```

```python
import functools

import jax
import jax.numpy as jnp
from jax import lax
from jax.experimental import pallas as pl
from jax.experimental.pallas import tpu as pltpu

F32 = jnp.float32
BF16 = jnp.bfloat16

DH = 128
H_A, H_B, H_C = 4, 6, 6
N_HEADS = H_A + H_B + H_C
MIX = N_HEADS * DH
CHUNK = 64
C_PREV = 8
C_PAST = C_PREV * CHUNK
REL_CLIP = 128
N_GROUPS = 4
EXP_PER_GROUP = 8
N_EXPERTS = N_GROUPS * EXP_PER_GROUP
EPS = 1e-6
NEG_INF = -1e30
SCALE = DH ** -0.5

LANES = 128
PROJ_TM = 512
PROJ_TN = 512
MOE_BLK = 256
COMBINE_TM = 256
SB_CUTOFF = -104.0
VMEM_LIMIT = 56 * 1024 * 1024


def _nt_dot(a, b):
    return lax.dot_general(a, b, (((1,), (1,)), ((), ())), preferred_element_type=F32)


def _dot(a, b):
    return jnp.dot(a, b, preferred_element_type=F32)


def _split3(x):
    x1 = x.astype(BF16)
    r1 = x - x1.astype(F32)
    x2 = r1.astype(BF16)
    x3 = (r1 - x2.astype(F32)).astype(BF16)
    return x1, x2, x3


def _lane_pick(x, idx):
    lane = lax.broadcasted_iota(jnp.int32, x.shape, 1)
    return jnp.sum(jnp.where(lane == idx, x, 0.0), axis=1, keepdims=True)


def _proj_segments():
    widths = [("q", MIX), ("ka", H_A * DH), ("kb", H_B * DH), ("kc", H_C * DH),
              ("va", H_A * DH), ("vb", H_B * DH), ("vc", H_C * DH)]
    segs = []
    col = 0
    for name, w in widths:
        segs.append((name, col, col + w))
        col += w
    return segs


def _inproj_kernel(x_ref, g_ref, w_ref, wf_ref, bf_ref,
                   q_ref, ka_ref, kb_ref, kc_ref, va_ref, vb_ref, vc_ref, lf_ref,
                   xn_ref, *, n_col_blocks):
    j = pl.program_id(1)
    outs = {"q": q_ref, "ka": ka_ref, "kb": kb_ref, "kc": kc_ref,
            "va": va_ref, "vb": vb_ref, "vc": vc_ref}

    @pl.when(j == 0)
    def _():
        x = x_ref[...]
        ms = jnp.mean(x * x, axis=-1, keepdims=True)
        xn = (x * lax.rsqrt(ms + EPS) * g_ref[...]).astype(BF16)
        xn_ref[...] = xn
        lf = _dot(xn, wf_ref[...]) + bf_ref[...]
        lf_ref[...] = jax.nn.log_sigmoid(lf)

    acc = _dot(xn_ref[...], w_ref[...])
    segs = _proj_segments()
    for jj in range(n_col_blocks):
        c0, c1 = jj * PROJ_TN, (jj + 1) * PROJ_TN

        @pl.when(j == jj)
        def _(c0=c0, c1=c1):
            for name, s0, s1 in segs:
                lo, hi = max(c0, s0), min(c1, s1)
                if lo >= hi:
                    continue
                piece = acc[:, lo - c0:hi - c0]
                if name == "q":
                    outs[name][:, lo - s0:hi - s0] = (piece * SCALE).astype(BF16)
                else:
                    outs[name][:, lo - s0:hi - s0] = piece


def _inproj(x2d, row_block_off, n_rows, g, w_bf, wf_bf, bfp, layer, depth, prev):
    d = x2d.shape[1]
    tm = PROJ_TM
    nb = n_rows // tm
    ncb = (3 * MIX) // PROJ_TN
    kv_w = {"ka": H_A * DH, "kb": H_B * DH, "kc": H_C * DH,
            "va": H_A * DH, "vb": H_B * DH, "vc": H_C * DH}
    names = ["ka", "kb", "kc", "va", "vb", "vc"]
    out_shape = [jax.ShapeDtypeStruct((n_rows, MIX), BF16)]
    out_specs = [pl.BlockSpec((tm, MIX), lambda i, j: (i, 0))]
    for nm in names:
        out_shape.append(jax.ShapeDtypeStruct((depth, n_rows, kv_w[nm]), F32))
        out_specs.append(pl.BlockSpec((None, tm, kv_w[nm]), lambda i, j: (layer, i, 0)))
    out_shape.append(jax.ShapeDtypeStruct((n_rows, LANES), F32))
    out_specs.append(pl.BlockSpec((tm, LANES), lambda i, j: (i, 0)))

    in_specs = [
        pl.BlockSpec((tm, d), lambda i, j: (i + row_block_off, 0)),
        pl.BlockSpec((1, d), lambda i, j: (0, 0)),
        pl.BlockSpec((d, PROJ_TN), lambda i, j: (0, j)),
        pl.BlockSpec((d, LANES), lambda i, j: (0, 0)),
        pl.BlockSpec((1, LANES), lambda i, j: (0, 0)),
    ]
    args = [x2d, g, w_bf, wf_bf, bfp]
    aliases = {}
    if prev is not None:
        for t, nm in enumerate(names):
            in_specs.append(pl.BlockSpec(memory_space=pl.ANY))
            args.append(prev[nm])
            aliases[5 + t] = 1 + t

    def body(*refs):
        ins = refs[:5]
        outs = refs[len(args):len(args) + 8]
        scratch = refs[len(args) + 8:]
        _inproj_kernel(*ins, *outs, *scratch, n_col_blocks=ncb)

    res = pl.pallas_call(
        body,
        grid=(nb, ncb),
        in_specs=in_specs,
        out_specs=out_specs,
        out_shape=out_shape,
        scratch_shapes=[pltpu.VMEM((tm, d), BF16)],
        input_output_aliases=aliases,
        compiler_params=pltpu.CompilerParams(
            dimension_semantics=("parallel", "arbitrary"), vmem_limit_bytes=VMEM_LIMIT),
    )(*args)
    q = res[0]
    stacks = dict(zip(names, res[1:7]))
    return q, stacks, res[7]


def _cumsum_kernel(x_ref, f_ref, carry_ref, *, tc):
    @pl.when(pl.program_id(1) == 0)
    def _():
        carry_ref[...] = jnp.zeros_like(carry_ref)

    x = x_ref[...]
    row = lax.broadcasted_iota(jnp.int32, (tc, tc), 0)
    col = lax.broadcasted_iota(jnp.int32, (tc, tc), 1)
    lower = jnp.where(row >= col, 1.0, 0.0).astype(BF16)
    x1, x2, x3 = _split3(x)
    cs = _dot(lower, x3) + _dot(lower, x2) + _dot(lower, x1) + carry_ref[...]
    f_ref[...] = cs
    carry_ref[...] = cs[tc - 1:tc, :]


def _cumsum_seq(x, tc):
    b, t, _ = x.shape
    return pl.pallas_call(
        functools.partial(_cumsum_kernel, tc=tc),
        grid=(b, t // tc),
        in_specs=[pl.BlockSpec((None, tc, LANES), lambda i, j: (i, j, 0))],
        out_specs=pl.BlockSpec((None, tc, LANES), lambda i, j: (i, j, 0)),
        out_shape=jax.ShapeDtypeStruct((b, t, LANES), F32),
        scratch_shapes=[pltpu.VMEM((1, LANES), F32)],
        compiler_params=pltpu.CompilerParams(
            dimension_semantics=("parallel", "arbitrary")),
    )(x)


def _attn_a_kernel(q_ref, k_ref, v_ref, o_ref, *, t, off_blocks):
    gi = pl.program_id(2) + off_blocks
    q = q_ref[...]
    row = lax.broadcasted_iota(jnp.int32, (t, t), 0)
    col = lax.broadcasted_iota(jnp.int32, (t, t), 1)
    later = jnp.where(row > col, 1.0, 0.0).astype(BF16)

    def cond(carry):
        j, c, _ = carry
        return jnp.logical_and(j >= 0, jnp.max(c) > SB_CUTOFF)

    def body(carry):
        j, c, acc = carry
        start = pl.multiple_of(j * t, t)
        kb = k_ref[pl.ds(start, t), :].astype(BF16)
        vb = v_ref[pl.ds(start, t), :].astype(BF16)
        z = _nt_dot(q, kb)
        sp = jnp.maximum(z, 0.0) + jnp.log1p(jnp.exp(-jnp.abs(z)))
        mask = col < row + jnp.where(j < gi, t, 0)
        lm = jnp.where(mask, -sp, 0.0)
        hi = lm.astype(BF16)
        lo = (lm - hi.astype(F32)).astype(BF16)
        suffix = _dot(lo, later) + _dot(hi, later)
        w = jnp.where(mask, jnp.exp(z - sp + suffix + c), 0.0)
        acc = acc + _dot(w.astype(BF16), vb)
        c = c + jnp.sum(lm, axis=1, keepdims=True)
        return j - 1, c, acc

    init = (gi, jnp.zeros((t, 1), F32), jnp.zeros((t, DH), F32))
    _, _, acc = lax.while_loop(cond, body, init)
    o_ref[...] = acc.astype(o_ref.dtype)


def _attn_a(q3, head_off, k4, v4, layer, t, off_blocks):
    b, tq, _ = q3.shape
    tk = k4.shape[2]
    return pl.pallas_call(
        functools.partial(_attn_a_kernel, t=t, off_blocks=off_blocks),
        grid=(b, H_A, tq // t),
        in_specs=[pl.BlockSpec((None, t, DH), lambda bi, h, i: (bi, i, head_off + h)),
                  pl.BlockSpec((None, None, tk, DH), lambda bi, h, i: (layer, bi, 0, h)),
                  pl.BlockSpec((None, None, tk, DH), lambda bi, h, i: (layer, bi, 0, h))],
        out_specs=pl.BlockSpec((None, t, DH), lambda bi, h, i: (bi, i, h)),
        out_shape=jax.ShapeDtypeStruct((b, tq, H_A * DH), BF16),
        compiler_params=pltpu.CompilerParams(
            dimension_semantics=("parallel", "parallel", "arbitrary"),
            vmem_limit_bytes=VMEM_LIMIT),
    )(q3, k4, v4)


def _attn_b_kernel(q_ref, k_ref, v_ref, fq_ref, fk_ref, o_ref, *, tq, tk, q_off):
    h = pl.program_id(1)
    single = k_ref.shape[0] == tk
    q_pos0 = q_off + pl.program_id(2) * tq
    n_full = 0 if single else q_pos0 // tk
    q = q_ref[...]
    fq = _lane_pick(fq_ref[...], h)

    def step(j, carry, masked):
        m, l, acc = carry
        start = 0 if single else pl.multiple_of(j * tk, tk)
        kb = k_ref[pl.ds(start, tk), :].astype(BF16)
        vb = v_ref[pl.ds(start, tk), :].astype(BF16)
        s = _nt_dot(q, kb) + fq - fk_ref[:, pl.ds(start, tk)]
        if masked:
            row = lax.broadcasted_iota(jnp.int32, (tq, tk), 0)
            col = lax.broadcasted_iota(jnp.int32, (tq, tk), 1)
            s = jnp.where(col <= row + (q_pos0 - start), s, NEG_INF)
        m_new = jnp.maximum(m, jnp.max(s, axis=1, keepdims=True))
        alpha = jnp.exp(m - m_new)
        p = jnp.exp(s - m_new)
        l = alpha * l + jnp.sum(p, axis=1, keepdims=True)
        acc = alpha * acc + _dot(p.astype(BF16), vb)
        return m_new, l, acc

    carry = (jnp.full((tq, 1), NEG_INF, F32), jnp.zeros((tq, 1), F32), jnp.zeros((tq, DH), F32))
    if not single:
        carry = lax.fori_loop(0, n_full, lambda j, c: step(j, c, False), carry)
    _, l, acc = step(n_full, carry, True)
    o_ref[...] = (acc / l).astype(o_ref.dtype)


def _attn_b(q3, head_off, k4, v4, layer, fcol, frow, tq, tk, q_off):
    b, tq_all, _ = q3.shape
    tk_all = k4.shape[2]
    assert tk % tq == 0 and q_off % tq == 0 and tk_all % tk == 0
    return pl.pallas_call(
        functools.partial(_attn_b_kernel, tq=tq, tk=tk, q_off=q_off),
        grid=(b, H_B, tq_all // tq),
        in_specs=[pl.BlockSpec((None, tq, DH), lambda bi, h, i: (bi, i, head_off + h)),
                  pl.BlockSpec((None, None, tk_all, DH), lambda bi, h, i: (layer, bi, 0, h)),
                  pl.BlockSpec((None, None, tk_all, DH), lambda bi, h, i: (layer, bi, 0, h)),
                  pl.BlockSpec((None, tq, LANES), lambda bi, h, i: (bi, i + q_off // tq, 0)),
                  pl.BlockSpec((None, None, 1, tk_all), lambda bi, h, i: (bi, h, 0, 0))],
        out_specs=pl.BlockSpec((None, tq, DH), lambda bi, h, i: (bi, i, h)),
        out_shape=jax.ShapeDtypeStruct((b, tq_all, H_B * DH), BF16),
        compiler_params=pltpu.CompilerParams(
            dimension_semantics=("parallel", "parallel", "arbitrary"),
            vmem_limit_bytes=VMEM_LIMIT),
    )(q3, k4, v4, fcol, frow)


def _band_table(rel_bias, r):
    w = C_PAST + r
    qi = jnp.arange(r)[:, None]
    kj = jnp.arange(w)[None, :] - C_PAST
    rel = jnp.clip(qi - kj, -REL_CLIP, REL_CLIP) + REL_CLIP
    cq = qi // CHUNK
    ck = jnp.floor_divide(kj, CHUNK)
    ok = (ck <= cq) & (ck >= cq - C_PREV)
    bias = rel_bias.astype(F32)[:, rel]
    return jnp.where(ok[None], bias, NEG_INF)


def _attn_c_kernel(q_ref, k_ref, v_ref, tab_ref, o_ref, *, r, pad):
    w = C_PAST + r
    start = pl.multiple_of(pl.program_id(2) * r, r)
    q = q_ref[...]
    kb = k_ref[pl.ds(start, w), :].astype(BF16)
    vb = v_ref[pl.ds(start, w), :].astype(BF16)
    s = _nt_dot(q, kb) + tab_ref[...]
    if pad > 0:
        col = lax.broadcasted_iota(jnp.int32, (r, w), 1)
        s = jnp.where(start + col >= pad, s, NEG_INF)
    m = jnp.max(s, axis=1, keepdims=True)
    p = jnp.exp(s - m)
    l = jnp.sum(p, axis=1, keepdims=True)
    o_ref[...] = (_dot(p.astype(BF16), vb) / l).astype(o_ref.dtype)


def _attn_c(q3, head_off, k4, v4, layer, table, r, pad):
    b, tq, _ = q3.shape
    tk = k4.shape[2]
    w = C_PAST + r
    return pl.pallas_call(
        functools.partial(_attn_c_kernel, r=r, pad=pad),
        grid=(b, H_C, tq // r),
        in_specs=[pl.BlockSpec((None, r, DH), lambda bi, h, i: (bi, i, head_off + h)),
                  pl.BlockSpec((None, None, tk, DH), lambda bi, h, i: (layer, bi, 0, h)),
                  pl.BlockSpec((None, None, tk, DH), lambda bi, h, i: (layer, bi, 0, h)),
                  pl.BlockSpec((None, r, w), lambda bi, h, i: (h, 0, 0))],
        out_specs=pl.BlockSpec((None, r, DH), lambda bi, h, i: (bi, i, h)),
        out_shape=jax.ShapeDtypeStruct((b, tq, H_C * DH), BF16),
        compiler_params=pltpu.CompilerParams(
            dimension_semantics=("parallel", "parallel", "arbitrary"),
            vmem_limit_bytes=VMEM_LIMIT),
    )(q3, k4, v4, table)


def _outproj_kernel(x_ref, oa_ref, ob_ref, oc_ref, wo_ref, g_ref, wr_ref, br_ref,
                    h_ref, hn_ref, route_ref):
    a0, a1 = H_A * DH, (H_A + H_B) * DH
    acc = _dot(oa_ref[...], wo_ref[0:a0, :])
    acc = acc + _dot(ob_ref[...], wo_ref[a0:a1, :])
    acc = acc + _dot(oc_ref[...], wo_ref[a1:MIX, :])
    h = x_ref[...] + acc
    h_ref[...] = h
    ms = jnp.mean(h * h, axis=-1, keepdims=True)
    hn = h * lax.rsqrt(ms + EPS) * g_ref[...]
    hn_ref[...] = hn

    h1, h2, _ = _split3(hn)
    w1 = wr_ref[0]
    w2 = wr_ref[1]
    lg = _dot(h2, w1) + _dot(h1, w2) + _dot(h1, w1) + br_ref[...]

    lane = lax.broadcasted_iota(jnp.int32, lg.shape, 1)
    lane_f = lane.astype(F32)
    is_g = lane < N_GROUPS
    lgm = jnp.where(is_g, lg, NEG_INF)
    gmax = jnp.max(lgm, axis=1, keepdims=True)
    gidx = jnp.min(jnp.where(lgm == gmax, lane_f, float(LANES)), axis=1, keepdims=True)
    pg = 1.0 / jnp.sum(jnp.where(is_g, jnp.exp(lgm - gmax), 0.0), axis=1, keepdims=True)
    grp = jnp.floor((lane_f - N_GROUPS) * (1.0 / EXP_PER_GROUP))
    in_grp = (lane >= N_GROUPS) & (lane < N_GROUPS + N_EXPERTS) & (grp == gidx)
    le = jnp.where(in_grp, lg, NEG_INF)
    t1 = jnp.max(le, axis=1, keepdims=True)
    i1 = jnp.min(jnp.where(le == t1, lane_f, float(LANES)), axis=1, keepdims=True)
    le2 = jnp.where(lane_f == i1, NEG_INF, le)
    t2 = jnp.max(le2, axis=1, keepdims=True)
    i2 = jnp.min(jnp.where(le2 == t2, lane_f, float(LANES)), axis=1, keepdims=True)
    e = jnp.exp(t2 - t1)
    g1 = pg / (1.0 + e)
    g2 = pg * e / (1.0 + e)
    route = jnp.where(lane == 0, i1 - N_GROUPS,
                      jnp.where(lane == 1, i2 - N_GROUPS,
                                jnp.where(lane == 2, g1, jnp.where(lane == 3, g2, 0.0))))
    route_ref[...] = route


def _outproj(x2d, x_block_off, oa, ob, oc, wo_bf, g, wr2, br, n_total, out_block_off, prev):
    n_rows, d = oa.shape[0], x2d.shape[1]
    tm = PROJ_TM
    nb = n_rows // tm
    in_specs = [
        pl.BlockSpec((tm, d), lambda i: (i + x_block_off, 0)),
        pl.BlockSpec((tm, H_A * DH), lambda i: (i, 0)),
        pl.BlockSpec((tm, H_B * DH), lambda i: (i, 0)),
        pl.BlockSpec((tm, H_C * DH), lambda i: (i, 0)),
        pl.BlockSpec((MIX, d), lambda i: (0, 0)),
        pl.BlockSpec((1, d), lambda i: (0, 0)),
        pl.BlockSpec((2, d, LANES), lambda i: (0, 0, 0)),
        pl.BlockSpec((1, LANES), lambda i: (0, 0)),
    ]
    args = [x2d, oa, ob, oc, wo_bf, g, wr2, br]
    aliases = {}
    if prev is not None:
        for t, arr in enumerate(prev):
            in_specs.append(pl.BlockSpec(memory_space=pl.ANY))
            args.append(arr)
            aliases[8 + t] = t

    def body(*refs):
        _outproj_kernel(*refs[:8], *refs[len(args):len(args) + 3])

    return pl.pallas_call(
        body,
        grid=(nb,),
        in_specs=in_specs,
        out_specs=[pl.BlockSpec((tm, d), lambda i: (i + out_block_off, 0)),
                   pl.BlockSpec((tm, d), lambda i: (i + out_block_off, 0)),
                   pl.BlockSpec((tm, LANES), lambda i: (i + out_block_off, 0))],
        out_shape=[jax.ShapeDtypeStruct((n_total, d), F32),
                   jax.ShapeDtypeStruct((n_total, d), F32),
                   jax.ShapeDtypeStruct((n_total, LANES), F32)],
        input_output_aliases=aliases,
        compiler_params=pltpu.CompilerParams(
            dimension_semantics=("parallel",), vmem_limit_bytes=VMEM_LIMIT),
    )(*args)


def _moe_kernel(be_ref, nused_ref, src_ref, hn_hbm, wg_ref, wu_ref, wd_ref, yb_ref, xbuf, sem):
    del be_ref
    i = pl.program_id(0)
    blk = xbuf.shape[0]

    @pl.when(i < nused_ref[0])
    def _():
        def issue(r, carry):
            tok = src_ref[0, 0, r]
            pltpu.make_async_copy(hn_hbm.at[pl.ds(tok, 1), :], xbuf.at[pl.ds(r, 1), :], sem).start()
            return carry

        lax.fori_loop(0, blk, issue, 0)
        pltpu.make_async_copy(hn_hbm.at[pl.ds(0, blk), :], xbuf, sem).wait()
        xb = xbuf[...].astype(BF16)
        gate = _dot(xb, wg_ref[...])
        up = _dot(xb, wu_ref[...])
        act = (gate * jax.nn.sigmoid(gate) * up).astype(BF16)
        yb_ref[...] = _dot(act, wd_ref[...])

    @pl.when(i >= nused_ref[0])
    def _():
        yb_ref[...] = jnp.zeros_like(yb_ref)


def _moe(hn, src, block_e, nused, wg_bf, wu_bf, wd_bf, n_blocks):
    d = hn.shape[1]
    de = wg_bf.shape[2]
    grid_spec = pltpu.PrefetchScalarGridSpec(
        num_scalar_prefetch=2,
        grid=(n_blocks,),
        in_specs=[
            pl.BlockSpec((1, 1, MOE_BLK), lambda i, be, nu: (i, 0, 0), memory_space=pltpu.SMEM),
            pl.BlockSpec(memory_space=pl.ANY),
            pl.BlockSpec((None, d, de), lambda i, be, nu: (be[i], 0, 0)),
            pl.BlockSpec((None, d, de), lambda i, be, nu: (be[i], 0, 0)),
            pl.BlockSpec((None, de, d), lambda i, be, nu: (be[i], 0, 0)),
        ],
        out_specs=pl.BlockSpec((MOE_BLK, d), lambda i, be, nu: (i, 0)),
        scratch_shapes=[pltpu.VMEM((MOE_BLK, d), F32), pltpu.SemaphoreType.DMA(())],
    )
    return pl.pallas_call(
        _moe_kernel,
        grid_spec=grid_spec,
        out_shape=jax.ShapeDtypeStruct((n_blocks * MOE_BLK, d), F32),
        compiler_params=pltpu.CompilerParams(
            dimension_semantics=("arbitrary",), vmem_limit_bytes=VMEM_LIMIT),
    )(block_e, nused, src.reshape(n_blocks, 1, MOE_BLK), hn, wg_bf, wu_bf, wd_bf)


def _dispatch(eid, n_blocks):
    p = eid.shape[0] * eid.shape[1]
    e = eid.reshape(-1)
    onehot = (e[:, None] == jnp.arange(N_EXPERTS, dtype=jnp.int32)[None, :]).astype(jnp.int32)
    cs = jnp.cumsum(onehot, axis=0)
    rank = jnp.take_along_axis(cs, e[:, None], axis=1)[:, 0] - 1
    counts = cs[-1]
    padded = (counts + MOE_BLK - 1) // MOE_BLK * MOE_BLK
    pend = jnp.cumsum(padded)
    pstart = pend - padded
    dest = (pstart[e] + rank).astype(jnp.int32)
    src = jnp.zeros((n_blocks * MOE_BLK,), jnp.int32).at[dest].set(
        jnp.arange(p, dtype=jnp.int32) // eid.shape[1])
    block_e = jnp.minimum(
        jnp.searchsorted(pend, jnp.arange(n_blocks, dtype=jnp.int32) * MOE_BLK, side="right"),
        N_EXPERTS - 1).astype(jnp.int32)
    nused = (pend[-1:] // MOE_BLK).astype(jnp.int32)
    return dest, src, block_e, nused


def _combine_kernel(pos_ref, h_ref, route_ref, yb_hbm, g_ref, o_ref, buf, sem, *, final):
    tm = h_ref.shape[0]

    def issue(r, carry):
        p0 = pos_ref[0, 0, 2 * r]
        p1 = pos_ref[0, 0, 2 * r + 1]
        pltpu.make_async_copy(yb_hbm.at[pl.ds(p0, 1), :], buf.at[0, pl.ds(r, 1), :], sem).start()
        pltpu.make_async_copy(yb_hbm.at[pl.ds(p1, 1), :], buf.at[1, pl.ds(r, 1), :], sem).start()
        return carry

    lax.fori_loop(0, tm, issue, 0)
    pltpu.make_async_copy(yb_hbm.at[pl.ds(0, tm), :], buf.at[0], sem).wait()
    pltpu.make_async_copy(yb_hbm.at[pl.ds(0, tm), :], buf.at[1], sem).wait()
    route = route_ref[...]
    g1 = _lane_pick(route, 2)
    g2 = _lane_pick(route, 3)
    y = h_ref[...] + (g1 * buf[0] + g2 * buf[1])
    if final:
        ms = jnp.mean(y * y, axis=-1, keepdims=True)
        y = y * lax.rsqrt(ms + EPS) * g_ref[...]
    o_ref[...] = y


def _combine(pos3, h, route, yb, g, block_off, n_rows, final):
    d = h.shape[1]
    tm = COMBINE_TM
    return pl.pallas_call(
        functools.partial(_combine_kernel, final=final),
        grid=(n_rows // tm,),
        in_specs=[
            pl.BlockSpec((1, 1, 2 * tm), lambda i: (i + block_off, 0, 0), memory_space=pltpu.SMEM),
            pl.BlockSpec((tm, d), lambda i: (i + block_off, 0)),
            pl.BlockSpec((tm, LANES), lambda i: (i + block_off, 0)),
            pl.BlockSpec(memory_space=pl.ANY),
            pl.BlockSpec((1, d), lambda i: (0, 0)),
        ],
        out_specs=pl.BlockSpec((tm, d), lambda i: (i, 0)),
        out_shape=jax.ShapeDtypeStruct((n_rows, d), F32),
        scratch_shapes=[pltpu.VMEM((2, tm, d), F32), pltpu.SemaphoreType.DMA(())],
        compiler_params=pltpu.CompilerParams(
            dimension_semantics=("arbitrary",), vmem_limit_bytes=VMEM_LIMIT),
    )(pos3, h, route, yb, g)


def _pad_lanes(a):
    return jnp.pad(a, [(0, 0)] * (a.ndim - 1) + [(0, LANES - a.shape[-1])])


def kernel(x_prompt, x_sample, cache_a_k, cache_a_v, cache_b_k, cache_b_v, cache_b_logf,
           cache_c_k, cache_c_v, norm1_g, norm2_g, w_in, b_f, rel_bias, w_o,
           w_rg, b_rg, w_re, b_re, w_gate, w_up, w_down, final_g):
    bp, sp, d = x_prompt.shape
    bs, ss, _ = x_sample.shape
    depth = w_in.shape[0]
    past = cache_a_k.shape[2]
    keep = cache_c_k.shape[2]
    n_p, n_s = bp * sp, bs * ss
    n_tot = n_p + n_s
    assert d == MIX and n_p % PROJ_TM == 0 and n_s % PROJ_TM == 0
    assert ss == CHUNK and past % ss == 0 and keep == C_PAST

    t_p = min(256, sp)
    r_p = min(256, sp)
    n_blocks = (2 * n_tot) // MOE_BLK + N_EXPERTS

    xs_p = (x_prompt.reshape(n_p, d), 0)
    xs_s = (x_sample.reshape(n_s, d), 0)
    stacks_p = stacks_s = None
    logf_p, logf_s = [], []
    y_prompt = y_sample = None

    for l in range(depth):
        w_bf = w_in[l, :, :3 * MIX].astype(BF16)
        wf_bf = _pad_lanes(w_in[l, :, 3 * MIX:]).astype(BF16)
        bfp = _pad_lanes(b_f[l][None, :].astype(F32))
        g1 = norm1_g[l][None, :].astype(F32)
        q_p, stacks_p, lf_p = _inproj(xs_p[0], xs_p[1], n_p, g1, w_bf, wf_bf, bfp, l, depth, stacks_p)
        q_s, stacks_s, lf_s = _inproj(xs_s[0], xs_s[1], n_s, g1, w_bf, wf_bf, bfp, l, depth, stacks_s)
        logf_p.append(lf_p[:, :H_B].reshape(bp, sp, H_B))
        logf_s.append(lf_s[:, :H_B].reshape(bs, ss, H_B))
        q_p3 = q_p.reshape(bp, sp, MIX)
        q_s3 = q_s.reshape(bs, ss, MIX)

        def view_p(nm):
            return stacks_p[nm].reshape(depth, bp, sp, -1)

        def new_s(nm):
            return stacks_s[nm][l].reshape(bs, ss, -1)

        def cat_s(cache, nm):
            c = cache[l].reshape(bs, cache.shape[2], -1)
            return jnp.concatenate([c, new_s(nm)], axis=1)[None]

        oa_p = _attn_a(q_p3, 0, view_p("ka"), view_p("va"), l, t_p, 0)
        oa_s = _attn_a(q_s3, 0, cat_s(cache_a_k, "ka"), cat_s(cache_a_v, "va"), 0, ss, past // ss)

        def row_layout(fcol):
            return jnp.swapaxes(fcol[:, :, :H_B], 1, 2)[:, :, None, :]

        fcol_p = _cumsum_seq(lf_p.reshape(bp, sp, LANES), min(512, sp))
        lf_cat = jnp.concatenate([_pad_lanes(cache_b_logf[l].astype(F32)),
                                  lf_s.reshape(bs, ss, LANES)], axis=1)
        t_cat = past + ss
        tc_s = t_cat // 3 if (t_cat % 24 == 0) else t_cat
        fcol_s = _cumsum_seq(lf_cat, tc_s)
        ob_p = _attn_b(q_p3, H_A, view_p("kb"), view_p("vb"), l, fcol_p, row_layout(fcol_p),
                       t_p, t_p, 0)
        ob_s = _attn_b(q_s3, H_A, cat_s(cache_b_k, "kb"), cat_s(cache_b_v, "vb"), 0, fcol_s,
                       row_layout(fcol_s), ss, t_cat, past)

        zpad = ((0, 0), (C_PAST, 0), (0, 0))
        kc_p = jnp.pad(stacks_p["kc"][l].reshape(bp, sp, -1), zpad)[None]
        vc_p = jnp.pad(stacks_p["vc"][l].reshape(bp, sp, -1), zpad)[None]
        oc_p = _attn_c(q_p3, H_A + H_B, kc_p, vc_p, 0, _band_table(rel_bias[l], r_p), r_p, C_PAST)
        oc_s = _attn_c(q_s3, H_A + H_B, cat_s(cache_c_k, "kc"), cat_s(cache_c_v, "vc"), 0,
                       _band_table(rel_bias[l], ss), ss, 0)

        wo_bf = w_o[l].astype(BF16)
        g2 = norm2_g[l][None, :].astype(F32)
        wr = _pad_lanes(jnp.concatenate([w_rg[l], w_re[l]], axis=1).astype(F32))
        wr1 = wr.astype(BF16)
        wr2 = jnp.stack([wr1, (wr - wr1.astype(F32)).astype(BF16)])
        br = _pad_lanes(jnp.concatenate([b_rg[l], b_re[l]])[None, :].astype(F32))
        shared = _outproj(xs_p[0], xs_p[1], oa_p.reshape(n_p, -1), ob_p.reshape(n_p, -1),
                          oc_p.reshape(n_p, -1), wo_bf, g2, wr2, br, n_tot, 0, None)
        h, hn, route = _outproj(xs_s[0], xs_s[1], oa_s.reshape(n_s, -1), ob_s.reshape(n_s, -1),
                                oc_s.reshape(n_s, -1), wo_bf, g2, wr2, br, n_tot,
                                n_p // PROJ_TM, shared)

        eid = route[:, :2].astype(jnp.int32)
        dest, src, block_e, nused = _dispatch(eid, n_blocks)
        yb = _moe(hn, src, block_e, nused, w_gate[l].astype(BF16), w_up[l].astype(BF16),
                  w_down[l].astype(BF16), n_blocks)
        pos3 = dest.reshape(n_tot // COMBINE_TM, 1, 2 * COMBINE_TM)
        if l + 1 < depth:
            y = _combine(pos3, h, route, yb, final_g[None, :].astype(F32), 0, n_tot, False)
            xs_p = (y, 0)
            xs_s = (y, n_p // PROJ_TM)
        else:
            fg = final_g[None, :].astype(F32)
            y_prompt = _combine(pos3, h, route, yb, fg, 0, n_p, True).reshape(bp, sp, d)
            y_sample = _combine(pos3, h, route, yb, fg, n_p // COMBINE_TM, n_s, True).reshape(bs, ss, d)

    def st_p(nm, h):
        return stacks_p[nm].reshape(depth, bp, sp, h, DH)

    def st_s(nm, h):
        return stacks_s[nm].reshape(depth, bs, ss, h, DH)

    keep_p = min(C_PAST, sp)
    return (y_prompt, y_sample,
            st_p("ka", H_A), st_p("va", H_A), st_p("kb", H_B), st_p("vb", H_B),
            jnp.stack(logf_p, axis=0),
            st_p("kc", H_C)[:, :, sp - keep_p:], st_p("vc", H_C)[:, :, sp - keep_p:],
            st_s("ka", H_A), st_s("va", H_A), st_s("kb", H_B), st_s("vb", H_B),
            jnp.stack(logf_s, axis=0),
            st_s("kc", H_C), st_s("vc", H_C))
```

```python
import functools
import math

import jax
import jax.numpy as jnp
from jax import lax
from jax.experimental import pallas as pl
from jax.experimental.pallas import tpu as pltpu

F32 = jnp.float32
BF16 = jnp.bfloat16

DH = 128
H_A, H_B, H_C = 4, 6, 6
N_HEADS = H_A + H_B + H_C
MIX = N_HEADS * DH
CHUNK = 64
C_PREV = 8
C_PAST = C_PREV * CHUNK
REL_CLIP = 128
N_GROUPS = 4
EXP_PER_GROUP = 8
N_EXPERTS = N_GROUPS * EXP_PER_GROUP
EPS = 1e-6
NEG_INF = -1e30
SCALE = DH ** -0.5
LOG2E = math.log2(math.e)

LANES = 128
SUBLANES = 8
PROJ_TM = 512
PROJ_TN = 512
MOE_BLK = 256
COMBINE_TM = 256
SB_CUTOFF = -104.0
VMEM_LIMIT = 56 * 1024 * 1024


def _nt_dot(a, b):
    return lax.dot_general(a, b, (((1,), (1,)), ((), ())), preferred_element_type=F32)


def _dot(a, b):
    return jnp.dot(a, b, preferred_element_type=F32)


def _split3(x):
    x1 = x.astype(BF16)
    r1 = x - x1.astype(F32)
    x2 = r1.astype(BF16)
    x3 = (r1 - x2.astype(F32)).astype(BF16)
    return x1, x2, x3


def _lane_pick(x, idx):
    lane = lax.broadcasted_iota(jnp.int32, x.shape, 1)
    return jnp.sum(jnp.where(lane == idx, x, 0.0), axis=1, keepdims=True)


def _inproj_kernel(x_ref, g_ref, w_ref, wf_ref, bf_ref,
                   q_ref, ka_ref, va_ref, kb_ref, vb_ref, kc_ref, vc_ref, k16_ref, v16_ref, lf_ref,
                   xn_ref, *, n_col_blocks, rb):
    j = pl.program_id(1)
    tm = x_ref.shape[0]
    nbb = tm // rb
    heads_per_blk = PROJ_TN // DH

    @pl.when(j == 0)
    def _():
        x = x_ref[...]
        ms = jnp.mean(x * x, axis=-1, keepdims=True)
        xn = (x * lax.rsqrt(ms + EPS) * g_ref[...]).astype(BF16)
        xn_ref[...] = xn
        lf = _dot(xn, wf_ref[...]) + bf_ref[...]
        lf_ref[...] = jax.nn.log_sigmoid(lf)

    acc = _dot(xn_ref[...], w_ref[...])

    def write_kv(jj, tok_ref, b_ref, c_ref, h16_ref):
        for u in range(heads_per_blk):
            hh = (jj * heads_per_blk + u) % N_HEADS
            piece = acc[:, u * DH:(u + 1) * DH]
            if hh < H_A:
                tok_ref[:, hh * DH:(hh + 1) * DH] = piece
            for bb in range(nbb):
                rows = piece[bb * rb:(bb + 1) * rb]
                h16_ref[bb, hh] = rows.astype(BF16)
                if H_A <= hh < H_A + H_B:
                    b_ref[bb, hh - H_A] = rows
                elif hh >= H_A + H_B:
                    c_ref[bb, hh - H_A - H_B] = rows

    blocks_per_part = MIX // PROJ_TN
    for jj in range(n_col_blocks):
        @pl.when(j == jj)
        def _(jj=jj):
            part = jj // blocks_per_part
            if part == 0:
                sc = SCALE if (jj + 1) * heads_per_blk <= H_A else SCALE * LOG2E
                q_ref[:, jj * PROJ_TN:(jj + 1) * PROJ_TN] = (acc * sc).astype(BF16)
            elif part == 1:
                write_kv(jj, ka_ref, kb_ref, kc_ref, k16_ref)
            else:
                write_kv(jj, va_ref, vb_ref, vc_ref, v16_ref)


_KV_NAMES = ["ka", "va", "kb", "vb", "kc", "vc"]


def _inproj(x2d, row_block_off, nb_batch, seq, g, w_bf, wf_bf, bfp, layer, depth, prev):
    assert H_A * DH == PROJ_TN
    d = x2d.shape[1]
    tm = PROJ_TM
    n_rows = nb_batch * seq
    nb = n_rows // tm
    ncb = (3 * MIX) // PROJ_TN
    rb = min(seq, tm)
    nbb = tm // rb
    spb = seq // rb

    def hm_map(i, j):
        return (layer, i // spb, 0, i % spb, 0)

    def hm16_map(i, j):
        return (i // spb, 0, i % spb, 0)

    out_shape = [jax.ShapeDtypeStruct((n_rows, MIX), BF16)]
    out_specs = [pl.BlockSpec((tm, MIX), lambda i, j: (i, 0))]
    for nm in _KV_NAMES:
        if nm in ("ka", "va"):
            out_shape.append(jax.ShapeDtypeStruct((depth, n_rows, H_A * DH), F32))
            out_specs.append(pl.BlockSpec((None, tm, H_A * DH), lambda i, j: (layer, i, 0)))
        else:
            hn = H_B if nm[1] == "b" else H_C
            out_shape.append(jax.ShapeDtypeStruct((depth, nb_batch, hn, seq, DH), F32))
            out_specs.append(pl.BlockSpec((None, nbb, hn, rb, DH), hm_map))
    for _ in range(2):
        out_shape.append(jax.ShapeDtypeStruct((nb_batch, N_HEADS, seq, DH), BF16))
        out_specs.append(pl.BlockSpec((nbb, N_HEADS, rb, DH), hm16_map))
    out_shape.append(jax.ShapeDtypeStruct((n_rows, LANES), F32))
    out_specs.append(pl.BlockSpec((tm, LANES), lambda i, j: (i, 0)))
    n_out = len(out_shape)

    in_specs = [
        pl.BlockSpec((tm, d), lambda i, j: (i + row_block_off, 0)),
        pl.BlockSpec((1, d), lambda i, j: (0, 0)),
        pl.BlockSpec((d, PROJ_TN), lambda i, j: (0, j)),
        pl.BlockSpec((d, LANES), lambda i, j: (0, 0)),
        pl.BlockSpec((1, LANES), lambda i, j: (0, 0)),
    ]
    args = [x2d, g, w_bf, wf_bf, bfp]
    n_in = len(args)
    aliases = {}
    if prev is not None:
        for t, nm in enumerate(_KV_NAMES):
            in_specs.append(pl.BlockSpec(memory_space=pl.ANY))
            args.append(prev[nm])
            aliases[n_in + t] = 1 + t

    def body(*refs):
        _inproj_kernel(*refs[:n_in], *refs[len(args):len(args) + n_out], *refs[len(args) + n_out:],
                       n_col_blocks=ncb, rb=rb)

    res = pl.pallas_call(
        body,
        grid=(nb, ncb),
        in_specs=in_specs,
        out_specs=out_specs,
        out_shape=out_shape,
        scratch_shapes=[pltpu.VMEM((tm, d), BF16)],
        input_output_aliases=aliases,
        compiler_params=pltpu.CompilerParams(
            dimension_semantics=("parallel", "arbitrary"), vmem_limit_bytes=VMEM_LIMIT),
    )(*args)
    stacks = dict(zip(_KV_NAMES, res[1:7]))
    return res[0], stacks, res[7], res[8], res[9]


def _cumsum_kernel(x_ref, f_ref, carry_ref, *, tc):
    @pl.when(pl.program_id(1) == 0)
    def _():
        carry_ref[...] = jnp.zeros_like(carry_ref)

    x = x_ref[...]
    row = lax.broadcasted_iota(jnp.int32, (tc, tc), 0)
    col = lax.broadcasted_iota(jnp.int32, (tc, tc), 1)
    lower = jnp.where(row >= col, 1.0, 0.0).astype(BF16)
    x1, x2, x3 = _split3(x)
    cs = _dot(lower, x3) + _dot(lower, x2) + _dot(lower, x1) + carry_ref[...]
    f_ref[...] = cs
    carry_ref[...] = cs[tc - 1:tc, :]


def _cumsum_seq(x, tc):
    b, t, _ = x.shape
    return pl.pallas_call(
        functools.partial(_cumsum_kernel, tc=tc),
        grid=(b, t // tc),
        in_specs=[pl.BlockSpec((None, tc, LANES), lambda i, j: (i, j, 0))],
        out_specs=pl.BlockSpec((None, tc, LANES), lambda i, j: (i, j, 0)),
        out_shape=jax.ShapeDtypeStruct((b, t, LANES), F32),
        scratch_shapes=[pltpu.VMEM((1, LANES), F32)],
        compiler_params=pltpu.CompilerParams(
            dimension_semantics=("parallel", "arbitrary")),
    )(x)


def _kv_spec(arr, layer, head_off):
    if arr.ndim == 5:
        return pl.BlockSpec((None, None, None, arr.shape[3], DH),
                            lambda bi, h, i: (layer, bi, head_off + h, 0, 0))
    return pl.BlockSpec((None, None, arr.shape[2], DH),
                        lambda bi, h, i: (layer, bi, 0, head_off + h))


def _attn_a_kernel(q_ref, k_ref, v_ref, o_ref, *, t, off_blocks):
    gi = pl.program_id(2) + off_blocks
    q = q_ref[...]
    row = lax.broadcasted_iota(jnp.int32, (t, t), 0)
    col = lax.broadcasted_iota(jnp.int32, (t, t), 1)
    later = jnp.where(row > col, 1.0, 0.0).astype(BF16)

    def cond(carry):
        j, c, _ = carry
        return jnp.logical_and(j >= 0, jnp.max(c) > SB_CUTOFF)

    def body(carry):
        j, c, acc = carry
        start = pl.multiple_of(j * t, t)
        kb = k_ref[pl.ds(start, t), :].astype(BF16)
        vb = v_ref[pl.ds(start, t), :].astype(BF16)
        z = _nt_dot(q, kb)
        sp = jnp.maximum(z, 0.0) + jnp.log1p(jnp.exp(-jnp.abs(z)))
        mask = col < row + jnp.where(j < gi, t, 0)
        lm = jnp.where(mask, -sp, 0.0)
        hi = lm.astype(BF16)
        lo = (lm - hi.astype(F32)).astype(BF16)
        suffix = _dot(lo, later) + _dot(hi, later)
        w = jnp.where(mask, jnp.exp(z - sp + suffix + c), 0.0)
        acc = acc + _dot(w.astype(BF16), vb)
        c = c + jnp.sum(lm, axis=1, keepdims=True)
        return j - 1, c, acc

    init = (gi, jnp.zeros((t, 1), F32), jnp.zeros((t, DH), F32))
    _, _, acc = lax.while_loop(cond, body, init)
    o_ref[...] = acc.astype(o_ref.dtype)


def _attn_a(q3, k_arr, v_arr, layer, head_off, t, off_blocks):
    b, tq, _ = q3.shape
    return pl.pallas_call(
        functools.partial(_attn_a_kernel, t=t, off_blocks=off_blocks),
        grid=(b, H_A, tq // t),
        in_specs=[pl.BlockSpec((None, t, DH), lambda bi, h, i: (bi, i, h)),
                  _kv_spec(k_arr, layer, head_off), _kv_spec(v_arr, layer, head_off)],
        out_specs=pl.BlockSpec((None, t, DH), lambda bi, h, i: (bi, i, h)),
        out_shape=jax.ShapeDtypeStruct((b, tq, H_A * DH), BF16),
        compiler_params=pltpu.CompilerParams(
            dimension_semantics=("parallel", "parallel", "arbitrary"),
            vmem_limit_bytes=VMEM_LIMIT),
    )(q3, k_arr, v_arr)


def _attn_b_kernel(q_ref, k_ref, v_ref, fk_ref, o_ref, *, tq, tk, q_off):
    single = k_ref.shape[0] == tk
    q_pos0 = q_off + pl.program_id(2) * tq
    n_full = 0 if single else q_pos0 // tk
    q = q_ref[...]

    def key_start(j):
        return 0 if single else pl.multiple_of(j * tk, tk)

    def scores(j):
        start = key_start(j)
        kb = k_ref[pl.ds(start, tk), :].astype(BF16)
        return _nt_dot(q, kb) - fk_ref[:, pl.ds(start, tk)]

    def update(s, j, carry):
        m, l, acc = carry
        vb = v_ref[pl.ds(key_start(j), tk), :].astype(BF16)
        m_new = jnp.maximum(m, jnp.max(s, axis=1, keepdims=True))
        alpha = jnp.exp2(m - m_new)
        p = jnp.exp2(s - m_new)
        l = alpha * l + jnp.sum(p, axis=1, keepdims=True)
        acc = alpha * acc + _dot(p.astype(BF16), vb)
        return m_new, l, acc

    carry = (jnp.full((tq, 1), NEG_INF, F32), jnp.zeros((tq, 1), F32), jnp.zeros((tq, DH), F32))
    s = scores(0)
    if not single:
        def body(j, c):
            s_cur, inner = c
            s_next = scores(j + 1)
            return s_next, update(s_cur, j, inner)

        s, carry = lax.fori_loop(0, n_full, body, (s, carry))
    row = lax.broadcasted_iota(jnp.int32, (tq, tk), 0)
    col = lax.broadcasted_iota(jnp.int32, (tq, tk), 1)
    s = jnp.where(col <= row + (q_pos0 - key_start(n_full)), s, NEG_INF)
    _, l, acc = update(s, n_full, carry)
    o_ref[...] = (acc / l).astype(o_ref.dtype)


def _attn_b(q3, k_arr, v_arr, layer, head_off, frow, tq, tk, q_off):
    b, tq_all, _ = q3.shape
    tk_all = frow.shape[3]
    assert tk % tq == 0 and q_off % tq == 0 and tk_all % tk == 0
    return pl.pallas_call(
        functools.partial(_attn_b_kernel, tq=tq, tk=tk, q_off=q_off),
        grid=(b, H_B, tq_all // tq),
        in_specs=[pl.BlockSpec((None, tq, DH), lambda bi, h, i: (bi, i, H_A + h)),
                  _kv_spec(k_arr, layer, head_off), _kv_spec(v_arr, layer, head_off),
                  pl.BlockSpec((None, None, 1, tk_all), lambda bi, h, i: (bi, h, 0, 0))],
        out_specs=pl.BlockSpec((None, tq, DH), lambda bi, h, i: (bi, i, h)),
        out_shape=jax.ShapeDtypeStruct((b, tq_all, H_B * DH), BF16),
        compiler_params=pltpu.CompilerParams(
            dimension_semantics=("parallel", "parallel", "arbitrary"),
            vmem_limit_bytes=VMEM_LIMIT),
    )(q3, k_arr, v_arr, frow)


def _band_table(rel_bias, r):
    w = C_PAST + r
    lh = w + r - 1
    dist = C_PAST + r - 1 - jnp.arange(lh)
    hvec = rel_bias.astype(F32)[:, jnp.clip(dist, -REL_CLIP, REL_CLIP) + REL_CLIP]
    q = lh + 1
    hq = jnp.pad(hvec, ((0, 0), (0, 1)))
    skew = jnp.tile(hq, (1, r + 1))[:, :r * (q + 1)].reshape(H_C, r, q + 1)[:, :, :w]
    bias = skew[:, ::-1, :]
    qi = jnp.arange(r)[:, None] // CHUNK
    kj = jnp.arange(w)[None, :] // CHUNK
    ok = (kj >= qi) & (kj <= qi + C_PREV)
    return jnp.where(ok[None], bias * LOG2E, NEG_INF)


def _attn_c_kernel(q_ref, k_ref, v_ref, tab_ref, o_ref, *, r, k_lead):
    i = pl.program_id(2)
    w = C_PAST + r
    q = q_ref[...]

    def run(start, width, tcol0):
        kb = k_ref[pl.ds(start, width), :].astype(BF16)
        vb = v_ref[pl.ds(start, width), :].astype(BF16)
        s = _nt_dot(q, kb) + tab_ref[:, tcol0:tcol0 + width]
        m = jnp.max(s, axis=1, keepdims=True)
        p = jnp.exp2(s - m)
        l = jnp.sum(p, axis=1, keepdims=True)
        o_ref[...] = (_dot(p.astype(BF16), vb) / l).astype(o_ref.dtype)

    n_short = max(0, C_PAST - k_lead) // r
    for t in range(n_short):
        @pl.when(i == t)
        def _(t=t):
            have = k_lead + t * r
            run(0, have + r, C_PAST - have)

    @pl.when(i >= n_short)
    def _():
        run(pl.multiple_of(i * r + k_lead - C_PAST, SUBLANES), w, 0)


def _attn_c(q3, k_arr, v_arr, layer, head_off, table, r, k_lead):
    b, tq, _ = q3.shape
    w = C_PAST + r
    assert (C_PAST - k_lead) % r == 0 or k_lead >= C_PAST
    return pl.pallas_call(
        functools.partial(_attn_c_kernel, r=r, k_lead=k_lead),
        grid=(b, H_C, tq // r),
        in_specs=[pl.BlockSpec((None, r, DH), lambda bi, h, i: (bi, i, H_A + H_B + h)),
                  _kv_spec(k_arr, layer, head_off), _kv_spec(v_arr, layer, head_off),
                  pl.BlockSpec((None, r, w), lambda bi, h, i: (h, 0, 0))],
        out_specs=pl.BlockSpec((None, r, DH), lambda bi, h, i: (bi, i, h)),
        out_shape=jax.ShapeDtypeStruct((b, tq, H_C * DH), BF16),
        compiler_params=pltpu.CompilerParams(
            dimension_semantics=("parallel", "parallel", "arbitrary"),
            vmem_limit_bytes=VMEM_LIMIT),
    )(q3, k_arr, v_arr, table)


def _outproj_kernel(x_ref, oa_ref, ob_ref, oc_ref, wo_ref, g_ref, wr_ref, br_ref,
                    h_ref, hn_ref, route_ref):
    a0, a1 = H_A * DH, (H_A + H_B) * DH
    acc = _dot(oa_ref[...], wo_ref[0:a0, :])
    acc = acc + _dot(ob_ref[...], wo_ref[a0:a1, :])
    acc = acc + _dot(oc_ref[...], wo_ref[a1:MIX, :])
    h = x_ref[...] + acc
    h_ref[...] = h
    ms = jnp.mean(h * h, axis=-1, keepdims=True)
    hn = h * lax.rsqrt(ms + EPS) * g_ref[...]
    hn_ref[...] = hn

    h1, h2, _ = _split3(hn)
    w1 = wr_ref[0]
    w2 = wr_ref[1]
    lg = _dot(h2, w1) + _dot(h1, w2) + _dot(h1, w1) + br_ref[...]

    lane = lax.broadcasted_iota(jnp.int32, lg.shape, 1)
    lane_f = lane.astype(F32)
    is_g = lane < N_GROUPS
    lgm = jnp.where(is_g, lg, NEG_INF)
    gmax = jnp.max(lgm, axis=1, keepdims=True)
    gidx = jnp.min(jnp.where(lgm == gmax, lane_f, float(LANES)), axis=1, keepdims=True)
    pg = 1.0 / jnp.sum(jnp.where(is_g, jnp.exp(lgm - gmax), 0.0), axis=1, keepdims=True)
    grp = jnp.floor((lane_f - N_GROUPS) * (1.0 / EXP_PER_GROUP))
    in_grp = (lane >= N_GROUPS) & (lane < N_GROUPS + N_EXPERTS) & (grp == gidx)
    le = jnp.where(in_grp, lg, NEG_INF)
    t1 = jnp.max(le, axis=1, keepdims=True)
    i1 = jnp.min(jnp.where(le == t1, lane_f, float(LANES)), axis=1, keepdims=True)
    le2 = jnp.where(lane_f == i1, NEG_INF, le)
    t2 = jnp.max(le2, axis=1, keepdims=True)
    i2 = jnp.min(jnp.where(le2 == t2, lane_f, float(LANES)), axis=1, keepdims=True)
    e = jnp.exp(t2 - t1)
    g1 = pg / (1.0 + e)
    g2 = pg * e / (1.0 + e)
    route = jnp.where(lane == 0, i1 - N_GROUPS,
                      jnp.where(lane == 1, i2 - N_GROUPS,
                                jnp.where(lane == 2, g1, jnp.where(lane == 3, g2, 0.0))))
    route_ref[...] = route


def _outproj(x2d, x_block_off, oa, ob, oc, wo_bf, g, wr2, br, n_total, out_block_off, prev):
    n_rows, d = oa.shape[0], x2d.shape[1]
    tm = PROJ_TM
    nb = n_rows // tm
    in_specs = [
        pl.BlockSpec((tm, d), lambda i: (i + x_block_off, 0)),
        pl.BlockSpec((tm, H_A * DH), lambda i: (i, 0)),
        pl.BlockSpec((tm, H_B * DH), lambda i: (i, 0)),
        pl.BlockSpec((tm, H_C * DH), lambda i: (i, 0)),
        pl.BlockSpec((MIX, d), lambda i: (0, 0)),
        pl.BlockSpec((1, d), lambda i: (0, 0)),
        pl.BlockSpec((2, d, LANES), lambda i: (0, 0, 0)),
        pl.BlockSpec((1, LANES), lambda i: (0, 0)),
    ]
    args = [x2d, oa, ob, oc, wo_bf, g, wr2, br]
    aliases = {}
    if prev is not None:
        for t, arr in enumerate(prev):
            in_specs.append(pl.BlockSpec(memory_space=pl.ANY))
            args.append(arr)
            aliases[8 + t] = t

    def body(*refs):
        _outproj_kernel(*refs[:8], *refs[len(args):len(args) + 3])

    return pl.pallas_call(
        body,
        grid=(nb,),
        in_specs=in_specs,
        out_specs=[pl.BlockSpec((tm, d), lambda i: (i + out_block_off, 0)),
                   pl.BlockSpec((tm, d), lambda i: (i + out_block_off, 0)),
                   pl.BlockSpec((tm, LANES), lambda i: (i + out_block_off, 0))],
        out_shape=[jax.ShapeDtypeStruct((n_total, d), F32),
                   jax.ShapeDtypeStruct((n_total, d), F32),
                   jax.ShapeDtypeStruct((n_total, LANES), F32)],
        input_output_aliases=aliases,
        compiler_params=pltpu.CompilerParams(
            dimension_semantics=("parallel",), vmem_limit_bytes=VMEM_LIMIT),
    )(*args)


def _issue_rows(idx_ref, n_idx, copy_for):
    def group(g, carry):
        base = pl.multiple_of(g * SUBLANES, SUBLANES)
        for u in range(SUBLANES):
            copy_for(idx_ref[0, 0, base + u], base, u).start()
        return carry

    lax.fori_loop(0, n_idx // SUBLANES, group, 0)


def _moe_kernel(be_ref, nused_ref, src_ref, srcn_ref, hn_hbm, wg_ref, wu_ref, wd_ref, yb_ref,
                xbuf, wg16, wu16, wd16, sem):
    i = pl.program_id(0)
    nused = nused_ref[0]
    blk = xbuf.shape[1]

    def gather(idx_ref, slot):
        def copy_for(tok, base, u):
            return pltpu.make_async_copy(hn_hbm.at[pl.ds(tok, 1), :],
                                         xbuf.at[slot, pl.ds(base + u, 1), :], sem.at[slot])
        _issue_rows(idx_ref, blk, copy_for)

    @pl.when(i == 0)
    def _():
        gather(src_ref, 0)

    @pl.when(i + 1 < nused)
    def _():
        gather(srcn_ref, (i + 1) % 2)

    @pl.when(i < nused)
    def _():
        prev_e = be_ref[jnp.maximum(i - 1, 0)]

        @pl.when(jnp.logical_or(i == 0, be_ref[i] != prev_e))
        def _():
            wg16[...] = wg_ref[...].astype(BF16)
            wu16[...] = wu_ref[...].astype(BF16)
            wd16[...] = wd_ref[...].astype(BF16)

        slot = i % 2
        pltpu.make_async_copy(hn_hbm.at[pl.ds(0, blk), :], xbuf.at[slot], sem.at[slot]).wait()
        xb = xbuf[slot].astype(BF16)
        gate = _dot(xb, wg16[...])
        up = _dot(xb, wu16[...])
        act = (gate * jax.nn.sigmoid(gate) * up).astype(BF16)
        yb_ref[...] = _dot(act, wd16[...])

    @pl.when(i >= nused)
    def _():
        yb_ref[...] = jnp.zeros_like(yb_ref)


def _moe(hn, src, block_e, nused, wg, wu, wd, n_blocks):
    d = hn.shape[1]
    de = wg.shape[2]
    src3 = src.reshape(n_blocks, 1, MOE_BLK)
    last = n_blocks - 1
    grid_spec = pltpu.PrefetchScalarGridSpec(
        num_scalar_prefetch=2,
        grid=(n_blocks,),
        in_specs=[
            pl.BlockSpec((1, 1, MOE_BLK), lambda i, be, nu: (i, 0, 0), memory_space=pltpu.SMEM),
            pl.BlockSpec((1, 1, MOE_BLK), lambda i, be, nu: (jnp.minimum(i + 1, last), 0, 0),
                         memory_space=pltpu.SMEM),
            pl.BlockSpec(memory_space=pl.ANY),
            pl.BlockSpec((None, d, de), lambda i, be, nu: (be[i], 0, 0)),
            pl.BlockSpec((None, d, de), lambda i, be, nu: (be[i], 0, 0)),
            pl.BlockSpec((None, de, d), lambda i, be, nu: (be[i], 0, 0)),
        ],
        out_specs=pl.BlockSpec((MOE_BLK, d), lambda i, be, nu: (i, 0)),
        scratch_shapes=[pltpu.VMEM((2, MOE_BLK, d), F32),
                        pltpu.VMEM((d, de), BF16), pltpu.VMEM((d, de), BF16), pltpu.VMEM((de, d), BF16),
                        pltpu.SemaphoreType.DMA((2,))],
    )
    return pl.pallas_call(
        _moe_kernel,
        grid_spec=grid_spec,
        out_shape=jax.ShapeDtypeStruct((n_blocks * MOE_BLK, d), F32),
        compiler_params=pltpu.CompilerParams(
            dimension_semantics=("arbitrary",), vmem_limit_bytes=VMEM_LIMIT),
    )(block_e, nused, src3, src3, hn, wg, wu, wd)


def _dispatch(eid, n_blocks):
    p = eid.shape[0] * eid.shape[1]
    e = eid.reshape(-1)
    onehot = (e[:, None] == jnp.arange(N_EXPERTS, dtype=jnp.int32)[None, :]).astype(jnp.int32)
    cs = jnp.cumsum(onehot, axis=0)
    rank = jnp.sum(cs * onehot, axis=1) - 1
    counts = cs[-1]
    padded = (counts + MOE_BLK - 1) // MOE_BLK * MOE_BLK
    pend = jnp.cumsum(padded)
    pstart = pend - padded
    dest = (jnp.sum(onehot * pstart[None, :], axis=1) + rank).astype(jnp.int32)
    src = jnp.zeros((n_blocks * MOE_BLK,), jnp.int32).at[dest].set(
        jnp.arange(p, dtype=jnp.int32) // eid.shape[1])
    blk_start = jnp.arange(n_blocks, dtype=jnp.int32) * MOE_BLK
    block_e = jnp.minimum(jnp.sum((pend[None, :] <= blk_start[:, None]).astype(jnp.int32), axis=1),
                          N_EXPERTS - 1).astype(jnp.int32)
    nused = (pend[-1:] // MOE_BLK).astype(jnp.int32)
    return dest, src, block_e, nused


def _combine_kernel(pos_ref, h_ref, route_ref, yb_hbm, g_ref, o_ref, buf, sem, *, final):
    tm = h_ref.shape[0]

    def copy_for(p, base, u):
        return pltpu.make_async_copy(yb_hbm.at[pl.ds(p, 1), :],
                                     buf.at[u % 2, pl.ds(base // 2 + u // 2, 1), :], sem)

    _issue_rows(pos_ref, 2 * tm, copy_for)
    pltpu.make_async_copy(yb_hbm.at[pl.ds(0, tm), :], buf.at[0], sem).wait()
    pltpu.make_async_copy(yb_hbm.at[pl.ds(0, tm), :], buf.at[1], sem).wait()
    route = route_ref[...]
    g1 = _lane_pick(route, 2)
    g2 = _lane_pick(route, 3)
    y = h_ref[...] + (g1 * buf[0] + g2 * buf[1])
    if final:
        ms = jnp.mean(y * y, axis=-1, keepdims=True)
        y = y * lax.rsqrt(ms + EPS) * g_ref[...]
    o_ref[...] = y


def _combine(pos3, h, route, yb, g, block_off, n_rows, final):
    d = h.shape[1]
    tm = COMBINE_TM
    return pl.pallas_call(
        functools.partial(_combine_kernel, final=final),
        grid=(n_rows // tm,),
        in_specs=[
            pl.BlockSpec((1, 1, 2 * tm), lambda i: (i + block_off, 0, 0), memory_space=pltpu.SMEM),
            pl.BlockSpec((tm, d), lambda i: (i + block_off, 0)),
            pl.BlockSpec((tm, LANES), lambda i: (i + block_off, 0)),
            pl.BlockSpec(memory_space=pl.ANY),
            pl.BlockSpec((1, d), lambda i: (0, 0)),
        ],
        out_specs=pl.BlockSpec((tm, d), lambda i: (i, 0)),
        out_shape=jax.ShapeDtypeStruct((n_rows, d), F32),
        scratch_shapes=[pltpu.VMEM((2, tm, d), F32), pltpu.SemaphoreType.DMA(())],
        compiler_params=pltpu.CompilerParams(
            dimension_semantics=("arbitrary",), vmem_limit_bytes=VMEM_LIMIT),
    )(pos3, h, route, yb, g)


def _pad_lanes(a):
    return jnp.pad(a, [(0, 0)] * (a.ndim - 1) + [(0, LANES - a.shape[-1])])


def _head_major(a):
    return jnp.swapaxes(a, 1, 2)


def kernel(x_prompt, x_sample, cache_a_k, cache_a_v, cache_b_k, cache_b_v, cache_b_logf,
           cache_c_k, cache_c_v, norm1_g, norm2_g, w_in, b_f, rel_bias, w_o,
           w_rg, b_rg, w_re, b_re, w_gate, w_up, w_down, final_g):
    bp, sp, d = x_prompt.shape
    bs, ss, _ = x_sample.shape
    depth = w_in.shape[0]
    past = cache_a_k.shape[2]
    keep = cache_c_k.shape[2]
    n_p, n_s = bp * sp, bs * ss
    n_tot = n_p + n_s
    assert d == MIX and n_p % PROJ_TM == 0 and n_s % PROJ_TM == 0
    assert ss == CHUNK and past % ss == 0 and keep == C_PAST

    t_a = min(256, sp)
    t_b = min(512, sp)
    r_p = min(256, sp)
    n_blocks = (2 * n_tot) // MOE_BLK + N_EXPERTS

    xs_p = (x_prompt.reshape(n_p, d), 0)
    xs_s = (x_sample.reshape(n_s, d), 0)
    stacks_p = stacks_s = None
    logf_p, logf_s = [], []
    y_prompt = y_sample = None

    for l in range(depth):
        w_bf = w_in[l, :, :3 * MIX].astype(BF16)
        wf_bf = _pad_lanes(w_in[l, :, 3 * MIX:]).astype(BF16)
        bfp = _pad_lanes(b_f[l][None, :].astype(F32))
        g1 = norm1_g[l][None, :].astype(F32)
        q_p, stacks_p, k16_p, v16_p, lf_p = _inproj(xs_p[0], xs_p[1], bp, sp, g1, w_bf, wf_bf, bfp,
                                                    l, depth, stacks_p)
        q_s, stacks_s, _, _, lf_s = _inproj(xs_s[0], xs_s[1], bs, ss, g1, w_bf, wf_bf, bfp,
                                            l, depth, stacks_s)
        logf_p.append(lf_p[:, :H_B].reshape(bp, sp, H_B))
        logf_s.append(lf_s[:, :H_B].reshape(bs, ss, H_B))
        q_p3 = q_p.reshape(bp, sp, MIX)
        q_s3 = q_s.reshape(bs, ss, MIX)
        k16_p, v16_p = k16_p[None], v16_p[None]

        def cat_tok(cache, nm):
            c = cache[l].reshape(bs, cache.shape[2], -1)
            return jnp.concatenate([c, stacks_s[nm][l].reshape(bs, ss, -1)], axis=1)[None]

        def cat_hm(cache, nm):
            return jnp.concatenate([_head_major(cache[l]), stacks_s[nm][l]], axis=2)[None]

        oa_p = _attn_a(q_p3, k16_p, v16_p, 0, 0, t_a, 0)
        oa_s = _attn_a(q_s3, cat_tok(cache_a_k, "ka"), cat_tok(cache_a_v, "va"), 0, 0, ss, past // ss)

        def row_layout(fcol):
            return (jnp.swapaxes(fcol[:, :, :H_B], 1, 2) * LOG2E)[:, :, None, :]

        fcol_p = _cumsum_seq(lf_p.reshape(bp, sp, LANES), min(512, sp))
        lf_cat = jnp.concatenate([_pad_lanes(cache_b_logf[l].astype(F32)),
                                  lf_s.reshape(bs, ss, LANES)], axis=1)
        t_cat = past + ss
        tc_s = t_cat // 3 if (t_cat % 24 == 0) else t_cat
        fcol_s = _cumsum_seq(lf_cat, tc_s)
        ob_p = _attn_b(q_p3, k16_p, v16_p, 0, H_A, row_layout(fcol_p), t_b, t_b, 0)
        ob_s = _attn_b(q_s3, cat_hm(cache_b_k, "kb"), cat_hm(cache_b_v, "vb"), 0, 0,
                       row_layout(fcol_s), ss, t_cat, past)

        oc_p = _attn_c(q_p3, k16_p, v16_p, 0, H_A + H_B, _band_table(rel_bias[l], r_p), r_p, 0)
        oc_s = _attn_c(q_s3, cat_hm(cache_c_k, "kc"), cat_hm(cache_c_v, "vc"), 0, 0,
                       _band_table(rel_bias[l], ss), ss, keep)

        wo_bf = w_o[l].astype(BF16)
        g2 = norm2_g[l][None, :].astype(F32)
        wr = _pad_lanes(jnp.concatenate([w_rg[l], w_re[l]], axis=1).astype(F32))
        wr1 = wr.astype(BF16)
        wr2 = jnp.stack([wr1, (wr - wr1.astype(F32)).astype(BF16)])
        br = _pad_lanes(jnp.concatenate([b_rg[l], b_re[l]])[None, :].astype(F32))
        shared = _outproj(xs_p[0], xs_p[1], oa_p.reshape(n_p, -1), ob_p.reshape(n_p, -1),
                          oc_p.reshape(n_p, -1), wo_bf, g2, wr2, br, n_tot, 0, None)
        h, hn, route = _outproj(xs_s[0], xs_s[1], oa_s.reshape(n_s, -1), ob_s.reshape(n_s, -1),
                                oc_s.reshape(n_s, -1), wo_bf, g2, wr2, br, n_tot,
                                n_p // PROJ_TM, shared)

        eid = route[:, :2].astype(jnp.int32)
        dest, src, block_e, nused = _dispatch(eid, n_blocks)
        yb = _moe(hn, src, block_e, nused, w_gate[l], w_up[l], w_down[l], n_blocks)
        pos3 = dest.reshape(n_tot // COMBINE_TM, 1, 2 * COMBINE_TM)
        fg = final_g[None, :].astype(F32)
        if l + 1 < depth:
            y = _combine(pos3, h, route, yb, fg, 0, n_tot, False)
            xs_p = (y, 0)
            xs_s = (y, n_p // PROJ_TM)
        else:
            y_prompt = _combine(pos3, h, route, yb, fg, 0, n_p, True).reshape(bp, sp, d)
            y_sample = _combine(pos3, h, route, yb, fg, n_p // COMBINE_TM, n_s, True).reshape(bs, ss, d)

    def tok_state(stacks, nm, b, s):
        return stacks[nm].reshape(depth, b, s, H_A, DH)

    def hm_state(stacks, nm):
        return jnp.swapaxes(stacks[nm], 2, 3)

    keep_p = min(C_PAST, sp)
    return (y_prompt, y_sample,
            tok_state(stacks_p, "ka", bp, sp), tok_state(stacks_p, "va", bp, sp),
            hm_state(stacks_p, "kb"), hm_state(stacks_p, "vb"),
            jnp.stack(logf_p, axis=0),
            hm_state(stacks_p, "kc")[:, :, sp - keep_p:], hm_state(stacks_p, "vc")[:, :, sp - keep_p:],
            tok_state(stacks_s, "ka", bs, ss), tok_state(stacks_s, "va", bs, ss),
            hm_state(stacks_s, "kb"), hm_state(stacks_s, "vb"),
            jnp.stack(logf_s, axis=0),
            hm_state(stacks_s, "kc"), hm_state(stacks_s, "vc"))
```

```python
import functools
import math

import jax
import jax.numpy as jnp
from jax import lax
from jax.experimental import pallas as pl
from jax.experimental.pallas import tpu as pltpu

F32 = jnp.float32
BF16 = jnp.bfloat16

DH = 128
H_A, H_B, H_C = 4, 6, 6
N_HEADS = H_A + H_B + H_C
MIX = N_HEADS * DH
CHUNK = 64
C_PREV = 8
C_PAST = C_PREV * CHUNK
REL_CLIP = 128
N_GROUPS = 4
EXP_PER_GROUP = 8
N_EXPERTS = N_GROUPS * EXP_PER_GROUP
EPS = 1e-6
NEG_INF = -1e30
SCALE = DH ** -0.5
LOG2E = math.log2(math.e)

LANES = 128
SUBLANES = 8
BF16_ROWS = 16
INPROJ_TM = 256
PROJ_TM = 512
PROJ_TN = 512
MOE_BLK = 256
COMBINE_TM = 256
SB_CUTOFF = -104.0
VMEM_LIMIT = 56 * 1024 * 1024


def _nt_dot(a, b):
    return lax.dot_general(a, b, (((1,), (1,)), ((), ())), preferred_element_type=F32)


def _dot(a, b):
    return jnp.dot(a, b, preferred_element_type=F32)


def _split3(x):
    x1 = x.astype(BF16)
    r1 = x - x1.astype(F32)
    x2 = r1.astype(BF16)
    x3 = (r1 - x2.astype(F32)).astype(BF16)
    return x1, x2, x3


def _aligned(x, m):
    return x if isinstance(x, int) else pl.multiple_of(x, m)


def _lane_pick(x, idx):
    lane = lax.broadcasted_iota(jnp.int32, x.shape, 1)
    return jnp.sum(jnp.where(lane == idx, x, 0.0), axis=1, keepdims=True)


def _inproj_kernel(x_ref, g_ref, w_hbm, wf_ref, bf_ref,
                   q_ref, ka_ref, va_ref, kb_ref, vb_ref, kc_ref, vc_ref, k16_ref, v16_ref, lf_ref,
                   w_vmem, xn_ref, sem, *, rb):
    tm = x_ref.shape[0]
    nbb = tm // rb
    heads_per_chunk = PROJ_TN // DH
    chunks_per_part = MIX // PROJ_TN

    @pl.when(pl.program_id(0) == 0)
    def _():
        cp = pltpu.make_async_copy(w_hbm, w_vmem, sem)
        cp.start()
        cp.wait()

    x = x_ref[...]
    ms = jnp.mean(x * x, axis=-1, keepdims=True)
    xn_ref[...] = (x * lax.rsqrt(ms + EPS) * g_ref[...]).astype(BF16)
    lf_ref[...] = jax.nn.log_sigmoid(_dot(xn_ref[...], wf_ref[...]) + bf_ref[...])

    def write_kv(jj, acc, tok_ref, b_ref, c_ref, h16_ref):
        for u in range(heads_per_chunk):
            hh = (jj * heads_per_chunk + u) % N_HEADS
            piece = acc[:, u * DH:(u + 1) * DH]
            if hh < H_A:
                tok_ref[:, hh * DH:(hh + 1) * DH] = piece
            for bb in range(nbb):
                rows = piece[bb * rb:(bb + 1) * rb]
                h16_ref[bb, hh] = rows.astype(BF16)
                if H_A <= hh < H_A + H_B:
                    b_ref[bb, hh - H_A] = rows
                elif hh >= H_A + H_B:
                    c_ref[bb, hh - H_A - H_B] = rows

    for jj in range(3 * chunks_per_part):
        acc = _dot(xn_ref[...], w_vmem[:, jj * PROJ_TN:(jj + 1) * PROJ_TN])
        part = jj // chunks_per_part
        if part == 0:
            sc = SCALE if (jj + 1) * heads_per_chunk <= H_A else SCALE * LOG2E
            q_ref[:, jj * PROJ_TN:(jj + 1) * PROJ_TN] = (acc * sc).astype(BF16)
        elif part == 1:
            write_kv(jj, acc, ka_ref, kb_ref, kc_ref, k16_ref)
        else:
            write_kv(jj, acc, va_ref, vb_ref, vc_ref, v16_ref)


_KV_NAMES = ["ka", "va", "kb", "vb", "kc", "vc"]


def _inproj(x2d, row_block_off, nb_batch, seq, g, w_bf, wf_bf, bfp, layer, depth, prev):
    assert H_A * DH == PROJ_TN
    d = x2d.shape[1]
    tm = INPROJ_TM
    n_rows = nb_batch * seq
    nb = n_rows // tm
    rb = min(seq, tm)
    nbb = tm // rb
    spb = seq // rb

    def hm_map(i):
        return (layer, i // spb, 0, i % spb, 0)

    def hm16_map(i):
        return (i // spb, 0, i % spb, 0)

    out_shape = [jax.ShapeDtypeStruct((n_rows, MIX), BF16)]
    out_specs = [pl.BlockSpec((tm, MIX), lambda i: (i, 0))]
    for nm in _KV_NAMES:
        if nm in ("ka", "va"):
            out_shape.append(jax.ShapeDtypeStruct((depth, n_rows, H_A * DH), F32))
            out_specs.append(pl.BlockSpec((None, tm, H_A * DH), lambda i: (layer, i, 0)))
        else:
            hn = H_B if nm[1] == "b" else H_C
            out_shape.append(jax.ShapeDtypeStruct((depth, nb_batch, hn, seq, DH), F32))
            out_specs.append(pl.BlockSpec((None, nbb, hn, rb, DH), hm_map))
    for _ in range(2):
        out_shape.append(jax.ShapeDtypeStruct((nb_batch, N_HEADS, seq, DH), BF16))
        out_specs.append(pl.BlockSpec((nbb, N_HEADS, rb, DH), hm16_map))
    out_shape.append(jax.ShapeDtypeStruct((n_rows, LANES), F32))
    out_specs.append(pl.BlockSpec((tm, LANES), lambda i: (i, 0)))
    n_out = len(out_shape)

    in_specs = [
        pl.BlockSpec((tm, d), lambda i: (i + row_block_off, 0)),
        pl.BlockSpec((1, d), lambda i: (0, 0)),
        pl.BlockSpec(memory_space=pl.ANY),
        pl.BlockSpec((d, LANES), lambda i: (0, 0)),
        pl.BlockSpec((1, LANES), lambda i: (0, 0)),
    ]
    args = [x2d, g, w_bf, wf_bf, bfp]
    n_in = len(args)
    aliases = {}
    if prev is not None:
        for t, nm in enumerate(_KV_NAMES):
            in_specs.append(pl.BlockSpec(memory_space=pl.ANY))
            args.append(prev[nm])
            aliases[n_in + t] = 1 + t

    def body(*refs):
        _inproj_kernel(*refs[:n_in], *refs[len(args):len(args) + n_out], *refs[len(args) + n_out:], rb=rb)

    res = pl.pallas_call(
        body,
        grid=(nb,),
        in_specs=in_specs,
        out_specs=out_specs,
        out_shape=out_shape,
        scratch_shapes=[pltpu.VMEM(w_bf.shape, BF16), pltpu.VMEM((tm, d), BF16),
                        pltpu.SemaphoreType.DMA(())],
        input_output_aliases=aliases,
        compiler_params=pltpu.CompilerParams(
            dimension_semantics=("arbitrary",), vmem_limit_bytes=VMEM_LIMIT),
    )(*args)
    stacks = dict(zip(_KV_NAMES, res[1:7]))
    return res[0], stacks, res[7], res[8], res[9]


def _cumsum_kernel(x_ref, f_ref, carry_ref, *, tc):
    @pl.when(pl.program_id(1) == 0)
    def _():
        carry_ref[...] = jnp.zeros_like(carry_ref)

    x = x_ref[...]
    row = lax.broadcasted_iota(jnp.int32, (tc, tc), 0)
    col = lax.broadcasted_iota(jnp.int32, (tc, tc), 1)
    lower = jnp.where(row >= col, 1.0, 0.0).astype(BF16)
    x1, x2, x3 = _split3(x)
    cs = _dot(lower, x3) + _dot(lower, x2) + _dot(lower, x1) + carry_ref[...]
    f_ref[...] = cs
    carry_ref[...] = cs[tc - 1:tc, :]


def _cumsum_seq(x, tc):
    b, t, _ = x.shape
    return pl.pallas_call(
        functools.partial(_cumsum_kernel, tc=tc),
        grid=(b, t // tc),
        in_specs=[pl.BlockSpec((None, tc, LANES), lambda i, j: (i, j, 0))],
        out_specs=pl.BlockSpec((None, tc, LANES), lambda i, j: (i, j, 0)),
        out_shape=jax.ShapeDtypeStruct((b, t, LANES), F32),
        scratch_shapes=[pltpu.VMEM((1, LANES), F32)],
        compiler_params=pltpu.CompilerParams(
            dimension_semantics=("parallel", "arbitrary")),
    )(x)


def _kv_spec(arr, layer, head_off):
    if arr.ndim == 5:
        return pl.BlockSpec((None, None, None, arr.shape[3], DH),
                            lambda bi, h: (layer, bi, head_off + h, 0, 0))
    return pl.BlockSpec((None, None, arr.shape[2], DH),
                        lambda bi, h: (layer, bi, 0, head_off + h))


def _attn_call(body, q3, q_head_off, kv, extra, n_heads):
    b, tq, _ = q3.shape
    in_specs = [pl.BlockSpec((None, tq, DH), lambda bi, h: (bi, 0, q_head_off + h))]
    args = [q3]
    for arr, layer, head_off in kv:
        in_specs.append(_kv_spec(arr, layer, head_off))
        args.append(arr)
    for arr, spec in extra:
        in_specs.append(spec)
        args.append(arr)
    return pl.pallas_call(
        body,
        grid=(b, n_heads),
        in_specs=in_specs,
        out_specs=pl.BlockSpec((None, tq, DH), lambda bi, h: (bi, 0, h)),
        out_shape=jax.ShapeDtypeStruct((b, tq, n_heads * DH), BF16),
        compiler_params=pltpu.CompilerParams(
            dimension_semantics=("parallel", "parallel"), vmem_limit_bytes=VMEM_LIMIT),
    )(*args)


def _attn_a_kernel(*refs, t, tp, has_past):
    if has_past:
        q_ref, k_ref, v_ref, kp_ref, vp_ref, o_ref = refs
    else:
        q_ref, k_ref, v_ref, o_ref = refs
    n_q = q_ref.shape[0] // t

    def later_matrix(n):
        row = lax.broadcasted_iota(jnp.int32, (n, n), 0)
        col = lax.broadcasted_iota(jnp.int32, (n, n), 1)
        return jnp.where(row > col, 1.0, 0.0).astype(BF16)

    def cond(carry):
        j, c, _ = carry
        return jnp.logical_and(j >= 0, jnp.max(c) > SB_CUTOFF)

    def make_body(q, kr, vr, tb, diag):
        later = later_matrix(tb)
        row = lax.broadcasted_iota(jnp.int32, (t, tb), 0)
        col = lax.broadcasted_iota(jnp.int32, (t, tb), 1)

        def body(carry):
            j, c, acc = carry
            start = pl.multiple_of(j * tb, tb)
            kb = kr[pl.ds(start, tb), :].astype(BF16)
            vb = vr[pl.ds(start, tb), :].astype(BF16)
            z = _nt_dot(q, kb)
            sp = jnp.maximum(z, 0.0) + jnp.log1p(jnp.exp(-jnp.abs(z)))
            if diag is None:
                lm = -sp
            else:
                mask = col < row + jnp.where(j < diag, tb, 0)
                lm = jnp.where(mask, -sp, 0.0)
            hi = lm.astype(BF16)
            lo = (lm - hi.astype(F32)).astype(BF16)
            suffix = _dot(lo, later) + _dot(hi, later)
            w = jnp.exp(z - sp + suffix + c)
            if diag is not None:
                w = jnp.where(mask, w, 0.0)
            acc = acc + _dot(w.astype(BF16), vb)
            c = c + jnp.sum(lm, axis=1, keepdims=True)
            return j - 1, c, acc

        return body

    def q_block(i, carry):
        rows = pl.ds(_aligned(i * t, t), t)
        q = q_ref[rows, :]
        state = (jnp.asarray(i, jnp.int32), jnp.zeros((t, 1), F32), jnp.zeros((t, DH), F32))
        _, c, acc = lax.while_loop(cond, make_body(q, k_ref, v_ref, t, i), state)
        if has_past:
            n_past = kp_ref.shape[0] // tp
            state = (jnp.asarray(n_past - 1, jnp.int32), c, acc)
            _, c, acc = lax.while_loop(cond, make_body(q, kp_ref, vp_ref, tp, None), state)
        o_ref[rows, :] = acc.astype(o_ref.dtype)
        return carry

    if n_q == 1:
        q_block(0, 0)
    else:
        lax.fori_loop(0, n_q, q_block, 0)


def _attn_a(q3, own, past, t, tp):
    kv = list(own) + (list(past) if past else [])
    return _attn_call(functools.partial(_attn_a_kernel, t=t, tp=tp, has_past=bool(past)),
                      q3, 0, kv, [], H_A)


def _attn_b_kernel(*refs, t, tp, has_past):
    if has_past:
        q_ref, k_ref, v_ref, kp_ref, vp_ref, fk_ref, o_ref = refs
        t_past = kp_ref.shape[0]
    else:
        q_ref, k_ref, v_ref, fk_ref, o_ref = refs
        t_past = 0
    n_q = q_ref.shape[0] // t
    rt = BF16_ROWS

    def update(s, fk, vb, carry, q_minus_k):
        m, l, acc = carry
        tb = s.shape[1]
        if q_minus_k is not None:
            row = lax.broadcasted_iota(jnp.int32, (rt, tb), 0)
            col = lax.broadcasted_iota(jnp.int32, (rt, tb), 1)
        sb = s - fk
        if q_minus_k is not None:
            row_b = lax.broadcasted_iota(jnp.int32, (t, tb), 0)
            col_b = lax.broadcasted_iota(jnp.int32, (t, tb), 1)
            sb = jnp.where(col_b <= row_b + q_minus_k, sb, NEG_INF)
        m_new = jnp.maximum(m, jnp.max(sb, axis=1, keepdims=True))
        alpha = jnp.exp2(m - m_new)
        ps, parts = [], []
        for r in range(t // rt):
            s_t = s[r * rt:(r + 1) * rt] - fk
            if q_minus_k is not None:
                s_t = jnp.where(col <= row + (r * rt + q_minus_k), s_t, NEG_INF)
            p = jnp.exp2(s_t - m_new[r * rt:(r + 1) * rt])
            if tb % LANES == 0:
                parts.append(functools.reduce(
                    jnp.add, [p[:, c * LANES:(c + 1) * LANES] for c in range(tb // LANES)]))
            else:
                lane = lax.broadcasted_iota(jnp.int32, (rt, LANES), 1)
                parts.append(jnp.where(lane == 0, jnp.sum(p, axis=1, keepdims=True), 0.0))
            ps.append(p.astype(BF16))
        l = alpha * l + jnp.concatenate(parts, axis=0)
        acc = alpha * acc + _dot(jnp.concatenate(ps, axis=0), vb)
        return m_new, l, acc

    def q_block(i, carry_unused):
        rows = pl.ds(_aligned(i * t, t), t)
        q = q_ref[rows, :]
        carry = (jnp.full((t, 1), NEG_INF, F32), jnp.zeros((t, LANES), F32), jnp.zeros((t, DH), F32))

        if has_past:
            def past_body(j, c):
                start = pl.multiple_of(j * tp, tp)
                kb = kp_ref[pl.ds(start, tp), :].astype(BF16)
                vb = vp_ref[pl.ds(start, tp), :].astype(BF16)
                return update(_nt_dot(q, kb), fk_ref[:, pl.ds(start, tp)], vb, c, None)

            carry = lax.fori_loop(0, t_past // tp, past_body, carry)

        def own_start(j):
            return _aligned(j * t, t)

        def scores(j):
            return _nt_dot(q, k_ref[pl.ds(own_start(j), t), :].astype(BF16))

        def own_update(s, j, c, diag):
            vb = v_ref[pl.ds(own_start(j), t), :].astype(BF16)
            fk = fk_ref[:, pl.ds(t_past + own_start(j), t)]
            return update(s, fk, vb, c, 0 if diag else None)

        def body(j, c):
            s_cur, inner = c
            s_next = scores(j + 1)
            return s_next, own_update(s_cur, j, inner, False)

        s = scores(0)
        if n_q > 1:
            s, carry = lax.fori_loop(0, i, body, (s, carry))
        _, l, acc = own_update(s, i, carry, True)
        o_ref[rows, :] = (acc / jnp.sum(l, axis=1, keepdims=True)).astype(o_ref.dtype)
        return carry_unused

    if n_q == 1:
        q_block(0, 0)
    else:
        lax.fori_loop(0, n_q, q_block, 0)


def _attn_b(q3, own, past, frow, t, tp):
    kv = list(own) + (list(past) if past else [])
    fspec = pl.BlockSpec((None, None, 1, frow.shape[3]), lambda bi, h: (bi, h, 0, 0))
    return _attn_call(functools.partial(_attn_b_kernel, t=t, tp=tp, has_past=bool(past)),
                      q3, H_A, kv, [(frow, fspec)], H_B)


def _band_table(rel_bias, r):
    w = C_PAST + r
    lh = w + r - 1
    dist = C_PAST + r - 1 - jnp.arange(lh)
    hvec = rel_bias.astype(F32)[:, jnp.clip(dist, -REL_CLIP, REL_CLIP) + REL_CLIP]
    q = lh + 1
    hq = jnp.pad(hvec, ((0, 0), (0, 1)))
    skew = jnp.tile(hq, (1, r + 1))[:, :r * (q + 1)].reshape(H_C, r, q + 1)[:, :, :w]
    bias = skew[:, ::-1, :]
    qi = jnp.arange(r)[:, None] // CHUNK
    kj = jnp.arange(w)[None, :] // CHUNK
    ok = (kj >= qi) & (kj <= qi + C_PREV)
    return jnp.where(ok[None], bias * LOG2E, NEG_INF)


def _attn_c_kernel(*refs, r, has_past):
    if has_past:
        q_ref, k_ref, v_ref, kp_ref, vp_ref, tab_ref, o_ref = refs
    else:
        q_ref, k_ref, v_ref, tab_ref, o_ref = refs
    w = C_PAST + r

    def finish(pieces, rows):
        m = functools.reduce(jnp.maximum, [jnp.max(s, axis=1, keepdims=True) for s, _ in pieces])
        ps = [jnp.exp2(s - m) for s, _ in pieces]
        l = functools.reduce(jnp.add, [jnp.sum(p, axis=1, keepdims=True) for p in ps])
        o = functools.reduce(jnp.add, [_dot(p.astype(BF16), v) for p, (_, v) in zip(ps, pieces)])
        o_ref[rows, :] = (o / l).astype(o_ref.dtype)

    if has_past:
        q = q_ref[...]
        s_p = _nt_dot(q, kp_ref[...].astype(BF16)) + tab_ref[:, 0:C_PAST]
        s_o = _nt_dot(q, k_ref[...].astype(BF16)) + tab_ref[:, C_PAST:w]
        finish([(s_p, vp_ref[...].astype(BF16)), (s_o, v_ref[...].astype(BF16))], pl.ds(0, r))
        return

    def tile(i, start, width, tcol0):
        rows = pl.ds(_aligned(i * r, r), r)
        q = q_ref[rows, :]
        kb = k_ref[pl.ds(start, width), :].astype(BF16)
        vb = v_ref[pl.ds(start, width), :].astype(BF16)
        finish([(_nt_dot(q, kb) + tab_ref[:, tcol0:tcol0 + width], vb)], rows)

    n_tiles = q_ref.shape[0] // r
    n_short = min(C_PAST // r, n_tiles)
    for t in range(n_short):
        tile(t, 0, (t + 1) * r, C_PAST - t * r)

    def full_tile(i, carry):
        tile(i, pl.multiple_of(i * r - C_PAST, SUBLANES), w, 0)
        return carry

    lax.fori_loop(n_short, n_tiles, full_tile, 0)


def _attn_c(q3, own, past, table, r):
    assert C_PAST % r == 0
    kv = list(own) + (list(past) if past else [])
    tspec = pl.BlockSpec((None, r, C_PAST + r), lambda bi, h: (h, 0, 0))
    return _attn_call(functools.partial(_attn_c_kernel, r=r, has_past=bool(past)),
                      q3, H_A + H_B, kv, [(table, tspec)], H_C)


def _outproj_kernel(x_ref, oa_ref, ob_ref, oc_ref, wo_ref, g_ref, wr_ref, br_ref,
                    h_ref, hn_ref, route_ref):
    a0, a1 = H_A * DH, (H_A + H_B) * DH
    acc = _dot(oa_ref[...], wo_ref[0:a0, :])
    acc = acc + _dot(ob_ref[...], wo_ref[a0:a1, :])
    acc = acc + _dot(oc_ref[...], wo_ref[a1:MIX, :])
    h = x_ref[...] + acc
    h_ref[...] = h
    ms = jnp.mean(h * h, axis=-1, keepdims=True)
    hn = h * lax.rsqrt(ms + EPS) * g_ref[...]
    hn_ref[...] = hn

    h1, h2, _ = _split3(hn)
    w1 = wr_ref[0]
    w2 = wr_ref[1]
    lg = _dot(h2, w1) + _dot(h1, w2) + _dot(h1, w1) + br_ref[...]

    lane = lax.broadcasted_iota(jnp.int32, lg.shape, 1)
    lane_f = lane.astype(F32)
    is_g = lane < N_GROUPS
    lgm = jnp.where(is_g, lg, NEG_INF)
    gmax = jnp.max(lgm, axis=1, keepdims=True)
    gidx = jnp.min(jnp.where(lgm == gmax, lane_f, float(LANES)), axis=1, keepdims=True)
    pg = 1.0 / jnp.sum(jnp.where(is_g, jnp.exp(lgm - gmax), 0.0), axis=1, keepdims=True)
    grp = jnp.floor((lane_f - N_GROUPS) * (1.0 / EXP_PER_GROUP))
    in_grp = (lane >= N_GROUPS) & (lane < N_GROUPS + N_EXPERTS) & (grp == gidx)
    le = jnp.where(in_grp, lg, NEG_INF)
    t1 = jnp.max(le, axis=1, keepdims=True)
    i1 = jnp.min(jnp.where(le == t1, lane_f, float(LANES)), axis=1, keepdims=True)
    le2 = jnp.where(lane_f == i1, NEG_INF, le)
    t2 = jnp.max(le2, axis=1, keepdims=True)
    i2 = jnp.min(jnp.where(le2 == t2, lane_f, float(LANES)), axis=1, keepdims=True)
    e = jnp.exp(t2 - t1)
    g1 = pg / (1.0 + e)
    g2 = pg * e / (1.0 + e)
    route = jnp.where(lane == 0, i1 - N_GROUPS,
                      jnp.where(lane == 1, i2 - N_GROUPS,
                                jnp.where(lane == 2, g1, jnp.where(lane == 3, g2, 0.0))))
    route_ref[...] = route


def _outproj(x2d, x_block_off, oa, ob, oc, wo_bf, g, wr2, br, n_total, out_block_off, prev):
    n_rows, d = oa.shape[0], x2d.shape[1]
    tm = PROJ_TM
    nb = n_rows // tm
    in_specs = [
        pl.BlockSpec((tm, d), lambda i: (i + x_block_off, 0)),
        pl.BlockSpec((tm, H_A * DH), lambda i: (i, 0)),
        pl.BlockSpec((tm, H_B * DH), lambda i: (i, 0)),
        pl.BlockSpec((tm, H_C * DH), lambda i: (i, 0)),
        pl.BlockSpec((MIX, d), lambda i: (0, 0)),
        pl.BlockSpec((1, d), lambda i: (0, 0)),
        pl.BlockSpec((2, d, LANES), lambda i: (0, 0, 0)),
        pl.BlockSpec((1, LANES), lambda i: (0, 0)),
    ]
    args = [x2d, oa, ob, oc, wo_bf, g, wr2, br]
    aliases = {}
    if prev is not None:
        for t, arr in enumerate(prev):
            in_specs.append(pl.BlockSpec(memory_space=pl.ANY))
            args.append(arr)
            aliases[8 + t] = t

    def body(*refs):
        _outproj_kernel(*refs[:8], *refs[len(args):len(args) + 3])

    return pl.pallas_call(
        body,
        grid=(nb,),
        in_specs=in_specs,
        out_specs=[pl.BlockSpec((tm, d), lambda i: (i + out_block_off, 0)),
                   pl.BlockSpec((tm, d), lambda i: (i + out_block_off, 0)),
                   pl.BlockSpec((tm, LANES), lambda i: (i + out_block_off, 0))],
        out_shape=[jax.ShapeDtypeStruct((n_total, d), F32),
                   jax.ShapeDtypeStruct((n_total, d), F32),
                   jax.ShapeDtypeStruct((n_total, LANES), F32)],
        input_output_aliases=aliases,
        compiler_params=pltpu.CompilerParams(
            dimension_semantics=("parallel",), vmem_limit_bytes=VMEM_LIMIT),
    )(*args)


def _moe_kernel(be_ref, src_ref, srcn_ref, hn_hbm, wg_ref, wu_ref, wd_ref, yb_ref,
                xbuf, wg16, wu16, wd16, sem):
    i = pl.program_id(0)
    last = pl.num_programs(0) - 1
    blk = xbuf.shape[1]
    slot = i % 2

    def gather(idx_ref, dst_slot):
        for r in range(blk):
            pltpu.make_async_copy(hn_hbm.at[pl.ds(idx_ref[0, 0, r], 1), :],
                                  xbuf.at[dst_slot, pl.ds(r, 1), :], sem.at[dst_slot]).start()

    def wait_rows(s):
        pltpu.make_async_copy(hn_hbm.at[pl.ds(0, blk), :], xbuf.at[s], sem.at[s]).wait()

    @pl.when(i == 0)
    def _():
        gather(src_ref, 0)

    @pl.when(jnp.logical_or(i == 0, be_ref[i] != be_ref[jnp.maximum(i - 1, 0)]))
    def _():
        wg16[...] = wg_ref[...].astype(BF16)
        wu16[...] = wu_ref[...].astype(BF16)
        wd16[...] = wd_ref[...].astype(BF16)

    wait_rows(slot)
    gather(srcn_ref, 1 - slot)
    xb = xbuf[slot].astype(BF16)
    gate = _dot(xb, wg16[...])
    up = _dot(xb, wu16[...])
    act = (gate * jax.nn.sigmoid(gate) * up).astype(BF16)
    yb_ref[...] = _dot(act, wd16[...])

    @pl.when(i == last)
    def _():
        wait_rows(1 - slot)


def _moe(hn, src, block_e, wg, wu, wd, layer, n_blocks):
    d = hn.shape[1]
    de = wg.shape[3]
    src3 = src.reshape(n_blocks, 1, MOE_BLK)
    last = n_blocks - 1
    grid_spec = pltpu.PrefetchScalarGridSpec(
        num_scalar_prefetch=1,
        grid=(n_blocks,),
        in_specs=[
            pl.BlockSpec((1, 1, MOE_BLK), lambda i, be: (i, 0, 0), memory_space=pltpu.SMEM),
            pl.BlockSpec((1, 1, MOE_BLK), lambda i, be: (jnp.minimum(i + 1, last), 0, 0),
                         memory_space=pltpu.SMEM),
            pl.BlockSpec(memory_space=pl.ANY),
            pl.BlockSpec((None, None, d, de), lambda i, be: (layer, be[i], 0, 0)),
            pl.BlockSpec((None, None, d, de), lambda i, be: (layer, be[i], 0, 0)),
            pl.BlockSpec((None, None, de, d), lambda i, be: (layer, be[i], 0, 0)),
        ],
        out_specs=pl.BlockSpec((MOE_BLK, d), lambda i, be: (i, 0)),
        scratch_shapes=[pltpu.VMEM((2, MOE_BLK, d), F32),
                        pltpu.VMEM((d, de), BF16), pltpu.VMEM((d, de), BF16), pltpu.VMEM((de, d), BF16),
                        pltpu.SemaphoreType.DMA((2,))],
    )
    return pl.pallas_call(
        _moe_kernel,
        grid_spec=grid_spec,
        out_shape=jax.ShapeDtypeStruct((n_blocks * MOE_BLK, d), F32),
        compiler_params=pltpu.CompilerParams(
            dimension_semantics=("arbitrary",), vmem_limit_bytes=VMEM_LIMIT),
    )(block_e, src3, src3, hn, wg, wu, wd)


def _dispatch(eid, n_blocks):
    p = eid.shape[0] * eid.shape[1]
    e = eid.reshape(-1)
    onehot = (e[:, None] == jnp.arange(N_EXPERTS, dtype=jnp.int32)[None, :]).astype(jnp.int32)
    cs = jnp.cumsum(onehot, axis=0)
    rank = jnp.sum(cs * onehot, axis=1) - 1
    counts = cs[-1]
    padded = (counts + MOE_BLK - 1) // MOE_BLK * MOE_BLK
    pend = jnp.cumsum(padded)
    pstart = pend - padded
    dest = (jnp.sum(onehot * pstart[None, :], axis=1) + rank).astype(jnp.int32)
    src = jnp.zeros((n_blocks * MOE_BLK,), jnp.int32).at[dest].set(
        jnp.arange(p, dtype=jnp.int32) // eid.shape[1])
    blk_start = jnp.arange(n_blocks, dtype=jnp.int32) * MOE_BLK
    block_e = jnp.minimum(jnp.sum((pend[None, :] <= blk_start[:, None]).astype(jnp.int32), axis=1),
                          N_EXPERTS - 1).astype(jnp.int32)
    return dest, src, block_e


def _combine_kernel(pos_ref, h_ref, route_ref, yb_hbm, g_ref, o_ref, buf, sem, *, final):
    tm = h_ref.shape[0]

    def group(gi, carry):
        base = pl.multiple_of(gi * SUBLANES, SUBLANES)
        for u in range(SUBLANES):
            for s in range(2):
                pltpu.make_async_copy(yb_hbm.at[pl.ds(pos_ref[0, 0, 2 * (base + u) + s], 1), :],
                                      buf.at[s, pl.ds(base + u, 1), :], sem).start()
        return carry

    lax.fori_loop(0, tm // SUBLANES, group, 0)
    pltpu.make_async_copy(yb_hbm.at[pl.ds(0, tm), :], buf.at[0], sem).wait()
    pltpu.make_async_copy(yb_hbm.at[pl.ds(0, tm), :], buf.at[1], sem).wait()
    route = route_ref[...]
    g1 = _lane_pick(route, 2)
    g2 = _lane_pick(route, 3)
    y = h_ref[...] + (g1 * buf[0] + g2 * buf[1])
    if final:
        ms = jnp.mean(y * y, axis=-1, keepdims=True)
        y = y * lax.rsqrt(ms + EPS) * g_ref[...]
    o_ref[...] = y


def _combine(pos3, h, route, yb, g, block_off, n_rows, final):
    d = h.shape[1]
    tm = COMBINE_TM
    return pl.pallas_call(
        functools.partial(_combine_kernel, final=final),
        grid=(n_rows // tm,),
        in_specs=[
            pl.BlockSpec((1, 1, 2 * tm), lambda i: (i + block_off, 0, 0), memory_space=pltpu.SMEM),
            pl.BlockSpec((tm, d), lambda i: (i + block_off, 0)),
            pl.BlockSpec((tm, LANES), lambda i: (i + block_off, 0)),
            pl.BlockSpec(memory_space=pl.ANY),
            pl.BlockSpec((1, d), lambda i: (0, 0)),
        ],
        out_specs=pl.BlockSpec((tm, d), lambda i: (i, 0)),
        out_shape=jax.ShapeDtypeStruct((n_rows, d), F32),
        scratch_shapes=[pltpu.VMEM((2, tm, d), F32), pltpu.SemaphoreType.DMA(())],
        compiler_params=pltpu.CompilerParams(
            dimension_semantics=("arbitrary",), vmem_limit_bytes=VMEM_LIMIT),
    )(pos3, h, route, yb, g)


def _pad_lanes(a):
    return jnp.pad(a, [(0, 0)] * (a.ndim - 1) + [(0, LANES - a.shape[-1])])


def kernel(x_prompt, x_sample, cache_a_k, cache_a_v, cache_b_k, cache_b_v, cache_b_logf,
           cache_c_k, cache_c_v, norm1_g, norm2_g, w_in, b_f, rel_bias, w_o,
           w_rg, b_rg, w_re, b_re, w_gate, w_up, w_down, final_g):
    bp, sp, d = x_prompt.shape
    bs, ss, _ = x_sample.shape
    depth = w_in.shape[0]
    past = cache_a_k.shape[2]
    keep = cache_c_k.shape[2]
    n_p, n_s = bp * sp, bs * ss
    n_tot = n_p + n_s
    assert d == MIX and n_p % PROJ_TM == 0 and n_s % PROJ_TM == 0
    assert ss == CHUNK and keep == C_PAST

    t_a = min(256, sp)
    t_b = min(512, sp)
    r_p = min(256, sp)
    tp_a = min(256, past)
    tp_b = min(512, past)
    assert past % tp_a == 0 and past % tp_b == 0
    n_blocks = (2 * n_tot) // MOE_BLK + N_EXPERTS

    cb_k, cb_v = jnp.swapaxes(cache_b_k, 2, 3), jnp.swapaxes(cache_b_v, 2, 3)
    cc_k, cc_v = jnp.swapaxes(cache_c_k, 2, 3), jnp.swapaxes(cache_c_v, 2, 3)
    ca_k = cache_a_k.reshape(depth, bs, past, H_A * DH)
    ca_v = cache_a_v.reshape(depth, bs, past, H_A * DH)

    xs_p = (x_prompt.reshape(n_p, d), 0)
    xs_s = (x_sample.reshape(n_s, d), 0)
    stacks_p = stacks_s = None
    logf_p, logf_s = [], []
    y_prompt = y_sample = None

    for l in range(depth):
        w_bf = w_in[l, :, :3 * MIX].astype(BF16)
        wf_bf = _pad_lanes(w_in[l, :, 3 * MIX:]).astype(BF16)
        bfp = _pad_lanes(b_f[l][None, :].astype(F32))
        g1 = norm1_g[l][None, :].astype(F32)
        q_p, stacks_p, k16_p, v16_p, lf_p = _inproj(xs_p[0], xs_p[1] * (PROJ_TM // INPROJ_TM), bp, sp, g1,
                                                    w_bf, wf_bf, bfp, l, depth, stacks_p)
        q_s, stacks_s, _, _, lf_s = _inproj(xs_s[0], xs_s[1] * (PROJ_TM // INPROJ_TM), bs, ss, g1,
                                            w_bf, wf_bf, bfp, l, depth, stacks_s)
        logf_p.append(lf_p[:, :H_B].reshape(bp, sp, H_B))
        logf_s.append(lf_s[:, :H_B].reshape(bs, ss, H_B))
        q_p3 = q_p.reshape(bp, sp, MIX)
        q_s3 = q_s.reshape(bs, ss, MIX)
        k16_p, v16_p = k16_p[None], v16_p[None]
        ka_s = stacks_s["ka"].reshape(depth, bs, ss, H_A * DH)
        va_s = stacks_s["va"].reshape(depth, bs, ss, H_A * DH)

        oa_p = _attn_a(q_p3, [(k16_p, 0, 0), (v16_p, 0, 0)], None, t_a, t_a)
        oa_s = _attn_a(q_s3, [(ka_s, l, 0), (va_s, l, 0)], [(ca_k, l, 0), (ca_v, l, 0)], ss, tp_a)

        def row_layout(fcol):
            return (jnp.swapaxes(fcol[:, :, :H_B], 1, 2) * LOG2E)[:, :, None, :]

        fcol_p = _cumsum_seq(lf_p.reshape(bp, sp, LANES), min(512, sp))
        lf_cat = jnp.concatenate([_pad_lanes(cache_b_logf[l].astype(F32)),
                                  lf_s.reshape(bs, ss, LANES)], axis=1)
        t_cat = past + ss
        tc_s = t_cat // 3 if (t_cat % 24 == 0) else t_cat
        fcol_s = _cumsum_seq(lf_cat, tc_s)
        ob_p = _attn_b(q_p3, [(k16_p, 0, H_A), (v16_p, 0, H_A)], None, row_layout(fcol_p), t_b, t_b)
        ob_s = _attn_b(q_s3, [(stacks_s["kb"], l, 0), (stacks_s["vb"], l, 0)],
                       [(cb_k, l, 0), (cb_v, l, 0)], row_layout(fcol_s), ss, tp_b)

        oc_p = _attn_c(q_p3, [(k16_p, 0, H_A + H_B), (v16_p, 0, H_A + H_B)], None,
                       _band_table(rel_bias[l], r_p), r_p)
        oc_s = _attn_c(q_s3, [(stacks_s["kc"], l, 0), (stacks_s["vc"], l, 0)],
                       [(cc_k, l, 0), (cc_v, l, 0)], _band_table(rel_bias[l], ss), ss)

        wo_bf = w_o[l].astype(BF16)
        g2 = norm2_g[l][None, :].astype(F32)
        wr = _pad_lanes(jnp.concatenate([w_rg[l], w_re[l]], axis=1).astype(F32))
        wr1 = wr.astype(BF16)
        wr2 = jnp.stack([wr1, (wr - wr1.astype(F32)).astype(BF16)])
        br = _pad_lanes(jnp.concatenate([b_rg[l], b_re[l]])[None, :].astype(F32))
        shared = _outproj(xs_p[0], xs_p[1], oa_p.reshape(n_p, -1), ob_p.reshape(n_p, -1),
                          oc_p.reshape(n_p, -1), wo_bf, g2, wr2, br, n_tot, 0, None)
        h, hn, route = _outproj(xs_s[0], xs_s[1], oa_s.reshape(n_s, -1), ob_s.reshape(n_s, -1),
                                oc_s.reshape(n_s, -1), wo_bf, g2, wr2, br, n_tot,
                                n_p // PROJ_TM, shared)

        eid = route[:, :2].astype(jnp.int32)
        dest, src, block_e = _dispatch(eid, n_blocks)
        yb = _moe(hn, src, block_e, w_gate, w_up, w_down, l, n_blocks)
        pos3 = dest.reshape(n_tot // COMBINE_TM, 1, 2 * COMBINE_TM)
        fg = final_g[None, :].astype(F32)
        if l + 1 < depth:
            y = _combine(pos3, h, route, yb, fg, 0, n_tot, False)
            xs_p = (y, 0)
            xs_s = (y, n_p // PROJ_TM)
        else:
            y_prompt = _combine(pos3, h, route, yb, fg, 0, n_p, True).reshape(bp, sp, d)
            y_sample = _combine(pos3, h, route, yb, fg, n_p // COMBINE_TM, n_s, True).reshape(bs, ss, d)

    def tok_state(stacks, nm, b, s):
        return stacks[nm].reshape(depth, b, s, H_A, DH)

    def hm_state(stacks, nm):
        return jnp.swapaxes(stacks[nm], 2, 3)

    keep_p = min(C_PAST, sp)
    return (y_prompt, y_sample,
            tok_state(stacks_p, "ka", bp, sp), tok_state(stacks_p, "va", bp, sp),
            hm_state(stacks_p, "kb"), hm_state(stacks_p, "vb"),
            jnp.stack(logf_p, axis=0),
            hm_state(stacks_p, "kc")[:, :, sp - keep_p:], hm_state(stacks_p, "vc")[:, :, sp - keep_p:],
            tok_state(stacks_s, "ka", bs, ss), tok_state(stacks_s, "va", bs, ss),
            hm_state(stacks_s, "kb"), hm_state(stacks_s, "vb"),
            jnp.stack(logf_s, axis=0),
            hm_state(stacks_s, "kc"), hm_state(stacks_s, "vc"))
```

```python
import functools
import math

import jax
import jax.numpy as jnp
from jax import lax
from jax.experimental import pallas as pl
from jax.experimental.pallas import tpu as pltpu

F32 = jnp.float32
BF16 = jnp.bfloat16

DH = 128
H_A, H_B, H_C = 4, 6, 6
N_HEADS = H_A + H_B + H_C
MIX = N_HEADS * DH
CHUNK = 64
C_PREV = 8
C_PAST = C_PREV * CHUNK
REL_CLIP = 128
N_GROUPS = 4
EXP_PER_GROUP = 8
N_EXPERTS = N_GROUPS * EXP_PER_GROUP
EPS = 1e-6
NEG_INF = -1e30
SCALE = DH ** -0.5
LOG2E = math.log2(math.e)

LANES = 128
SUBLANES = 8
BF16_ROWS = 16
INPROJ_TM = 256
PROJ_TM = 512
PROJ_TN = 512
MOE_BLK = 256
COMBINE_TM = 256
SB_CUTOFF = -104.0
VMEM_LIMIT = 56 * 1024 * 1024


def _nt_dot(a, b):
    return lax.dot_general(a, b, (((1,), (1,)), ((), ())), preferred_element_type=F32)


def _dot(a, b):
    return jnp.dot(a, b, preferred_element_type=F32)


def _split3(x):
    x1 = x.astype(BF16)
    r1 = x - x1.astype(F32)
    x2 = r1.astype(BF16)
    x3 = (r1 - x2.astype(F32)).astype(BF16)
    return x1, x2, x3


def _aligned(x, m):
    return x if isinstance(x, int) else pl.multiple_of(x, m)


def _lane_pick(x, idx):
    lane = lax.broadcasted_iota(jnp.int32, x.shape, 1)
    return jnp.sum(jnp.where(lane == idx, x, 0.0), axis=1, keepdims=True)


def _inproj_kernel(x_ref, g_ref, w_hbm, wf_ref, bf_ref,
                   q_ref, ka_ref, va_ref, kb_ref, vb_ref, kc_ref, vc_ref, k16_ref, v16_ref, lf_ref,
                   w_vmem, xn_ref, sem, *, rb):
    tm = x_ref.shape[0]
    nbb = tm // rb
    heads_per_chunk = PROJ_TN // DH
    chunks_per_part = MIX // PROJ_TN

    @pl.when(pl.program_id(0) == 0)
    def _():
        cp = pltpu.make_async_copy(w_hbm, w_vmem, sem)
        cp.start()
        cp.wait()

    x = x_ref[...]
    ms = jnp.mean(x * x, axis=-1, keepdims=True)
    xn_ref[...] = (x * lax.rsqrt(ms + EPS) * g_ref[...]).astype(BF16)
    lf_ref[...] = jax.nn.log_sigmoid(_dot(xn_ref[...], wf_ref[...]) + bf_ref[...])

    def write_kv(jj, acc, tok_ref, b_ref, c_ref, h16_ref):
        for u in range(heads_per_chunk):
            hh = (jj * heads_per_chunk + u) % N_HEADS
            piece = acc[:, u * DH:(u + 1) * DH]
            if hh < H_A:
                tok_ref[:, hh * DH:(hh + 1) * DH] = piece
            for bb in range(nbb):
                rows = piece[bb * rb:(bb + 1) * rb]
                h16_ref[bb, hh] = rows.astype(BF16)
                if H_A <= hh < H_A + H_B:
                    b_ref[bb, hh - H_A] = rows
                elif hh >= H_A + H_B:
                    c_ref[bb, hh - H_A - H_B] = rows

    for jj in range(3 * chunks_per_part):
        acc = _dot(xn_ref[...], w_vmem[:, jj * PROJ_TN:(jj + 1) * PROJ_TN])
        part = jj // chunks_per_part
        if part == 0:
            sc = SCALE if (jj + 1) * heads_per_chunk <= H_A else SCALE * LOG2E
            q_ref[:, jj * PROJ_TN:(jj + 1) * PROJ_TN] = (acc * sc).astype(BF16)
        elif part == 1:
            write_kv(jj, acc, ka_ref, kb_ref, kc_ref, k16_ref)
        else:
            write_kv(jj, acc, va_ref, vb_ref, vc_ref, v16_ref)


_KV_NAMES = ["ka", "va", "kb", "vb", "kc", "vc"]


def _inproj(x2d, row_block_off, nb_batch, seq, g, w_bf, wf_bf, bfp, layer, depth, prev):
    assert H_A * DH == PROJ_TN
    d = x2d.shape[1]
    tm = INPROJ_TM
    n_rows = nb_batch * seq
    nb = n_rows // tm
    rb = min(seq, tm)
    nbb = tm // rb
    spb = seq // rb

    def hm_map(i):
        return (layer, i // spb, 0, i % spb, 0)

    def hm16_map(i):
        return (i // spb, 0, i % spb, 0)

    out_shape = [jax.ShapeDtypeStruct((n_rows, MIX), BF16)]
    out_specs = [pl.BlockSpec((tm, MIX), lambda i: (i, 0))]
    for nm in _KV_NAMES:
        if nm in ("ka", "va"):
            out_shape.append(jax.ShapeDtypeStruct((depth, n_rows, H_A * DH), F32))
            out_specs.append(pl.BlockSpec((None, tm, H_A * DH), lambda i: (layer, i, 0)))
        else:
            hn = H_B if nm[1] == "b" else H_C
            out_shape.append(jax.ShapeDtypeStruct((depth, nb_batch, hn, seq, DH), F32))
            out_specs.append(pl.BlockSpec((None, nbb, hn, rb, DH), hm_map))
    for _ in range(2):
        out_shape.append(jax.ShapeDtypeStruct((nb_batch, N_HEADS, seq, DH), BF16))
        out_specs.append(pl.BlockSpec((nbb, N_HEADS, rb, DH), hm16_map))
    out_shape.append(jax.ShapeDtypeStruct((n_rows, LANES), F32))
    out_specs.append(pl.BlockSpec((tm, LANES), lambda i: (i, 0)))
    n_out = len(out_shape)

    in_specs = [
        pl.BlockSpec((tm, d), lambda i: (i + row_block_off, 0)),
        pl.BlockSpec((1, d), lambda i: (0, 0)),
        pl.BlockSpec(memory_space=pl.ANY),
        pl.BlockSpec((d, LANES), lambda i: (0, 0)),
        pl.BlockSpec((1, LANES), lambda i: (0, 0)),
    ]
    args = [x2d, g, w_bf, wf_bf, bfp]
    n_in = len(args)
    aliases = {}
    if prev is not None:
        for t, nm in enumerate(_KV_NAMES):
            in_specs.append(pl.BlockSpec(memory_space=pl.ANY))
            args.append(prev[nm])
            aliases[n_in + t] = 1 + t

    def body(*refs):
        _inproj_kernel(*refs[:n_in], *refs[len(args):len(args) + n_out], *refs[len(args) + n_out:], rb=rb)

    res = pl.pallas_call(
        body,
        grid=(nb,),
        in_specs=in_specs,
        out_specs=out_specs,
        out_shape=out_shape,
        scratch_shapes=[pltpu.VMEM(w_bf.shape, BF16), pltpu.VMEM((tm, d), BF16),
                        pltpu.SemaphoreType.DMA(())],
        input_output_aliases=aliases,
        compiler_params=pltpu.CompilerParams(
            dimension_semantics=("arbitrary",), vmem_limit_bytes=VMEM_LIMIT),
    )(*args)
    stacks = dict(zip(_KV_NAMES, res[1:7]))
    return res[0], stacks, res[7], res[8], res[9]


def _cumsum_kernel(x_ref, f_ref, carry_ref, *, tc):
    @pl.when(pl.program_id(1) == 0)
    def _():
        carry_ref[...] = jnp.zeros_like(carry_ref)

    x = x_ref[...]
    row = lax.broadcasted_iota(jnp.int32, (tc, tc), 0)
    col = lax.broadcasted_iota(jnp.int32, (tc, tc), 1)
    lower = jnp.where(row >= col, 1.0, 0.0).astype(BF16)
    x1, x2, x3 = _split3(x)
    cs = _dot(lower, x3) + _dot(lower, x2) + _dot(lower, x1) + carry_ref[...]
    f_ref[...] = cs
    carry_ref[...] = cs[tc - 1:tc, :]


def _cumsum_seq(x, tc):
    b, t, _ = x.shape
    return pl.pallas_call(
        functools.partial(_cumsum_kernel, tc=tc),
        grid=(b, t // tc),
        in_specs=[pl.BlockSpec((None, tc, LANES), lambda i, j: (i, j, 0))],
        out_specs=pl.BlockSpec((None, tc, LANES), lambda i, j: (i, j, 0)),
        out_shape=jax.ShapeDtypeStruct((b, t, LANES), F32),
        scratch_shapes=[pltpu.VMEM((1, LANES), F32)],
        compiler_params=pltpu.CompilerParams(
            dimension_semantics=("parallel", "arbitrary")),
    )(x)


def _kv_spec(arr, layer, head_off):
    if arr.ndim == 5:
        return pl.BlockSpec((None, None, None, arr.shape[3], DH),
                            lambda bi, h: (layer, bi, head_off + h, 0, 0))
    return pl.BlockSpec((None, None, arr.shape[2], DH),
                        lambda bi, h: (layer, bi, 0, head_off + h))


def _attn_call(body, q3, q_head_off, kv, extra, n_heads, scratch=()):
    b, tq, _ = q3.shape
    in_specs = [pl.BlockSpec((None, tq, DH), lambda bi, h: (bi, 0, q_head_off + h))]
    args = [q3]
    for arr, layer, head_off in kv:
        in_specs.append(_kv_spec(arr, layer, head_off))
        args.append(arr)
    for arr, spec in extra:
        in_specs.append(spec)
        args.append(arr)
    return pl.pallas_call(
        body,
        grid=(b, n_heads),
        in_specs=in_specs,
        out_specs=pl.BlockSpec((None, tq, DH), lambda bi, h: (bi, 0, h)),
        out_shape=jax.ShapeDtypeStruct((b, tq, n_heads * DH), BF16),
        scratch_shapes=list(scratch),
        compiler_params=pltpu.CompilerParams(
            dimension_semantics=("parallel", "parallel"), vmem_limit_bytes=VMEM_LIMIT),
    )(*args)


def _attn_a_kernel(*refs, t, tp, has_past):
    if has_past:
        q_ref, k_ref, v_ref, kp_ref, vp_ref, o_ref = refs
    else:
        q_ref, k_ref, v_ref, o_ref = refs
    n_q = q_ref.shape[0] // t

    def later_matrix(n):
        row = lax.broadcasted_iota(jnp.int32, (n, n), 0)
        col = lax.broadcasted_iota(jnp.int32, (n, n), 1)
        return jnp.where(row > col, 1.0, 0.0).astype(BF16)

    def cond(carry):
        j, c, _ = carry
        return jnp.logical_and(j >= 0, jnp.max(c) > SB_CUTOFF)

    def make_body(q, kr, vr, tb, diag):
        later = later_matrix(tb)
        row = lax.broadcasted_iota(jnp.int32, (t, tb), 0)
        col = lax.broadcasted_iota(jnp.int32, (t, tb), 1)

        def body(carry):
            j, c, acc = carry
            start = pl.multiple_of(j * tb, tb)
            kb = kr[pl.ds(start, tb), :].astype(BF16)
            vb = vr[pl.ds(start, tb), :].astype(BF16)
            z = _nt_dot(q, kb)
            sp = jnp.maximum(z, 0.0) + jnp.log1p(jnp.exp(-jnp.abs(z)))
            if diag is None:
                lm = -sp
            else:
                mask = col < row + jnp.where(j < diag, tb, 0)
                lm = jnp.where(mask, -sp, 0.0)
            hi = lm.astype(BF16)
            lo = (lm - hi.astype(F32)).astype(BF16)
            suffix = _dot(lo, later) + _dot(hi, later)
            w = jnp.exp(z - sp + suffix + c)
            if diag is not None:
                w = jnp.where(mask, w, 0.0)
            acc = acc + _dot(w.astype(BF16), vb)
            c = c + jnp.sum(lm, axis=1, keepdims=True)
            return j - 1, c, acc

        return body

    def q_block(i, carry):
        rows = pl.ds(_aligned(i * t, t), t)
        q = q_ref[rows, :]
        state = (jnp.asarray(i, jnp.int32), jnp.zeros((t, 1), F32), jnp.zeros((t, DH), F32))
        _, c, acc = lax.while_loop(cond, make_body(q, k_ref, v_ref, t, i), state)
        if has_past:
            n_past = kp_ref.shape[0] // tp
            state = (jnp.asarray(n_past - 1, jnp.int32), c, acc)
            _, c, acc = lax.while_loop(cond, make_body(q, kp_ref, vp_ref, tp, None), state)
        o_ref[rows, :] = acc.astype(o_ref.dtype)
        return carry

    if n_q == 1:
        q_block(0, 0)
    else:
        lax.fori_loop(0, n_q, q_block, 0)


def _attn_a(q3, own, past, t, tp):
    kv = list(own) + (list(past) if past else [])
    return _attn_call(functools.partial(_attn_a_kernel, t=t, tp=tp, has_past=bool(past)),
                      q3, 0, kv, [], H_A)


def _attn_b_kernel(*refs, t, tp, has_past):
    if has_past:
        q_ref, k_ref, v_ref, kp_ref, vp_ref, fk_ref, o_ref = refs
        t_past = kp_ref.shape[0]
    else:
        q_ref, k_ref, v_ref, fk_ref, o_ref = refs
        t_past = 0
    n_q = q_ref.shape[0] // t
    rt = BF16_ROWS

    def update(s, fk, vb, carry, q_minus_k):
        m, l, acc = carry
        tb = s.shape[1]
        if q_minus_k is not None:
            row = lax.broadcasted_iota(jnp.int32, (rt, tb), 0)
            col = lax.broadcasted_iota(jnp.int32, (rt, tb), 1)
        sb = s - fk
        if q_minus_k is not None:
            row_b = lax.broadcasted_iota(jnp.int32, (t, tb), 0)
            col_b = lax.broadcasted_iota(jnp.int32, (t, tb), 1)
            sb = jnp.where(col_b <= row_b + q_minus_k, sb, NEG_INF)
        m_new = jnp.maximum(m, jnp.max(sb, axis=1, keepdims=True))
        alpha = jnp.exp2(m - m_new)
        ps, parts = [], []
        for r in range(t // rt):
            s_t = s[r * rt:(r + 1) * rt] - fk
            if q_minus_k is not None:
                s_t = jnp.where(col <= row + (r * rt + q_minus_k), s_t, NEG_INF)
            p = jnp.exp2(s_t - m_new[r * rt:(r + 1) * rt])
            if tb % LANES == 0:
                parts.append(functools.reduce(
                    jnp.add, [p[:, c * LANES:(c + 1) * LANES] for c in range(tb // LANES)]))
            else:
                lane = lax.broadcasted_iota(jnp.int32, (rt, LANES), 1)
                parts.append(jnp.where(lane == 0, jnp.sum(p, axis=1, keepdims=True), 0.0))
            ps.append(p.astype(BF16))
        l = alpha * l + jnp.concatenate(parts, axis=0)
        acc = alpha * acc + _dot(jnp.concatenate(ps, axis=0), vb)
        return m_new, l, acc

    def q_block(i, carry_unused):
        rows = pl.ds(_aligned(i * t, t), t)
        q = q_ref[rows, :]
        carry = (jnp.full((t, 1), NEG_INF, F32), jnp.zeros((t, LANES), F32), jnp.zeros((t, DH), F32))

        if has_past:
            def past_body(j, c):
                start = pl.multiple_of(j * tp, tp)
                kb = kp_ref[pl.ds(start, tp), :].astype(BF16)
                vb = vp_ref[pl.ds(start, tp), :].astype(BF16)
                return update(_nt_dot(q, kb), fk_ref[:, pl.ds(start, tp)], vb, c, None)

            carry = lax.fori_loop(0, t_past // tp, past_body, carry)

        def own_start(j):
            return _aligned(j * t, t)

        def scores(j):
            return _nt_dot(q, k_ref[pl.ds(own_start(j), t), :].astype(BF16))

        def own_update(s, j, c, diag):
            vb = v_ref[pl.ds(own_start(j), t), :].astype(BF16)
            fk = fk_ref[:, pl.ds(t_past + own_start(j), t)]
            return update(s, fk, vb, c, 0 if diag else None)

        def body(j, c):
            s_cur, inner = c
            s_next = scores(j + 1)
            return s_next, own_update(s_cur, j, inner, False)

        s = scores(0)
        if n_q > 1:
            s, carry = lax.fori_loop(0, i, body, (s, carry))
        _, l, acc = own_update(s, i, carry, True)
        o_ref[rows, :] = (acc / jnp.sum(l, axis=1, keepdims=True)).astype(o_ref.dtype)
        return carry_unused

    if n_q == 1:
        q_block(0, 0)
    else:
        lax.fori_loop(0, n_q, q_block, 0)


def _attn_b_blocks_kernel(q_ref, k_ref, v_ref, fk_ref, o_ref,
                          s0_ref, s1_ref, p_ref, m_ref, l_ref, acc_ref, *, t):
    n_q = q_ref.shape[0] // t
    rt = BF16_ROWS

    def q_block(i, carry):
        rows = pl.ds(pl.multiple_of(i * t, t), t)
        q = q_ref[rows, :]
        m_ref[...] = jnp.full(m_ref.shape, NEG_INF, F32)
        l_ref[...] = jnp.zeros(l_ref.shape, F32)
        acc_ref[...] = jnp.zeros(acc_ref.shape, F32)

        def keys(j):
            return pl.ds(pl.multiple_of(j * t, t), t)

        def put_scores(dst, j):
            dst[...] = _nt_dot(q, k_ref[keys(j), :]) - fk_ref[:, keys(j)]

        def consume(src, j, diag):
            sb = src[...]
            if diag:
                row_b = lax.broadcasted_iota(jnp.int32, (t, t), 0)
                col_b = lax.broadcasted_iota(jnp.int32, (t, t), 1)
                sb = jnp.where(col_b <= row_b, sb, NEG_INF)
                row = lax.broadcasted_iota(jnp.int32, (rt, t), 0)
                col = lax.broadcasted_iota(jnp.int32, (rt, t), 1)
            m_old = m_ref[...]
            m_new = jnp.maximum(m_old, jnp.max(sb, axis=1, keepdims=True))
            alpha = jnp.exp2(m_old - m_new)
            m_ref[...] = m_new
            for r in range(t // rt):
                sl = slice(r * rt, (r + 1) * rt)
                s_t = src[sl, :]
                if diag:
                    s_t = jnp.where(col <= row + r * rt, s_t, NEG_INF)
                p = jnp.exp2(s_t - m_new[sl])
                part = functools.reduce(jnp.add, [p[:, c * LANES:(c + 1) * LANES] for c in range(t // LANES)])
                l_ref[sl, :] = alpha[sl] * l_ref[sl, :] + part
                p_ref[sl, :] = p.astype(BF16)
            acc_ref[...] = alpha * acc_ref[...] + _dot(p_ref[...], v_ref[keys(j), :])

        put_scores(s0_ref, 0)

        def pair(pp, c):
            j = 2 * pp
            put_scores(s1_ref, j + 1)
            consume(s0_ref, j, False)
            put_scores(s0_ref, j + 2)
            consume(s1_ref, j + 1, False)
            return c

        lax.fori_loop(0, i // 2, pair, 0)

        @pl.when(i % 2 == 1)
        def _():
            put_scores(s1_ref, i)
            consume(s0_ref, i - 1, False)
            consume(s1_ref, i, True)

        @pl.when(i % 2 == 0)
        def _():
            consume(s0_ref, i, True)

        o_ref[rows, :] = (acc_ref[...] / jnp.sum(l_ref[...], axis=1, keepdims=True)).astype(o_ref.dtype)
        return carry

    lax.fori_loop(0, n_q, q_block, 0)


def _attn_b(q3, own, past, frow, t, tp):
    kv = list(own) + (list(past) if past else [])
    fspec = pl.BlockSpec((None, None, 1, frow.shape[3]), lambda bi, h: (bi, h, 0, 0))
    if not past and t % LANES == 0 and own[0][0].dtype == BF16:
        scratch = [pltpu.VMEM((t, t), F32), pltpu.VMEM((t, t), F32), pltpu.VMEM((t, t), BF16),
                   pltpu.VMEM((t, 1), F32), pltpu.VMEM((t, LANES), F32), pltpu.VMEM((t, DH), F32)]
        return _attn_call(functools.partial(_attn_b_blocks_kernel, t=t),
                          q3, H_A, kv, [(frow, fspec)], H_B, scratch)
    return _attn_call(functools.partial(_attn_b_kernel, t=t, tp=tp, has_past=bool(past)),
                      q3, H_A, kv, [(frow, fspec)], H_B)


def _band_table(rel_bias, r):
    w = C_PAST + r
    lh = w + r - 1
    dist = C_PAST + r - 1 - jnp.arange(lh)
    hvec = rel_bias.astype(F32)[:, jnp.clip(dist, -REL_CLIP, REL_CLIP) + REL_CLIP]
    q = lh + 1
    hq = jnp.pad(hvec, ((0, 0), (0, 1)))
    skew = jnp.tile(hq, (1, r + 1))[:, :r * (q + 1)].reshape(H_C, r, q + 1)[:, :, :w]
    bias = skew[:, ::-1, :]
    qi = jnp.arange(r)[:, None] // CHUNK
    kj = jnp.arange(w)[None, :] // CHUNK
    ok = (kj >= qi) & (kj <= qi + C_PREV)
    return jnp.where(ok[None], bias * LOG2E, NEG_INF)


def _attn_c_kernel(*refs, r, has_past):
    if has_past:
        q_ref, k_ref, v_ref, kp_ref, vp_ref, tab_ref, o_ref = refs
    else:
        q_ref, k_ref, v_ref, tab_ref, o_ref = refs
    w = C_PAST + r

    def finish(pieces, rows):
        m = functools.reduce(jnp.maximum, [jnp.max(s, axis=1, keepdims=True) for s, _ in pieces])
        ps = [jnp.exp2(s - m) for s, _ in pieces]
        l = functools.reduce(jnp.add, [jnp.sum(p, axis=1, keepdims=True) for p in ps])
        o = functools.reduce(jnp.add, [_dot(p.astype(BF16), v) for p, (_, v) in zip(ps, pieces)])
        o_ref[rows, :] = (o / l).astype(o_ref.dtype)

    if has_past:
        q = q_ref[...]
        s_p = _nt_dot(q, kp_ref[...].astype(BF16)) + tab_ref[:, 0:C_PAST]
        s_o = _nt_dot(q, k_ref[...].astype(BF16)) + tab_ref[:, C_PAST:w]
        finish([(s_p, vp_ref[...].astype(BF16)), (s_o, v_ref[...].astype(BF16))], pl.ds(0, r))
        return

    def tile(i, start, width, tcol0):
        rows = pl.ds(_aligned(i * r, r), r)
        q = q_ref[rows, :]
        kb = k_ref[pl.ds(start, width), :].astype(BF16)
        vb = v_ref[pl.ds(start, width), :].astype(BF16)
        finish([(_nt_dot(q, kb) + tab_ref[:, tcol0:tcol0 + width], vb)], rows)

    n_tiles = q_ref.shape[0] // r
    n_short = min(C_PAST // r, n_tiles)
    for t in range(n_short):
        tile(t, 0, (t + 1) * r, C_PAST - t * r)

    def full_tile(i, carry):
        tile(i, pl.multiple_of(i * r - C_PAST, SUBLANES), w, 0)
        return carry

    lax.fori_loop(n_short, n_tiles, full_tile, 0)


def _attn_c(q3, own, past, table, r):
    assert C_PAST % r == 0
    kv = list(own) + (list(past) if past else [])
    tspec = pl.BlockSpec((None, r, C_PAST + r), lambda bi, h: (h, 0, 0))
    return _attn_call(functools.partial(_attn_c_kernel, r=r, has_past=bool(past)),
                      q3, H_A + H_B, kv, [(table, tspec)], H_C)


def _outproj_kernel(x_ref, oa_ref, ob_ref, oc_ref, wo_ref, g_ref, wr_ref, br_ref,
                    h_ref, hn_ref, route_ref):
    a0, a1 = H_A * DH, (H_A + H_B) * DH
    acc = _dot(oa_ref[...], wo_ref[0:a0, :])
    acc = acc + _dot(ob_ref[...], wo_ref[a0:a1, :])
    acc = acc + _dot(oc_ref[...], wo_ref[a1:MIX, :])
    h = x_ref[...] + acc
    h_ref[...] = h
    ms = jnp.mean(h * h, axis=-1, keepdims=True)
    hn = h * lax.rsqrt(ms + EPS) * g_ref[...]
    hn_ref[...] = hn

    h1, h2, _ = _split3(hn)
    w1 = wr_ref[0]
    w2 = wr_ref[1]
    lg = _dot(h2, w1) + _dot(h1, w2) + _dot(h1, w1) + br_ref[...]

    lane = lax.broadcasted_iota(jnp.int32, lg.shape, 1)
    lane_f = lane.astype(F32)
    is_g = lane < N_GROUPS
    lgm = jnp.where(is_g, lg, NEG_INF)
    gmax = jnp.max(lgm, axis=1, keepdims=True)
    gidx = jnp.min(jnp.where(lgm == gmax, lane_f, float(LANES)), axis=1, keepdims=True)
    pg = 1.0 / jnp.sum(jnp.where(is_g, jnp.exp(lgm - gmax), 0.0), axis=1, keepdims=True)
    grp = jnp.floor((lane_f - N_GROUPS) * (1.0 / EXP_PER_GROUP))
    in_grp = (lane >= N_GROUPS) & (lane < N_GROUPS + N_EXPERTS) & (grp == gidx)
    le = jnp.where(in_grp, lg, NEG_INF)
    t1 = jnp.max(le, axis=1, keepdims=True)
    i1 = jnp.min(jnp.where(le == t1, lane_f, float(LANES)), axis=1, keepdims=True)
    le2 = jnp.where(lane_f == i1, NEG_INF, le)
    t2 = jnp.max(le2, axis=1, keepdims=True)
    i2 = jnp.min(jnp.where(le2 == t2, lane_f, float(LANES)), axis=1, keepdims=True)
    e = jnp.exp(t2 - t1)
    g1 = pg / (1.0 + e)
    g2 = pg * e / (1.0 + e)
    route = jnp.where(lane == 0, i1 - N_GROUPS,
                      jnp.where(lane == 1, i2 - N_GROUPS,
                                jnp.where(lane == 2, g1, jnp.where(lane == 3, g2, 0.0))))
    route_ref[...] = route


def _outproj(x2d, x_block_off, oa, ob, oc, wo_bf, g, wr2, br, n_total, out_block_off, prev):
    n_rows, d = oa.shape[0], x2d.shape[1]
    tm = PROJ_TM
    nb = n_rows // tm
    in_specs = [
        pl.BlockSpec((tm, d), lambda i: (i + x_block_off, 0)),
        pl.BlockSpec((tm, H_A * DH), lambda i: (i, 0)),
        pl.BlockSpec((tm, H_B * DH), lambda i: (i, 0)),
        pl.BlockSpec((tm, H_C * DH), lambda i: (i, 0)),
        pl.BlockSpec((MIX, d), lambda i: (0, 0)),
        pl.BlockSpec((1, d), lambda i: (0, 0)),
        pl.BlockSpec((2, d, LANES), lambda i: (0, 0, 0)),
        pl.BlockSpec((1, LANES), lambda i: (0, 0)),
    ]
    args = [x2d, oa, ob, oc, wo_bf, g, wr2, br]
    aliases = {}
    if prev is not None:
        for t, arr in enumerate(prev):
            in_specs.append(pl.BlockSpec(memory_space=pl.ANY))
            args.append(arr)
            aliases[8 + t] = t

    def body(*refs):
        _outproj_kernel(*refs[:8], *refs[len(args):len(args) + 3])

    return pl.pallas_call(
        body,
        grid=(nb,),
        in_specs=in_specs,
        out_specs=[pl.BlockSpec((tm, d), lambda i: (i + out_block_off, 0)),
                   pl.BlockSpec((tm, d), lambda i: (i + out_block_off, 0)),
                   pl.BlockSpec((tm, LANES), lambda i: (i + out_block_off, 0))],
        out_shape=[jax.ShapeDtypeStruct((n_total, d), F32),
                   jax.ShapeDtypeStruct((n_total, d), F32),
                   jax.ShapeDtypeStruct((n_total, LANES), F32)],
        input_output_aliases=aliases,
        compiler_params=pltpu.CompilerParams(
            dimension_semantics=("parallel",), vmem_limit_bytes=VMEM_LIMIT),
    )(*args)


def _moe_kernel(be_ref, src_ref, srcn_ref, hn_hbm, wg_ref, wu_ref, wd_ref, yb_ref,
                xbuf, wg16, wu16, wd16, sem):
    i = pl.program_id(0)
    last = pl.num_programs(0) - 1
    blk = xbuf.shape[1]
    slot = i % 2

    def gather(idx_ref, dst_slot):
        for r in range(blk):
            pltpu.make_async_copy(hn_hbm.at[pl.ds(idx_ref[0, 0, r], 1), :],
                                  xbuf.at[dst_slot, pl.ds(r, 1), :], sem.at[dst_slot]).start()

    def wait_rows(s):
        pltpu.make_async_copy(hn_hbm.at[pl.ds(0, blk), :], xbuf.at[s], sem.at[s]).wait()

    @pl.when(i == 0)
    def _():
        gather(src_ref, 0)

    @pl.when(jnp.logical_or(i == 0, be_ref[i] != be_ref[jnp.maximum(i - 1, 0)]))
    def _():
        wg16[...] = wg_ref[...].astype(BF16)
        wu16[...] = wu_ref[...].astype(BF16)
        wd16[...] = wd_ref[...].astype(BF16)

    wait_rows(slot)
    gather(srcn_ref, 1 - slot)
    xb = xbuf[slot].astype(BF16)
    gate = _dot(xb, wg16[...])
    up = _dot(xb, wu16[...])
    act = (gate * jax.nn.sigmoid(gate) * up).astype(BF16)
    yb_ref[...] = _dot(act, wd16[...])

    @pl.when(i == last)
    def _():
        wait_rows(1 - slot)


def _moe(hn, src, block_e, wg, wu, wd, layer, n_blocks):
    d = hn.shape[1]
    de = wg.shape[3]
    src3 = src.reshape(n_blocks, 1, MOE_BLK)
    last = n_blocks - 1
    grid_spec = pltpu.PrefetchScalarGridSpec(
        num_scalar_prefetch=1,
        grid=(n_blocks,),
        in_specs=[
            pl.BlockSpec((1, 1, MOE_BLK), lambda i, be: (i, 0, 0), memory_space=pltpu.SMEM),
            pl.BlockSpec((1, 1, MOE_BLK), lambda i, be: (jnp.minimum(i + 1, last), 0, 0),
                         memory_space=pltpu.SMEM),
            pl.BlockSpec(memory_space=pl.ANY),
            pl.BlockSpec((None, None, d, de), lambda i, be: (layer, be[i], 0, 0)),
            pl.BlockSpec((None, None, d, de), lambda i, be: (layer, be[i], 0, 0)),
            pl.BlockSpec((None, None, de, d), lambda i, be: (layer, be[i], 0, 0)),
        ],
        out_specs=pl.BlockSpec((MOE_BLK, d), lambda i, be: (i, 0)),
        scratch_shapes=[pltpu.VMEM((2, MOE_BLK, d), F32),
                        pltpu.VMEM((d, de), BF16), pltpu.VMEM((d, de), BF16), pltpu.VMEM((de, d), BF16),
                        pltpu.SemaphoreType.DMA((2,))],
    )
    return pl.pallas_call(
        _moe_kernel,
        grid_spec=grid_spec,
        out_shape=jax.ShapeDtypeStruct((n_blocks * MOE_BLK, d), F32),
        compiler_params=pltpu.CompilerParams(
            dimension_semantics=("arbitrary",), vmem_limit_bytes=VMEM_LIMIT),
    )(block_e, src3, src3, hn, wg, wu, wd)


def _dispatch(eid, n_blocks):
    p = eid.shape[0] * eid.shape[1]
    e = eid.reshape(-1)
    onehot = (e[:, None] == jnp.arange(N_EXPERTS, dtype=jnp.int32)[None, :]).astype(jnp.int32)
    cs = jnp.cumsum(onehot, axis=0)
    rank = jnp.sum(cs * onehot, axis=1) - 1
    counts = cs[-1]
    padded = (counts + MOE_BLK - 1) // MOE_BLK * MOE_BLK
    pend = jnp.cumsum(padded)
    pstart = pend - padded
    dest = (jnp.sum(onehot * pstart[None, :], axis=1) + rank).astype(jnp.int32)
    src = jnp.zeros((n_blocks * MOE_BLK,), jnp.int32).at[dest].set(
        jnp.arange(p, dtype=jnp.int32) // eid.shape[1])
    blk_start = jnp.arange(n_blocks, dtype=jnp.int32) * MOE_BLK
    block_e = jnp.minimum(jnp.sum((pend[None, :] <= blk_start[:, None]).astype(jnp.int32), axis=1),
                          N_EXPERTS - 1).astype(jnp.int32)
    return dest, src, block_e


def _combine_kernel(pos_ref, h_ref, route_ref, yb_hbm, g_ref, o_ref, buf, sem, *, final):
    tm = h_ref.shape[0]

    def group(gi, carry):
        base = pl.multiple_of(gi * SUBLANES, SUBLANES)
        for u in range(SUBLANES):
            for s in range(2):
                pltpu.make_async_copy(yb_hbm.at[pl.ds(pos_ref[0, 0, 2 * (base + u) + s], 1), :],
                                      buf.at[s, pl.ds(base + u, 1), :], sem).start()
        return carry

    lax.fori_loop(0, tm // SUBLANES, group, 0)
    pltpu.make_async_copy(yb_hbm.at[pl.ds(0, tm), :], buf.at[0], sem).wait()
    pltpu.make_async_copy(yb_hbm.at[pl.ds(0, tm), :], buf.at[1], sem).wait()
    route = route_ref[...]
    g1 = _lane_pick(route, 2)
    g2 = _lane_pick(route, 3)
    y = h_ref[...] + (g1 * buf[0] + g2 * buf[1])
    if final:
        ms = jnp.mean(y * y, axis=-1, keepdims=True)
        y = y * lax.rsqrt(ms + EPS) * g_ref[...]
    o_ref[...] = y


def _combine(pos3, h, route, yb, g, block_off, n_rows, final):
    d = h.shape[1]
    tm = COMBINE_TM
    return pl.pallas_call(
        functools.partial(_combine_kernel, final=final),
        grid=(n_rows // tm,),
        in_specs=[
            pl.BlockSpec((1, 1, 2 * tm), lambda i: (i + block_off, 0, 0), memory_space=pltpu.SMEM),
            pl.BlockSpec((tm, d), lambda i: (i + block_off, 0)),
            pl.BlockSpec((tm, LANES), lambda i: (i + block_off, 0)),
            pl.BlockSpec(memory_space=pl.ANY),
            pl.BlockSpec((1, d), lambda i: (0, 0)),
        ],
        out_specs=pl.BlockSpec((tm, d), lambda i: (i, 0)),
        out_shape=jax.ShapeDtypeStruct((n_rows, d), F32),
        scratch_shapes=[pltpu.VMEM((2, tm, d), F32), pltpu.SemaphoreType.DMA(())],
        compiler_params=pltpu.CompilerParams(
            dimension_semantics=("arbitrary",), vmem_limit_bytes=VMEM_LIMIT),
    )(pos3, h, route, yb, g)


def _pad_lanes(a):
    return jnp.pad(a, [(0, 0)] * (a.ndim - 1) + [(0, LANES - a.shape[-1])])


def kernel(x_prompt, x_sample, cache_a_k, cache_a_v, cache_b_k, cache_b_v, cache_b_logf,
           cache_c_k, cache_c_v, norm1_g, norm2_g, w_in, b_f, rel_bias, w_o,
           w_rg, b_rg, w_re, b_re, w_gate, w_up, w_down, final_g):
    bp, sp, d = x_prompt.shape
    bs, ss, _ = x_sample.shape
    depth = w_in.shape[0]
    past = cache_a_k.shape[2]
    keep = cache_c_k.shape[2]
    n_p, n_s = bp * sp, bs * ss
    n_tot = n_p + n_s
    assert d == MIX and n_p % PROJ_TM == 0 and n_s % PROJ_TM == 0
    assert ss == CHUNK and keep == C_PAST

    t_a = min(256, sp)
    t_b = min(512, sp)
    r_p = min(256, sp)
    tp_a = min(256, past)
    tp_b = min(512, past)
    assert past % tp_a == 0 and past % tp_b == 0
    n_blocks = (2 * n_tot) // MOE_BLK + N_EXPERTS

    cb_k, cb_v = jnp.swapaxes(cache_b_k, 2, 3), jnp.swapaxes(cache_b_v, 2, 3)
    cc_k, cc_v = jnp.swapaxes(cache_c_k, 2, 3), jnp.swapaxes(cache_c_v, 2, 3)
    ca_k = cache_a_k.reshape(depth, bs, past, H_A * DH)
    ca_v = cache_a_v.reshape(depth, bs, past, H_A * DH)

    xs_p = (x_prompt.reshape(n_p, d), 0)
    xs_s = (x_sample.reshape(n_s, d), 0)
    stacks_p = stacks_s = None
    logf_p, logf_s = [], []
    y_prompt = y_sample = None

    for l in range(depth):
        w_bf = w_in[l, :, :3 * MIX].astype(BF16)
        wf_bf = _pad_lanes(w_in[l, :, 3 * MIX:]).astype(BF16)
        bfp = _pad_lanes(b_f[l][None, :].astype(F32))
        g1 = norm1_g[l][None, :].astype(F32)
        q_p, stacks_p, k16_p, v16_p, lf_p = _inproj(xs_p[0], xs_p[1] * (PROJ_TM // INPROJ_TM), bp, sp, g1,
                                                    w_bf, wf_bf, bfp, l, depth, stacks_p)
        q_s, stacks_s, _, _, lf_s = _inproj(xs_s[0], xs_s[1] * (PROJ_TM // INPROJ_TM), bs, ss, g1,
                                            w_bf, wf_bf, bfp, l, depth, stacks_s)
        logf_p.append(lf_p[:, :H_B].reshape(bp, sp, H_B))
        logf_s.append(lf_s[:, :H_B].reshape(bs, ss, H_B))
        q_p3 = q_p.reshape(bp, sp, MIX)
        q_s3 = q_s.reshape(bs, ss, MIX)
        k16_p, v16_p = k16_p[None], v16_p[None]
        ka_s = stacks_s["ka"].reshape(depth, bs, ss, H_A * DH)
        va_s = stacks_s["va"].reshape(depth, bs, ss, H_A * DH)

        oa_p = _attn_a(q_p3, [(k16_p, 0, 0), (v16_p, 0, 0)], None, t_a, t_a)
        oa_s = _attn_a(q_s3, [(ka_s, l, 0), (va_s, l, 0)], [(ca_k, l, 0), (ca_v, l, 0)], ss, tp_a)

        def row_layout(fcol):
            return (jnp.swapaxes(fcol[:, :, :H_B], 1, 2) * LOG2E)[:, :, None, :]

        fcol_p = _cumsum_seq(lf_p.reshape(bp, sp, LANES), min(512, sp))
        lf_cat = jnp.concatenate([_pad_lanes(cache_b_logf[l].astype(F32)),
                                  lf_s.reshape(bs, ss, LANES)], axis=1)
        t_cat = past + ss
        tc_s = t_cat // 3 if (t_cat % 24 == 0) else t_cat
        fcol_s = _cumsum_seq(lf_cat, tc_s)
        ob_p = _attn_b(q_p3, [(k16_p, 0, H_A), (v16_p, 0, H_A)], None, row_layout(fcol_p), t_b, t_b)
        ob_s = _attn_b(q_s3, [(stacks_s["kb"], l, 0), (stacks_s["vb"], l, 0)],
                       [(cb_k, l, 0), (cb_v, l, 0)], row_layout(fcol_s), ss, tp_b)

        oc_p = _attn_c(q_p3, [(k16_p, 0, H_A + H_B), (v16_p, 0, H_A + H_B)], None,
                       _band_table(rel_bias[l], r_p), r_p)
        oc_s = _attn_c(q_s3, [(stacks_s["kc"], l, 0), (stacks_s["vc"], l, 0)],
                       [(cc_k, l, 0), (cc_v, l, 0)], _band_table(rel_bias[l], ss), ss)

        wo_bf = w_o[l].astype(BF16)
        g2 = norm2_g[l][None, :].astype(F32)
        wr = _pad_lanes(jnp.concatenate([w_rg[l], w_re[l]], axis=1).astype(F32))
        wr1 = wr.astype(BF16)
        wr2 = jnp.stack([wr1, (wr - wr1.astype(F32)).astype(BF16)])
        br = _pad_lanes(jnp.concatenate([b_rg[l], b_re[l]])[None, :].astype(F32))
        shared = _outproj(xs_p[0], xs_p[1], oa_p.reshape(n_p, -1), ob_p.reshape(n_p, -1),
                          oc_p.reshape(n_p, -1), wo_bf, g2, wr2, br, n_tot, 0, None)
        h, hn, route = _outproj(xs_s[0], xs_s[1], oa_s.reshape(n_s, -1), ob_s.reshape(n_s, -1),
                                oc_s.reshape(n_s, -1), wo_bf, g2, wr2, br, n_tot,
                                n_p // PROJ_TM, shared)

        eid = route[:, :2].astype(jnp.int32)
        dest, src, block_e = _dispatch(eid, n_blocks)
        yb = _moe(hn, src, block_e, w_gate, w_up, w_down, l, n_blocks)
        pos3 = dest.reshape(n_tot // COMBINE_TM, 1, 2 * COMBINE_TM)
        fg = final_g[None, :].astype(F32)
        if l + 1 < depth:
            y = _combine(pos3, h, route, yb, fg, 0, n_tot, False)
            xs_p = (y, 0)
            xs_s = (y, n_p // PROJ_TM)
        else:
            y_prompt = _combine(pos3, h, route, yb, fg, 0, n_p, True).reshape(bp, sp, d)
            y_sample = _combine(pos3, h, route, yb, fg, n_p // COMBINE_TM, n_s, True).reshape(bs, ss, d)

    def tok_state(stacks, nm, b, s):
        return stacks[nm].reshape(depth, b, s, H_A, DH)

    def hm_state(stacks, nm):
        return jnp.swapaxes(stacks[nm], 2, 3)

    keep_p = min(C_PAST, sp)
    return (y_prompt, y_sample,
            tok_state(stacks_p, "ka", bp, sp), tok_state(stacks_p, "va", bp, sp),
            hm_state(stacks_p, "kb"), hm_state(stacks_p, "vb"),
            jnp.stack(logf_p, axis=0),
            hm_state(stacks_p, "kc")[:, :, sp - keep_p:], hm_state(stacks_p, "vc")[:, :, sp - keep_p:],
            tok_state(stacks_s, "ka", bs, ss), tok_state(stacks_s, "va", bs, ss),
            hm_state(stacks_s, "kb"), hm_state(stacks_s, "vb"),
            jnp.stack(logf_s, axis=0),
            hm_state(stacks_s, "kc"), hm_state(stacks_s, "vc"))
```

```python
import functools
import math

import jax
import jax.numpy as jnp
from jax import lax
from jax.experimental import pallas as pl
from jax.experimental.pallas import tpu as pltpu

F32 = jnp.float32
BF16 = jnp.bfloat16

DH = 128
H_A, H_B, H_C = 4, 6, 6
N_HEADS = H_A + H_B + H_C
MIX = N_HEADS * DH
CHUNK = 64
C_PREV = 8
C_PAST = C_PREV * CHUNK
REL_CLIP = 128
N_GROUPS = 4
EXP_PER_GROUP = 8
N_EXPERTS = N_GROUPS * EXP_PER_GROUP
EPS = 1e-6
NEG_INF = -1e30
SCALE = DH ** -0.5
LOG2E = math.log2(math.e)

LANES = 128
SUBLANES = 8
BF16_ROWS = 16
INPROJ_TM = 256
PROJ_TM = 512
PROJ_TN = 512
MOE_BLK = 256
COMBINE_TM = 256
SB_CUTOFF = -104.0
VMEM_LIMIT = 56 * 1024 * 1024


def _nt_dot(a, b):
    return lax.dot_general(a, b, (((1,), (1,)), ((), ())), preferred_element_type=F32)


def _dot(a, b):
    return jnp.dot(a, b, preferred_element_type=F32)


def _split3(x):
    x1 = x.astype(BF16)
    r1 = x - x1.astype(F32)
    x2 = r1.astype(BF16)
    x3 = (r1 - x2.astype(F32)).astype(BF16)
    return x1, x2, x3


def _aligned(x, m):
    return x if isinstance(x, int) else pl.multiple_of(x, m)


def _lane_pick(x, idx):
    lane = lax.broadcasted_iota(jnp.int32, x.shape, 1)
    return jnp.sum(jnp.where(lane == idx, x, 0.0), axis=1, keepdims=True)


def _inproj_kernel(x_ref, g_ref, w_hbm, wf_ref, bf_ref,
                   q_ref, ka_ref, va_ref, kb_ref, vb_ref, kc_ref, vc_ref, k16_ref, v16_ref, lf_ref,
                   w_vmem, xn_ref, sem, *, rb):
    tm = x_ref.shape[0]
    nbb = tm // rb
    heads_per_chunk = PROJ_TN // DH
    chunks_per_part = MIX // PROJ_TN

    @pl.when(pl.program_id(0) == 0)
    def _():
        cp = pltpu.make_async_copy(w_hbm, w_vmem, sem)
        cp.start()
        cp.wait()

    x = x_ref[...]
    ms = jnp.mean(x * x, axis=-1, keepdims=True)
    xn_ref[...] = (x * lax.rsqrt(ms + EPS) * g_ref[...]).astype(BF16)
    lf_ref[...] = jax.nn.log_sigmoid(_dot(xn_ref[...], wf_ref[...]) + bf_ref[...])

    def write_kv(jj, acc, tok_ref, b_ref, c_ref, h16_ref):
        for u in range(heads_per_chunk):
            hh = (jj * heads_per_chunk + u) % N_HEADS
            piece = acc[:, u * DH:(u + 1) * DH]
            for bb in range(nbb):
                rows = piece[bb * rb:(bb + 1) * rb]
                h16_ref[bb, hh] = rows.astype(BF16)
                if hh < H_A:
                    tok_ref[bb, :, hh, :] = rows
                elif hh < H_A + H_B:
                    b_ref[bb, hh - H_A] = rows
                elif hh >= H_A + H_B:
                    c_ref[bb, hh - H_A - H_B] = rows

    for jj in range(3 * chunks_per_part):
        acc = _dot(xn_ref[...], w_vmem[:, jj * PROJ_TN:(jj + 1) * PROJ_TN])
        part = jj // chunks_per_part
        if part == 0:
            sc = SCALE if (jj + 1) * heads_per_chunk <= H_A else SCALE * LOG2E
            q_ref[:, jj * PROJ_TN:(jj + 1) * PROJ_TN] = (acc * sc).astype(BF16)
        elif part == 1:
            write_kv(jj, acc, ka_ref, kb_ref, kc_ref, k16_ref)
        else:
            write_kv(jj, acc, va_ref, vb_ref, vc_ref, v16_ref)


_KV_NAMES = ["ka", "va", "kb", "vb", "kc", "vc"]


def _inproj(x2d, row_block_off, nb_batch, seq, g, w_bf, wf_bf, bfp, layer, depth, prev):
    assert H_A * DH == PROJ_TN
    d = x2d.shape[1]
    tm = INPROJ_TM
    n_rows = nb_batch * seq
    nb = n_rows // tm
    rb = min(seq, tm)
    nbb = tm // rb
    spb = seq // rb

    def hm_map(i):
        return (layer, i // spb, 0, i % spb, 0)

    def hm16_map(i):
        return (i // spb, 0, i % spb, 0)

    out_shape = [jax.ShapeDtypeStruct((n_rows, MIX), BF16)]
    out_specs = [pl.BlockSpec((tm, MIX), lambda i: (i, 0))]
    for nm in _KV_NAMES:
        if nm in ("ka", "va"):
            out_shape.append(jax.ShapeDtypeStruct((depth, nb_batch, seq, H_A, DH), F32))
            out_specs.append(pl.BlockSpec((None, nbb, rb, H_A, DH),
                                          lambda i: (layer, i // spb, i % spb, 0, 0)))
        else:
            hn = H_B if nm[1] == "b" else H_C
            out_shape.append(jax.ShapeDtypeStruct((depth, nb_batch, hn, seq, DH), F32))
            out_specs.append(pl.BlockSpec((None, nbb, hn, rb, DH), hm_map))
    for _ in range(2):
        out_shape.append(jax.ShapeDtypeStruct((nb_batch, N_HEADS, seq, DH), BF16))
        out_specs.append(pl.BlockSpec((nbb, N_HEADS, rb, DH), hm16_map))
    out_shape.append(jax.ShapeDtypeStruct((n_rows, LANES), F32))
    out_specs.append(pl.BlockSpec((tm, LANES), lambda i: (i, 0)))
    n_out = len(out_shape)

    in_specs = [
        pl.BlockSpec((tm, d), lambda i: (i + row_block_off, 0)),
        pl.BlockSpec((1, d), lambda i: (0, 0)),
        pl.BlockSpec(memory_space=pl.ANY),
        pl.BlockSpec((d, LANES), lambda i: (0, 0)),
        pl.BlockSpec((1, LANES), lambda i: (0, 0)),
    ]
    args = [x2d, g, w_bf, wf_bf, bfp]
    n_in = len(args)
    aliases = {}
    if prev is not None:
        for t, nm in enumerate(_KV_NAMES):
            in_specs.append(pl.BlockSpec(memory_space=pl.ANY))
            args.append(prev[nm])
            aliases[n_in + t] = 1 + t

    def body(*refs):
        _inproj_kernel(*refs[:n_in], *refs[len(args):len(args) + n_out], *refs[len(args) + n_out:], rb=rb)

    res = pl.pallas_call(
        body,
        grid=(nb,),
        in_specs=in_specs,
        out_specs=out_specs,
        out_shape=out_shape,
        scratch_shapes=[pltpu.VMEM(w_bf.shape, BF16), pltpu.VMEM((tm, d), BF16),
                        pltpu.SemaphoreType.DMA(())],
        input_output_aliases=aliases,
        compiler_params=pltpu.CompilerParams(
            dimension_semantics=("arbitrary",), vmem_limit_bytes=VMEM_LIMIT),
    )(*args)
    stacks = dict(zip(_KV_NAMES, res[1:7]))
    return res[0], stacks, res[7], res[8], res[9]


def _cumsum_kernel(x_ref, f_ref, carry_ref, *, tc):
    @pl.when(pl.program_id(1) == 0)
    def _():
        carry_ref[...] = jnp.zeros_like(carry_ref)

    x = x_ref[...]
    row = lax.broadcasted_iota(jnp.int32, (tc, tc), 0)
    col = lax.broadcasted_iota(jnp.int32, (tc, tc), 1)
    lower = jnp.where(row >= col, 1.0, 0.0).astype(BF16)
    x1, x2, x3 = _split3(x)
    cs = _dot(lower, x3) + _dot(lower, x2) + _dot(lower, x1) + carry_ref[...]
    f_ref[...] = cs
    carry_ref[...] = cs[tc - 1:tc, :]


def _cumsum_seq(x, tc):
    b, t, _ = x.shape
    return pl.pallas_call(
        functools.partial(_cumsum_kernel, tc=tc),
        grid=(b, t // tc),
        in_specs=[pl.BlockSpec((None, tc, LANES), lambda i, j: (i, j, 0))],
        out_specs=pl.BlockSpec((None, tc, LANES), lambda i, j: (i, j, 0)),
        out_shape=jax.ShapeDtypeStruct((b, t, LANES), F32),
        scratch_shapes=[pltpu.VMEM((1, LANES), F32)],
        compiler_params=pltpu.CompilerParams(
            dimension_semantics=("parallel", "arbitrary")),
    )(x)


def _kv_spec(arr, layer, head_off):
    if arr.ndim == 5:
        return pl.BlockSpec((None, None, None, arr.shape[3], DH),
                            lambda bi, h: (layer, bi, head_off + h, 0, 0))
    return pl.BlockSpec((None, None, arr.shape[2], DH), lambda bi, h: (layer, bi, 0, 0))


def _attn_call(body, q3, q_head_off, kv, extra, n_heads, scratch=()):
    b, tq, _ = q3.shape
    in_specs = [pl.BlockSpec((None, tq, DH), lambda bi, h: (bi, 0, q_head_off + h))]
    args = [q3]
    for arr, layer, head_off in kv:
        in_specs.append(_kv_spec(arr, layer, head_off))
        args.append(arr)
    for arr, spec in extra:
        in_specs.append(spec)
        args.append(arr)
    return pl.pallas_call(
        body,
        grid=(b, n_heads),
        in_specs=in_specs,
        out_specs=pl.BlockSpec((None, tq, DH), lambda bi, h: (bi, 0, h)),
        out_shape=jax.ShapeDtypeStruct((b, tq, n_heads * DH), BF16),
        scratch_shapes=list(scratch),
        compiler_params=pltpu.CompilerParams(
            dimension_semantics=("parallel", "parallel"), vmem_limit_bytes=VMEM_LIMIT),
    )(*args)


def _attn_a_kernel(*refs, t, tp, has_past, interleaved):
    if has_past:
        q_ref, k_ref, v_ref, kp_ref, vp_ref, o_ref = refs
    else:
        q_ref, k_ref, v_ref, o_ref = refs
    n_q = q_ref.shape[0] // t

    def later_matrix(n):
        row = lax.broadcasted_iota(jnp.int32, (n, n), 0)
        col = lax.broadcasted_iota(jnp.int32, (n, n), 1)
        return jnp.where(row > col, 1.0, 0.0).astype(BF16)

    def cond(carry):
        j, c, _ = carry
        return jnp.logical_and(j >= 0, jnp.max(c) > SB_CUTOFF)

    def make_body(q, kr, vr, tb, diag):
        later = later_matrix(tb)
        row = lax.broadcasted_iota(jnp.int32, (t, tb), 0)
        col = lax.broadcasted_iota(jnp.int32, (t, tb), 1)

        def body(carry):
            j, c, acc = carry
            start = pl.multiple_of(j * tb, tb)
            if interleaved:
                keys = pl.ds(start * H_A + pl.program_id(1), tb, stride=H_A)
            else:
                keys = pl.ds(start, tb)
            kb = kr[keys, :].astype(BF16)
            vb = vr[keys, :].astype(BF16)
            z = _nt_dot(q, kb)
            sp = jnp.maximum(z, 0.0) + jnp.log1p(jnp.exp(-jnp.abs(z)))
            if diag is None:
                lm = -sp
            else:
                mask = col < row + jnp.where(j < diag, tb, 0)
                lm = jnp.where(mask, -sp, 0.0)
            hi = lm.astype(BF16)
            lo = (lm - hi.astype(F32)).astype(BF16)
            suffix = _dot(lo, later) + _dot(hi, later)
            w = jnp.exp(z - sp + suffix + c)
            if diag is not None:
                w = jnp.where(mask, w, 0.0)
            acc = acc + _dot(w.astype(BF16), vb)
            c = c + jnp.sum(lm, axis=1, keepdims=True)
            return j - 1, c, acc

        return body

    def q_block(i, carry):
        rows = pl.ds(_aligned(i * t, t), t)
        q = q_ref[rows, :]
        state = (jnp.asarray(i, jnp.int32), jnp.zeros((t, 1), F32), jnp.zeros((t, DH), F32))
        _, c, acc = lax.while_loop(cond, make_body(q, k_ref, v_ref, t, i), state)
        if has_past:
            n_past = kp_ref.shape[0] // (tp * (H_A if interleaved else 1))
            state = (jnp.asarray(n_past - 1, jnp.int32), c, acc)
            _, c, acc = lax.while_loop(cond, make_body(q, kp_ref, vp_ref, tp, None), state)
        o_ref[rows, :] = acc.astype(o_ref.dtype)
        return carry

    if n_q == 1:
        q_block(0, 0)
    else:
        lax.fori_loop(0, n_q, q_block, 0)


def _attn_a(q3, own, past, t, tp):
    kv = list(own) + (list(past) if past else [])
    interleaved = own[0][0].ndim == 4
    return _attn_call(functools.partial(_attn_a_kernel, t=t, tp=tp, has_past=bool(past),
                                        interleaved=interleaved),
                      q3, 0, kv, [], H_A)


def _attn_b_kernel(*refs, t, tp, has_past):
    if has_past:
        q_ref, k_ref, v_ref, kp_ref, vp_ref, fk_ref, o_ref = refs
        t_past = kp_ref.shape[0]
    else:
        q_ref, k_ref, v_ref, fk_ref, o_ref = refs
        t_past = 0
    n_q = q_ref.shape[0] // t
    rt = BF16_ROWS

    def update(s, fk, vb, carry, q_minus_k):
        m, l, acc = carry
        tb = s.shape[1]
        if q_minus_k is not None:
            row = lax.broadcasted_iota(jnp.int32, (rt, tb), 0)
            col = lax.broadcasted_iota(jnp.int32, (rt, tb), 1)
        sb = s - fk
        if q_minus_k is not None:
            row_b = lax.broadcasted_iota(jnp.int32, (t, tb), 0)
            col_b = lax.broadcasted_iota(jnp.int32, (t, tb), 1)
            sb = jnp.where(col_b <= row_b + q_minus_k, sb, NEG_INF)
        m_new = jnp.maximum(m, jnp.max(sb, axis=1, keepdims=True))
        alpha = jnp.exp2(m - m_new)
        ps, parts = [], []
        for r in range(t // rt):
            s_t = s[r * rt:(r + 1) * rt] - fk
            if q_minus_k is not None:
                s_t = jnp.where(col <= row + (r * rt + q_minus_k), s_t, NEG_INF)
            p = jnp.exp2(s_t - m_new[r * rt:(r + 1) * rt])
            if tb % LANES == 0:
                parts.append(functools.reduce(
                    jnp.add, [p[:, c * LANES:(c + 1) * LANES] for c in range(tb // LANES)]))
            else:
                lane = lax.broadcasted_iota(jnp.int32, (rt, LANES), 1)
                parts.append(jnp.where(lane == 0, jnp.sum(p, axis=1, keepdims=True), 0.0))
            ps.append(p.astype(BF16))
        l = alpha * l + jnp.concatenate(parts, axis=0)
        acc = alpha * acc + _dot(jnp.concatenate(ps, axis=0), vb)
        return m_new, l, acc

    def q_block(i, carry_unused):
        rows = pl.ds(_aligned(i * t, t), t)
        q = q_ref[rows, :]
        carry = (jnp.full((t, 1), NEG_INF, F32), jnp.zeros((t, LANES), F32), jnp.zeros((t, DH), F32))

        if has_past:
            def past_body(j, c):
                start = pl.multiple_of(j * tp, tp)
                kb = kp_ref[pl.ds(start, tp), :].astype(BF16)
                vb = vp_ref[pl.ds(start, tp), :].astype(BF16)
                return update(_nt_dot(q, kb), fk_ref[:, pl.ds(start, tp)], vb, c, None)

            carry = lax.fori_loop(0, t_past // tp, past_body, carry)

        def own_start(j):
            return _aligned(j * t, t)

        def scores(j):
            return _nt_dot(q, k_ref[pl.ds(own_start(j), t), :].astype(BF16))

        def own_update(s, j, c, diag):
            vb = v_ref[pl.ds(own_start(j), t), :].astype(BF16)
            fk = fk_ref[:, pl.ds(t_past + own_start(j), t)]
            return update(s, fk, vb, c, 0 if diag else None)

        def body(j, c):
            s_cur, inner = c
            s_next = scores(j + 1)
            return s_next, own_update(s_cur, j, inner, False)

        s = scores(0)
        if n_q > 1:
            s, carry = lax.fori_loop(0, i, body, (s, carry))
        _, l, acc = own_update(s, i, carry, True)
        o_ref[rows, :] = (acc / jnp.sum(l, axis=1, keepdims=True)).astype(o_ref.dtype)
        return carry_unused

    if n_q == 1:
        q_block(0, 0)
    else:
        lax.fori_loop(0, n_q, q_block, 0)


def _attn_b_blocks_kernel(q_ref, k_ref, v_ref, fk_ref, o_ref,
                          s0_ref, s1_ref, p_ref, m_ref, l_ref, acc_ref, *, t):
    n_q = q_ref.shape[0] // t
    rt = BF16_ROWS

    def q_block(i, carry):
        rows = pl.ds(pl.multiple_of(i * t, t), t)
        q = q_ref[rows, :]
        m_ref[...] = jnp.full(m_ref.shape, NEG_INF, F32)
        l_ref[...] = jnp.zeros(l_ref.shape, F32)
        acc_ref[...] = jnp.zeros(acc_ref.shape, F32)

        def keys(j):
            return pl.ds(pl.multiple_of(j * t, t), t)

        def put_scores(dst, j):
            dst[...] = _nt_dot(q, k_ref[keys(j), :]) - fk_ref[:, keys(j)]

        def consume(src, j, diag):
            sb = src[...]
            if diag:
                row_b = lax.broadcasted_iota(jnp.int32, (t, t), 0)
                col_b = lax.broadcasted_iota(jnp.int32, (t, t), 1)
                sb = jnp.where(col_b <= row_b, sb, NEG_INF)
                row = lax.broadcasted_iota(jnp.int32, (rt, t), 0)
                col = lax.broadcasted_iota(jnp.int32, (rt, t), 1)
            m_old = m_ref[...]
            m_new = jnp.maximum(m_old, jnp.max(sb, axis=1, keepdims=True))
            alpha = jnp.exp2(m_old - m_new)
            m_ref[...] = m_new
            for r in range(t // rt):
                sl = slice(r * rt, (r + 1) * rt)
                s_t = src[sl, :]
                if diag:
                    s_t = jnp.where(col <= row + r * rt, s_t, NEG_INF)
                p = jnp.exp2(s_t - m_new[sl])
                part = functools.reduce(jnp.add, [p[:, c * LANES:(c + 1) * LANES] for c in range(t // LANES)])
                l_ref[sl, :] = alpha[sl] * l_ref[sl, :] + part
                p_ref[sl, :] = p.astype(BF16)
            acc_ref[...] = alpha * acc_ref[...] + _dot(p_ref[...], v_ref[keys(j), :])

        put_scores(s0_ref, 0)

        def pair(pp, c):
            j = 2 * pp
            put_scores(s1_ref, j + 1)
            consume(s0_ref, j, False)
            put_scores(s0_ref, j + 2)
            consume(s1_ref, j + 1, False)
            return c

        lax.fori_loop(0, i // 2, pair, 0)

        @pl.when(i % 2 == 1)
        def _():
            put_scores(s1_ref, i)
            consume(s0_ref, i - 1, False)
            consume(s1_ref, i, True)

        @pl.when(i % 2 == 0)
        def _():
            consume(s0_ref, i, True)

        o_ref[rows, :] = (acc_ref[...] / jnp.sum(l_ref[...], axis=1, keepdims=True)).astype(o_ref.dtype)
        return carry

    lax.fori_loop(0, n_q, q_block, 0)


def _attn_b(q3, own, past, frow, t, tp):
    kv = list(own) + (list(past) if past else [])
    fspec = pl.BlockSpec((None, None, 1, frow.shape[3]), lambda bi, h: (bi, h, 0, 0))
    if not past and t % LANES == 0 and own[0][0].dtype == BF16:
        scratch = [pltpu.VMEM((t, t), F32), pltpu.VMEM((t, t), F32), pltpu.VMEM((t, t), BF16),
                   pltpu.VMEM((t, 1), F32), pltpu.VMEM((t, LANES), F32), pltpu.VMEM((t, DH), F32)]
        return _attn_call(functools.partial(_attn_b_blocks_kernel, t=t),
                          q3, H_A, kv, [(frow, fspec)], H_B, scratch)
    return _attn_call(functools.partial(_attn_b_kernel, t=t, tp=tp, has_past=bool(past)),
                      q3, H_A, kv, [(frow, fspec)], H_B)


def _band_table(rel_bias, r):
    w = C_PAST + r
    lh = w + r - 1
    dist = C_PAST + r - 1 - jnp.arange(lh)
    hvec = rel_bias.astype(F32)[:, jnp.clip(dist, -REL_CLIP, REL_CLIP) + REL_CLIP]
    q = lh + 1
    hq = jnp.pad(hvec, ((0, 0), (0, 1)))
    skew = jnp.tile(hq, (1, r + 1))[:, :r * (q + 1)].reshape(H_C, r, q + 1)[:, :, :w]
    bias = skew[:, ::-1, :]
    qi = jnp.arange(r)[:, None] // CHUNK
    kj = jnp.arange(w)[None, :] // CHUNK
    ok = (kj >= qi) & (kj <= qi + C_PREV)
    return jnp.where(ok[None], bias * LOG2E, NEG_INF)


def _attn_c_kernel(*refs, r, has_past):
    if has_past:
        q_ref, k_ref, v_ref, kp_ref, vp_ref, tab_ref, o_ref = refs
    else:
        q_ref, k_ref, v_ref, tab_ref, o_ref = refs
    w = C_PAST + r

    def finish(pieces, rows):
        m = functools.reduce(jnp.maximum, [jnp.max(s, axis=1, keepdims=True) for s, _ in pieces])
        ps = [jnp.exp2(s - m) for s, _ in pieces]
        l = functools.reduce(jnp.add, [jnp.sum(p, axis=1, keepdims=True) for p in ps])
        o = functools.reduce(jnp.add, [_dot(p.astype(BF16), v) for p, (_, v) in zip(ps, pieces)])
        o_ref[rows, :] = (o / l).astype(o_ref.dtype)

    if has_past:
        q = q_ref[...]
        s_p = _nt_dot(q, kp_ref[...].astype(BF16)) + tab_ref[:, 0:C_PAST]
        s_o = _nt_dot(q, k_ref[...].astype(BF16)) + tab_ref[:, C_PAST:w]
        finish([(s_p, vp_ref[...].astype(BF16)), (s_o, v_ref[...].astype(BF16))], pl.ds(0, r))
        return

    def tile(i, start, width, tcol0):
        rows = pl.ds(_aligned(i * r, r), r)
        q = q_ref[rows, :]
        kb = k_ref[pl.ds(start, width), :].astype(BF16)
        vb = v_ref[pl.ds(start, width), :].astype(BF16)
        finish([(_nt_dot(q, kb) + tab_ref[:, tcol0:tcol0 + width], vb)], rows)

    n_tiles = q_ref.shape[0] // r
    n_short = min(C_PAST // r, n_tiles)
    for t in range(n_short):
        tile(t, 0, (t + 1) * r, C_PAST - t * r)

    def full_tile(i, carry):
        tile(i, pl.multiple_of(i * r - C_PAST, SUBLANES), w, 0)
        return carry

    lax.fori_loop(n_short, n_tiles, full_tile, 0, unroll=2)


def _attn_c(q3, own, past, table, r):
    assert C_PAST % r == 0
    kv = list(own) + (list(past) if past else [])
    tspec = pl.BlockSpec((None, r, C_PAST + r), lambda bi, h: (h, 0, 0))
    return _attn_call(functools.partial(_attn_c_kernel, r=r, has_past=bool(past)),
                      q3, H_A + H_B, kv, [(table, tspec)], H_C)


def _outproj_kernel(x_ref, oa_ref, ob_ref, oc_ref, wo_ref, g_ref, wr_ref, br_ref,
                    h_ref, hn_ref, route_ref):
    a0, a1 = H_A * DH, (H_A + H_B) * DH
    acc = _dot(oa_ref[...], wo_ref[0:a0, :])
    acc = acc + _dot(ob_ref[...], wo_ref[a0:a1, :])
    acc = acc + _dot(oc_ref[...], wo_ref[a1:MIX, :])
    h = x_ref[...] + acc
    h_ref[...] = h
    ms = jnp.mean(h * h, axis=-1, keepdims=True)
    hn = h * lax.rsqrt(ms + EPS) * g_ref[...]
    hn_ref[...] = hn

    h1, h2, _ = _split3(hn)
    w1 = wr_ref[0]
    w2 = wr_ref[1]
    lg = _dot(h2, w1) + _dot(h1, w2) + _dot(h1, w1) + br_ref[...]

    lane = lax.broadcasted_iota(jnp.int32, lg.shape, 1)
    lane_f = lane.astype(F32)
    is_g = lane < N_GROUPS
    lgm = jnp.where(is_g, lg, NEG_INF)
    gmax = jnp.max(lgm, axis=1, keepdims=True)
    gidx = jnp.min(jnp.where(lgm == gmax, lane_f, float(LANES)), axis=1, keepdims=True)
    pg = 1.0 / jnp.sum(jnp.where(is_g, jnp.exp(lgm - gmax), 0.0), axis=1, keepdims=True)
    grp = jnp.floor((lane_f - N_GROUPS) * (1.0 / EXP_PER_GROUP))
    in_grp = (lane >= N_GROUPS) & (lane < N_GROUPS + N_EXPERTS) & (grp == gidx)
    le = jnp.where(in_grp, lg, NEG_INF)
    t1 = jnp.max(le, axis=1, keepdims=True)
    i1 = jnp.min(jnp.where(le == t1, lane_f, float(LANES)), axis=1, keepdims=True)
    le2 = jnp.where(lane_f == i1, NEG_INF, le)
    t2 = jnp.max(le2, axis=1, keepdims=True)
    i2 = jnp.min(jnp.where(le2 == t2, lane_f, float(LANES)), axis=1, keepdims=True)
    e = jnp.exp(t2 - t1)
    g1 = pg / (1.0 + e)
    g2 = pg * e / (1.0 + e)
    route = jnp.where(lane == 0, i1 - N_GROUPS,
                      jnp.where(lane == 1, i2 - N_GROUPS,
                                jnp.where(lane == 2, g1, jnp.where(lane == 3, g2, 0.0))))
    route_ref[...] = route


def _outproj(x2d, x_block_off, oa, ob, oc, wo_bf, g, wr2, br, n_total, out_block_off, prev):
    n_rows, d = oa.shape[0], x2d.shape[1]
    tm = PROJ_TM
    nb = n_rows // tm
    in_specs = [
        pl.BlockSpec((tm, d), lambda i: (i + x_block_off, 0)),
        pl.BlockSpec((tm, H_A * DH), lambda i: (i, 0)),
        pl.BlockSpec((tm, H_B * DH), lambda i: (i, 0)),
        pl.BlockSpec((tm, H_C * DH), lambda i: (i, 0)),
        pl.BlockSpec((MIX, d), lambda i: (0, 0)),
        pl.BlockSpec((1, d), lambda i: (0, 0)),
        pl.BlockSpec((2, d, LANES), lambda i: (0, 0, 0)),
        pl.BlockSpec((1, LANES), lambda i: (0, 0)),
    ]
    args = [x2d, oa, ob, oc, wo_bf, g, wr2, br]
    aliases = {}
    if prev is not None:
        for t, arr in enumerate(prev):
            in_specs.append(pl.BlockSpec(memory_space=pl.ANY))
            args.append(arr)
            aliases[8 + t] = t

    def body(*refs):
        _outproj_kernel(*refs[:8], *refs[len(args):len(args) + 3])

    return pl.pallas_call(
        body,
        grid=(nb,),
        in_specs=in_specs,
        out_specs=[pl.BlockSpec((tm, d), lambda i: (i + out_block_off, 0)),
                   pl.BlockSpec((tm, d), lambda i: (i + out_block_off, 0)),
                   pl.BlockSpec((tm, LANES), lambda i: (i + out_block_off, 0))],
        out_shape=[jax.ShapeDtypeStruct((n_total, d), F32),
                   jax.ShapeDtypeStruct((n_total, d), F32),
                   jax.ShapeDtypeStruct((n_total, LANES), F32)],
        input_output_aliases=aliases,
        compiler_params=pltpu.CompilerParams(
            dimension_semantics=("parallel",), vmem_limit_bytes=VMEM_LIMIT),
    )(*args)


def _scatter_rows_kernel(cnt_ref, pst_ref, dest_ref, hn_hbm, xs_hbm, sem, fill_sem):
    i = pl.program_id(0)
    n_pairs = dest_ref.shape[2]
    tm = n_pairs // 2

    def fill_copy(row):
        return pltpu.make_async_copy(hn_hbm.at[pl.ds(0, 1), :], xs_hbm.at[pl.ds(row, 1), :], fill_sem)

    @pl.when(i == 0)
    def _():
        def per_expert(e, total):
            n_pad = (MOE_BLK - cnt_ref[e] % MOE_BLK) % MOE_BLK
            base = pst_ref[e] + cnt_ref[e]

            def one(r, c):
                fill_copy(base + r).start()
                return c

            lax.fori_loop(0, n_pad, one, 0)
            return total + n_pad

        total = lax.fori_loop(0, N_EXPERTS, per_expert, 0)

        def drain(r, c):
            fill_copy(0).wait()
            return c

        lax.fori_loop(0, total, drain, 0)

    def group(g, carry):
        base = pl.multiple_of(g * SUBLANES, SUBLANES)
        for u in range(SUBLANES):
            tok = i * tm + (base + u) // 2
            pltpu.make_async_copy(hn_hbm.at[pl.ds(tok, 1), :],
                                  xs_hbm.at[pl.ds(dest_ref[0, 0, base + u], 1), :], sem).start()
        return carry

    lax.fori_loop(0, n_pairs // SUBLANES, group, 0)
    for _ in range(2):
        pltpu.make_async_copy(hn_hbm.at[pl.ds(0, tm), :], xs_hbm.at[pl.ds(0, tm), :], sem).wait()


def _scatter_rows(hn, dest, counts, pstart, n_blocks):
    n, d = hn.shape
    tm = COMBINE_TM
    grid_spec = pltpu.PrefetchScalarGridSpec(
        num_scalar_prefetch=2,
        grid=(n // tm,),
        in_specs=[pl.BlockSpec((1, 1, 2 * tm), lambda i, c, p: (i, 0, 0), memory_space=pltpu.SMEM),
                  pl.BlockSpec(memory_space=pl.ANY)],
        out_specs=pl.BlockSpec(memory_space=pl.ANY),
        scratch_shapes=[pltpu.SemaphoreType.DMA(()), pltpu.SemaphoreType.DMA(())],
    )
    return pl.pallas_call(
        _scatter_rows_kernel,
        grid_spec=grid_spec,
        out_shape=jax.ShapeDtypeStruct((n_blocks * MOE_BLK, d), F32),
        compiler_params=pltpu.CompilerParams(
            dimension_semantics=("arbitrary",), has_side_effects=True),
    )(counts, pstart, dest.reshape(n // tm, 1, 2 * tm), hn)


def _moe_kernel(be_ref, nused_ref, xs_ref, wg_ref, wu_ref, wd_ref, yb_ref, wg16, wu16, wd16):
    i = pl.program_id(0)

    @pl.when(jnp.logical_or(i == 0, be_ref[i] != be_ref[jnp.maximum(i - 1, 0)]))
    def _():
        wg16[...] = wg_ref[...].astype(BF16)
        wu16[...] = wu_ref[...].astype(BF16)
        wd16[...] = wd_ref[...].astype(BF16)

    @pl.when(i < nused_ref[0])
    def _():
        xb = xs_ref[...].astype(BF16)
        gate = _dot(xb, wg16[...])
        up = _dot(xb, wu16[...])
        act = (gate * jax.nn.sigmoid(gate) * up).astype(BF16)
        yb_ref[...] = _dot(act, wd16[...])

    @pl.when(i >= nused_ref[0])
    def _():
        yb_ref[...] = jnp.zeros_like(yb_ref)


def _moe(xs, block_e, nused, wg, wu, wd, layer, n_blocks):
    d = xs.shape[1]
    de = wg.shape[3]

    def row_map(i, be, nu):
        return (jnp.minimum(i, nu[0] - 1), 0)

    grid_spec = pltpu.PrefetchScalarGridSpec(
        num_scalar_prefetch=2,
        grid=(n_blocks,),
        in_specs=[
            pl.BlockSpec((MOE_BLK, d), row_map),
            pl.BlockSpec((None, None, d, de), lambda i, be, nu: (layer, be[i], 0, 0)),
            pl.BlockSpec((None, None, d, de), lambda i, be, nu: (layer, be[i], 0, 0)),
            pl.BlockSpec((None, None, de, d), lambda i, be, nu: (layer, be[i], 0, 0)),
        ],
        out_specs=pl.BlockSpec((MOE_BLK, d), lambda i, be, nu: (i, 0)),
        scratch_shapes=[pltpu.VMEM((d, de), BF16), pltpu.VMEM((d, de), BF16), pltpu.VMEM((de, d), BF16)],
    )
    return pl.pallas_call(
        _moe_kernel,
        grid_spec=grid_spec,
        out_shape=jax.ShapeDtypeStruct((n_blocks * MOE_BLK, d), F32),
        compiler_params=pltpu.CompilerParams(
            dimension_semantics=("arbitrary",), vmem_limit_bytes=VMEM_LIMIT),
    )(block_e, nused, xs, wg, wu, wd)


def _dispatch(eid, n_blocks):
    e = eid.reshape(-1)
    onehot = (e[:, None] == jnp.arange(N_EXPERTS, dtype=jnp.int32)[None, :]).astype(jnp.int32)
    cs = jnp.cumsum(onehot, axis=0)
    rank = jnp.sum(cs * onehot, axis=1) - 1
    counts = cs[-1]
    padded = (counts + MOE_BLK - 1) // MOE_BLK * MOE_BLK
    pend = jnp.cumsum(padded)
    pstart = pend - padded
    dest = (jnp.sum(onehot * pstart[None, :], axis=1) + rank).astype(jnp.int32)
    blk_start = jnp.arange(n_blocks, dtype=jnp.int32) * MOE_BLK
    block_e = jnp.minimum(jnp.sum((pend[None, :] <= blk_start[:, None]).astype(jnp.int32), axis=1),
                          N_EXPERTS - 1).astype(jnp.int32)
    nused = (pend[-1:] // MOE_BLK).astype(jnp.int32)
    return dest, counts.astype(jnp.int32), pstart.astype(jnp.int32), block_e, nused


def _combine_kernel(pos_ref, h_ref, route_ref, yb_hbm, g_ref, o_ref, buf, sem, *, final):
    tm = h_ref.shape[0]

    def group(gi, carry):
        base = pl.multiple_of(gi * SUBLANES, SUBLANES)
        for u in range(SUBLANES):
            for s in range(2):
                pltpu.make_async_copy(yb_hbm.at[pl.ds(pos_ref[0, 0, 2 * (base + u) + s], 1), :],
                                      buf.at[s, pl.ds(base + u, 1), :], sem).start()
        return carry

    lax.fori_loop(0, tm // SUBLANES, group, 0)
    pltpu.make_async_copy(yb_hbm.at[pl.ds(0, tm), :], buf.at[0], sem).wait()
    pltpu.make_async_copy(yb_hbm.at[pl.ds(0, tm), :], buf.at[1], sem).wait()
    route = route_ref[...]
    g1 = _lane_pick(route, 2)
    g2 = _lane_pick(route, 3)
    y = h_ref[...] + (g1 * buf[0] + g2 * buf[1])
    if final:
        ms = jnp.mean(y * y, axis=-1, keepdims=True)
        y = y * lax.rsqrt(ms + EPS) * g_ref[...]
    o_ref[...] = y


def _combine(pos3, h, route, yb, g, block_off, n_rows, final):
    d = h.shape[1]
    tm = COMBINE_TM
    return pl.pallas_call(
        functools.partial(_combine_kernel, final=final),
        grid=(n_rows // tm,),
        in_specs=[
            pl.BlockSpec((1, 1, 2 * tm), lambda i: (i + block_off, 0, 0), memory_space=pltpu.SMEM),
            pl.BlockSpec((tm, d), lambda i: (i + block_off, 0)),
            pl.BlockSpec((tm, LANES), lambda i: (i + block_off, 0)),
            pl.BlockSpec(memory_space=pl.ANY),
            pl.BlockSpec((1, d), lambda i: (0, 0)),
        ],
        out_specs=pl.BlockSpec((tm, d), lambda i: (i, 0)),
        out_shape=jax.ShapeDtypeStruct((n_rows, d), F32),
        scratch_shapes=[pltpu.VMEM((2, tm, d), F32), pltpu.SemaphoreType.DMA(())],
        compiler_params=pltpu.CompilerParams(
            dimension_semantics=("arbitrary",), vmem_limit_bytes=VMEM_LIMIT),
    )(pos3, h, route, yb, g)


def _pad_lanes(a):
    return jnp.pad(a, [(0, 0)] * (a.ndim - 1) + [(0, LANES - a.shape[-1])])


def kernel(x_prompt, x_sample, cache_a_k, cache_a_v, cache_b_k, cache_b_v, cache_b_logf,
           cache_c_k, cache_c_v, norm1_g, norm2_g, w_in, b_f, rel_bias, w_o,
           w_rg, b_rg, w_re, b_re, w_gate, w_up, w_down, final_g):
    bp, sp, d = x_prompt.shape
    bs, ss, _ = x_sample.shape
    depth = w_in.shape[0]
    past = cache_a_k.shape[2]
    keep = cache_c_k.shape[2]
    n_p, n_s = bp * sp, bs * ss
    n_tot = n_p + n_s
    assert d == MIX and n_p % PROJ_TM == 0 and n_s % PROJ_TM == 0
    assert ss == CHUNK and keep == C_PAST

    t_a = min(256, sp)
    t_b = min(512, sp)
    r_p = min(256, sp)
    tp_a = min(256, past)
    tp_b = min(512, past)
    assert past % tp_a == 0 and past % tp_b == 0
    n_blocks = (2 * n_tot) // MOE_BLK + N_EXPERTS

    cb_k, cb_v = jnp.swapaxes(cache_b_k, 2, 3), jnp.swapaxes(cache_b_v, 2, 3)
    cc_k, cc_v = jnp.swapaxes(cache_c_k, 2, 3), jnp.swapaxes(cache_c_v, 2, 3)
    ca_k = cache_a_k.reshape(depth, bs, past * H_A, DH)
    ca_v = cache_a_v.reshape(depth, bs, past * H_A, DH)

    xs_p = (x_prompt.reshape(n_p, d), 0)
    xs_s = (x_sample.reshape(n_s, d), 0)
    stacks_p = stacks_s = None
    logf_p, logf_s = [], []
    y_prompt = y_sample = None

    for l in range(depth):
        w_bf = w_in[l, :, :3 * MIX].astype(BF16)
        wf_bf = _pad_lanes(w_in[l, :, 3 * MIX:]).astype(BF16)
        bfp = _pad_lanes(b_f[l][None, :].astype(F32))
        g1 = norm1_g[l][None, :].astype(F32)
        q_p, stacks_p, k16_p, v16_p, lf_p = _inproj(xs_p[0], xs_p[1] * (PROJ_TM // INPROJ_TM), bp, sp, g1,
                                                    w_bf, wf_bf, bfp, l, depth, stacks_p)
        q_s, stacks_s, _, _, lf_s = _inproj(xs_s[0], xs_s[1] * (PROJ_TM // INPROJ_TM), bs, ss, g1,
                                            w_bf, wf_bf, bfp, l, depth, stacks_s)
        logf_p.append(lf_p[:, :H_B].reshape(bp, sp, H_B))
        logf_s.append(lf_s[:, :H_B].reshape(bs, ss, H_B))
        q_p3 = q_p.reshape(bp, sp, MIX)
        q_s3 = q_s.reshape(bs, ss, MIX)
        k16_p, v16_p = k16_p[None], v16_p[None]
        ka_s = stacks_s["ka"].reshape(depth, bs, ss * H_A, DH)
        va_s = stacks_s["va"].reshape(depth, bs, ss * H_A, DH)

        oa_p = _attn_a(q_p3, [(k16_p, 0, 0), (v16_p, 0, 0)], None, t_a, t_a)
        oa_s = _attn_a(q_s3, [(ka_s, l, 0), (va_s, l, 0)], [(ca_k, l, 0), (ca_v, l, 0)], ss, tp_a)

        def row_layout(fcol):
            return (jnp.swapaxes(fcol[:, :, :H_B], 1, 2) * LOG2E)[:, :, None, :]

        fcol_p = _cumsum_seq(lf_p.reshape(bp, sp, LANES), min(512, sp))
        lf_cat = jnp.concatenate([_pad_lanes(cache_b_logf[l].astype(F32)),
                                  lf_s.reshape(bs, ss, LANES)], axis=1)
        t_cat = past + ss
        tc_s = t_cat // 3 if (t_cat % 24 == 0) else t_cat
        fcol_s = _cumsum_seq(lf_cat, tc_s)
        ob_p = _attn_b(q_p3, [(k16_p, 0, H_A), (v16_p, 0, H_A)], None, row_layout(fcol_p), t_b, t_b)
        ob_s = _attn_b(q_s3, [(stacks_s["kb"], l, 0), (stacks_s["vb"], l, 0)],
                       [(cb_k, l, 0), (cb_v, l, 0)], row_layout(fcol_s), ss, tp_b)

        oc_p = _attn_c(q_p3, [(k16_p, 0, H_A + H_B), (v16_p, 0, H_A + H_B)], None,
                       _band_table(rel_bias[l], r_p), r_p)
        oc_s = _attn_c(q_s3, [(stacks_s["kc"], l, 0), (stacks_s["vc"], l, 0)],
                       [(cc_k, l, 0), (cc_v, l, 0)], _band_table(rel_bias[l], ss), ss)

        wo_bf = w_o[l].astype(BF16)
        g2 = norm2_g[l][None, :].astype(F32)
        wr = _pad_lanes(jnp.concatenate([w_rg[l], w_re[l]], axis=1).astype(F32))
        wr1 = wr.astype(BF16)
        wr2 = jnp.stack([wr1, (wr - wr1.astype(F32)).astype(BF16)])
        br = _pad_lanes(jnp.concatenate([b_rg[l], b_re[l]])[None, :].astype(F32))
        shared = _outproj(xs_p[0], xs_p[1], oa_p.reshape(n_p, -1), ob_p.reshape(n_p, -1),
                          oc_p.reshape(n_p, -1), wo_bf, g2, wr2, br, n_tot, 0, None)
        h, hn, route = _outproj(xs_s[0], xs_s[1], oa_s.reshape(n_s, -1), ob_s.reshape(n_s, -1),
                                oc_s.reshape(n_s, -1), wo_bf, g2, wr2, br, n_tot,
                                n_p // PROJ_TM, shared)

        eid = route[:, :2].astype(jnp.int32)
        dest, counts, pstart, block_e, nused = _dispatch(eid, n_blocks)
        xs = _scatter_rows(hn, dest, counts, pstart, n_blocks)
        yb = _moe(xs, block_e, nused, w_gate, w_up, w_down, l, n_blocks)
        pos3 = dest.reshape(n_tot // COMBINE_TM, 1, 2 * COMBINE_TM)
        fg = final_g[None, :].astype(F32)
        if l + 1 < depth:
            y = _combine(pos3, h, route, yb, fg, 0, n_tot, False)
            xs_p = (y, 0)
            xs_s = (y, n_p // PROJ_TM)
        else:
            y_prompt = _combine(pos3, h, route, yb, fg, 0, n_p, True).reshape(bp, sp, d)
            y_sample = _combine(pos3, h, route, yb, fg, n_p // COMBINE_TM, n_s, True).reshape(bs, ss, d)

    def tok_state(stacks, nm, b, s):
        return stacks[nm]

    def hm_state(stacks, nm):
        return jnp.swapaxes(stacks[nm], 2, 3)

    keep_p = min(C_PAST, sp)
    return (y_prompt, y_sample,
            tok_state(stacks_p, "ka", bp, sp), tok_state(stacks_p, "va", bp, sp),
            hm_state(stacks_p, "kb"), hm_state(stacks_p, "vb"),
            jnp.stack(logf_p, axis=0),
            hm_state(stacks_p, "kc")[:, :, sp - keep_p:], hm_state(stacks_p, "vc")[:, :, sp - keep_p:],
            tok_state(stacks_s, "ka", bs, ss), tok_state(stacks_s, "va", bs, ss),
            hm_state(stacks_s, "kb"), hm_state(stacks_s, "vb"),
            jnp.stack(logf_s, axis=0),
            hm_state(stacks_s, "kc"), hm_state(stacks_s, "vc"))
```

```python
import functools
import math

import jax
import jax.numpy as jnp
from jax import lax
from jax.experimental import pallas as pl
from jax.experimental.pallas import tpu as pltpu

F32 = jnp.float32
BF16 = jnp.bfloat16

DH = 128
H_A, H_B, H_C = 4, 6, 6
N_HEADS = H_A + H_B + H_C
MIX = N_HEADS * DH
CHUNK = 64
C_PREV = 8
C_PAST = C_PREV * CHUNK
REL_CLIP = 128
N_GROUPS = 4
EXP_PER_GROUP = 8
N_EXPERTS = N_GROUPS * EXP_PER_GROUP
EPS = 1e-6
NEG_INF = -1e30
SCALE = DH ** -0.5
LOG2E = math.log2(math.e)

LANES = 128
SUBLANES = 8
BF16_ROWS = 16
INPROJ_TM = 256
PROJ_TM = 512
PROJ_TN = 512
MOE_BLK = 256
COMBINE_TM = 256
SB_CUTOFF = -104.0
VMEM_LIMIT = 56 * 1024 * 1024


def _nt_dot(a, b):
    return lax.dot_general(a, b, (((1,), (1,)), ((), ())), preferred_element_type=F32)


def _dot(a, b):
    return jnp.dot(a, b, preferred_element_type=F32)


def _split3(x):
    x1 = x.astype(BF16)
    r1 = x - x1.astype(F32)
    x2 = r1.astype(BF16)
    x3 = (r1 - x2.astype(F32)).astype(BF16)
    return x1, x2, x3


def _aligned(x, m):
    return x if isinstance(x, int) else pl.multiple_of(x, m)


def _lane_pick(x, idx):
    lane = lax.broadcasted_iota(jnp.int32, x.shape, 1)
    return jnp.sum(jnp.where(lane == idx, x, 0.0), axis=1, keepdims=True)


def _inproj_kernel(x_ref, g_ref, w_hbm, wf_ref, bf_ref,
                   q_ref, ka_ref, va_ref, kb_ref, vb_ref, kc_ref, vc_ref, k16_ref, v16_ref, lf_ref,
                   w_vmem, xn_ref, sem, *, rb):
    tm = x_ref.shape[0]
    nbb = tm // rb
    heads_per_chunk = PROJ_TN // DH
    chunks_per_part = MIX // PROJ_TN

    @pl.when(pl.program_id(0) == 0)
    def _():
        cp = pltpu.make_async_copy(w_hbm, w_vmem, sem)
        cp.start()
        cp.wait()

    x = x_ref[...]
    ms = jnp.mean(x * x, axis=-1, keepdims=True)
    xn_ref[...] = (x * lax.rsqrt(ms + EPS) * g_ref[...]).astype(BF16)
    lf_ref[...] = jax.nn.log_sigmoid(_dot(xn_ref[...], wf_ref[...]) + bf_ref[...])

    def write_kv(jj, acc, tok_ref, b_ref, c_ref, h16_ref):
        for u in range(heads_per_chunk):
            hh = (jj * heads_per_chunk + u) % N_HEADS
            piece = acc[:, u * DH:(u + 1) * DH]
            for bb in range(nbb):
                rows = piece[bb * rb:(bb + 1) * rb]
                h16_ref[bb, hh] = rows.astype(BF16)
                if hh < H_A:
                    tok_ref[bb, :, hh, :] = rows
                elif hh < H_A + H_B:
                    b_ref[bb, hh - H_A] = rows
                elif hh >= H_A + H_B:
                    c_ref[bb, hh - H_A - H_B] = rows

    for jj in range(3 * chunks_per_part):
        acc = _dot(xn_ref[...], w_vmem[:, jj * PROJ_TN:(jj + 1) * PROJ_TN])
        part = jj // chunks_per_part
        if part == 0:
            sc = SCALE if (jj + 1) * heads_per_chunk <= H_A else SCALE * LOG2E
            q_ref[:, jj * PROJ_TN:(jj + 1) * PROJ_TN] = (acc * sc).astype(BF16)
        elif part == 1:
            write_kv(jj, acc, ka_ref, kb_ref, kc_ref, k16_ref)
        else:
            write_kv(jj, acc, va_ref, vb_ref, vc_ref, v16_ref)


_KV_NAMES = ["ka", "va", "kb", "vb", "kc", "vc"]


def _inproj(x2d, row_block_off, nb_batch, seq, g, w_bf, wf_bf, bfp, layer, depth, prev):
    assert H_A * DH == PROJ_TN
    d = x2d.shape[1]
    tm = INPROJ_TM
    n_rows = nb_batch * seq
    nb = n_rows // tm
    rb = min(seq, tm)
    nbb = tm // rb
    spb = seq // rb

    def hm_map(i):
        return (layer, i // spb, 0, i % spb, 0)

    def hm16_map(i):
        return (i // spb, 0, i % spb, 0)

    out_shape = [jax.ShapeDtypeStruct((n_rows, MIX), BF16)]
    out_specs = [pl.BlockSpec((tm, MIX), lambda i: (i, 0))]
    for nm in _KV_NAMES:
        if nm in ("ka", "va"):
            out_shape.append(jax.ShapeDtypeStruct((depth, nb_batch, seq, H_A, DH), F32))
            out_specs.append(pl.BlockSpec((None, nbb, rb, H_A, DH),
                                          lambda i: (layer, i // spb, i % spb, 0, 0)))
        else:
            hn = H_B if nm[1] == "b" else H_C
            out_shape.append(jax.ShapeDtypeStruct((depth, nb_batch, hn, seq, DH), F32))
            out_specs.append(pl.BlockSpec((None, nbb, hn, rb, DH), hm_map))
    for _ in range(2):
        out_shape.append(jax.ShapeDtypeStruct((nb_batch, N_HEADS, seq, DH), BF16))
        out_specs.append(pl.BlockSpec((nbb, N_HEADS, rb, DH), hm16_map))
    out_shape.append(jax.ShapeDtypeStruct((n_rows, LANES), F32))
    out_specs.append(pl.BlockSpec((tm, LANES), lambda i: (i, 0)))
    n_out = len(out_shape)

    in_specs = [
        pl.BlockSpec((tm, d), lambda i: (i + row_block_off, 0)),
        pl.BlockSpec((1, d), lambda i: (0, 0)),
        pl.BlockSpec(memory_space=pl.ANY),
        pl.BlockSpec((d, LANES), lambda i: (0, 0)),
        pl.BlockSpec((1, LANES), lambda i: (0, 0)),
    ]
    args = [x2d, g, w_bf, wf_bf, bfp]
    n_in = len(args)
    aliases = {}
    if prev is not None:
        for t, nm in enumerate(_KV_NAMES):
            in_specs.append(pl.BlockSpec(memory_space=pl.ANY))
            args.append(prev[nm])
            aliases[n_in + t] = 1 + t

    def body(*refs):
        _inproj_kernel(*refs[:n_in], *refs[len(args):len(args) + n_out], *refs[len(args) + n_out:], rb=rb)

    res = pl.pallas_call(
        body,
        grid=(nb,),
        in_specs=in_specs,
        out_specs=out_specs,
        out_shape=out_shape,
        scratch_shapes=[pltpu.VMEM(w_bf.shape, BF16), pltpu.VMEM((tm, d), BF16),
                        pltpu.SemaphoreType.DMA(())],
        input_output_aliases=aliases,
        compiler_params=pltpu.CompilerParams(
            dimension_semantics=("arbitrary",), vmem_limit_bytes=VMEM_LIMIT),
    )(*args)
    stacks = dict(zip(_KV_NAMES, res[1:7]))
    return res[0], stacks, res[7], res[8], res[9]


def _cumsum_kernel(x_ref, f_ref, carry_ref, *, tc):
    @pl.when(pl.program_id(1) == 0)
    def _():
        carry_ref[...] = jnp.zeros_like(carry_ref)

    x = x_ref[...]
    row = lax.broadcasted_iota(jnp.int32, (tc, tc), 0)
    col = lax.broadcasted_iota(jnp.int32, (tc, tc), 1)
    lower = jnp.where(row >= col, 1.0, 0.0).astype(BF16)
    x1, x2, x3 = _split3(x)
    cs = _dot(lower, x3) + _dot(lower, x2) + _dot(lower, x1) + carry_ref[...]
    f_ref[...] = cs
    carry_ref[...] = cs[tc - 1:tc, :]


def _cumsum_seq(x, tc):
    b, t, _ = x.shape
    return pl.pallas_call(
        functools.partial(_cumsum_kernel, tc=tc),
        grid=(b, t // tc),
        in_specs=[pl.BlockSpec((None, tc, LANES), lambda i, j: (i, j, 0))],
        out_specs=pl.BlockSpec((None, tc, LANES), lambda i, j: (i, j, 0)),
        out_shape=jax.ShapeDtypeStruct((b, t, LANES), F32),
        scratch_shapes=[pltpu.VMEM((1, LANES), F32)],
        compiler_params=pltpu.CompilerParams(
            dimension_semantics=("parallel", "arbitrary")),
    )(x)


def _kv_spec(arr, layer, head_off):
    if arr.ndim == 5:
        return pl.BlockSpec((None, None, None, arr.shape[3], DH),
                            lambda bi, h: (layer, bi, head_off + h, 0, 0))
    return pl.BlockSpec((None, None, arr.shape[2], DH), lambda bi, h: (layer, bi, 0, 0))


def _attn_call(body, q3, q_head_off, kv, extra, n_heads, scratch=()):
    b, tq, _ = q3.shape
    in_specs = [pl.BlockSpec((None, tq, DH), lambda bi, h: (bi, 0, q_head_off + h))]
    args = [q3]
    for arr, layer, head_off in kv:
        in_specs.append(_kv_spec(arr, layer, head_off))
        args.append(arr)
    for arr, spec in extra:
        in_specs.append(spec)
        args.append(arr)
    return pl.pallas_call(
        body,
        grid=(b, n_heads),
        in_specs=in_specs,
        out_specs=pl.BlockSpec((None, tq, DH), lambda bi, h: (bi, 0, h)),
        out_shape=jax.ShapeDtypeStruct((b, tq, n_heads * DH), BF16),
        scratch_shapes=list(scratch),
        compiler_params=pltpu.CompilerParams(
            dimension_semantics=("parallel", "parallel"), vmem_limit_bytes=VMEM_LIMIT),
    )(*args)


def _attn_a_kernel(*refs, t, tp, has_past, interleaved):
    if has_past:
        q_ref, k_ref, v_ref, kp_ref, vp_ref, o_ref = refs
    else:
        q_ref, k_ref, v_ref, o_ref = refs
    n_q = q_ref.shape[0] // t

    def later_matrix(n):
        row = lax.broadcasted_iota(jnp.int32, (n, n), 0)
        col = lax.broadcasted_iota(jnp.int32, (n, n), 1)
        return jnp.where(row > col, 1.0, 0.0).astype(BF16)

    def cond(carry):
        j, c, _ = carry
        return jnp.logical_and(j >= 0, jnp.max(c) > SB_CUTOFF)

    def make_body(q, kr, vr, tb, diag):
        later = later_matrix(tb)
        row = lax.broadcasted_iota(jnp.int32, (t, tb), 0)
        col = lax.broadcasted_iota(jnp.int32, (t, tb), 1)

        def body(carry):
            j, c, acc = carry
            start = pl.multiple_of(j * tb, tb)
            if interleaved:
                keys = pl.ds(start * H_A + pl.program_id(1), tb, stride=H_A)
            else:
                keys = pl.ds(start, tb)
            kb = kr[keys, :].astype(BF16)
            vb = vr[keys, :].astype(BF16)
            z = _nt_dot(q, kb)
            sp = jnp.maximum(z, 0.0) + jnp.log1p(jnp.exp(-jnp.abs(z)))
            if diag is None:
                lm = -sp
            else:
                mask = col < row + jnp.where(j < diag, tb, 0)
                lm = jnp.where(mask, -sp, 0.0)
            hi = lm.astype(BF16)
            lo = (lm - hi.astype(F32)).astype(BF16)
            suffix = _dot(lo, later) + _dot(hi, later)
            w = jnp.exp(z - sp + suffix + c)
            if diag is not None:
                w = jnp.where(mask, w, 0.0)
            acc = acc + _dot(w.astype(BF16), vb)
            c = c + jnp.sum(lm, axis=1, keepdims=True)
            return j - 1, c, acc

        return body

    def q_block(i, carry):
        rows = pl.ds(_aligned(i * t, t), t)
        q = q_ref[rows, :]
        state = (jnp.asarray(i, jnp.int32), jnp.zeros((t, 1), F32), jnp.zeros((t, DH), F32))
        _, c, acc = lax.while_loop(cond, make_body(q, k_ref, v_ref, t, i), state)
        if has_past:
            n_past = kp_ref.shape[0] // (tp * (H_A if interleaved else 1))
            state = (jnp.asarray(n_past - 1, jnp.int32), c, acc)
            _, c, acc = lax.while_loop(cond, make_body(q, kp_ref, vp_ref, tp, None), state)
        o_ref[rows, :] = acc.astype(o_ref.dtype)
        return carry

    if n_q == 1:
        q_block(0, 0)
    else:
        lax.fori_loop(0, n_q, q_block, 0)


def _attn_a(q3, own, past, t, tp):
    kv = list(own) + (list(past) if past else [])
    interleaved = own[0][0].ndim == 4
    return _attn_call(functools.partial(_attn_a_kernel, t=t, tp=tp, has_past=bool(past),
                                        interleaved=interleaved),
                      q3, 0, kv, [], H_A)


def _attn_b_kernel(*refs, t, tp, has_past):
    if has_past:
        q_ref, k_ref, v_ref, kp_ref, vp_ref, fk_ref, o_ref = refs
        t_past = kp_ref.shape[0]
    else:
        q_ref, k_ref, v_ref, fk_ref, o_ref = refs
        t_past = 0
    n_q = q_ref.shape[0] // t
    rt = BF16_ROWS

    def update(s, fk, vb, carry, q_minus_k):
        m, l, acc = carry
        tb = s.shape[1]
        if q_minus_k is not None:
            row = lax.broadcasted_iota(jnp.int32, (rt, tb), 0)
            col = lax.broadcasted_iota(jnp.int32, (rt, tb), 1)
        sb = s - fk
        if q_minus_k is not None:
            row_b = lax.broadcasted_iota(jnp.int32, (t, tb), 0)
            col_b = lax.broadcasted_iota(jnp.int32, (t, tb), 1)
            sb = jnp.where(col_b <= row_b + q_minus_k, sb, NEG_INF)
        m_new = jnp.maximum(m, jnp.max(sb, axis=1, keepdims=True))
        alpha = jnp.exp2(m - m_new)
        ps, parts = [], []
        for r in range(t // rt):
            s_t = s[r * rt:(r + 1) * rt] - fk
            if q_minus_k is not None:
                s_t = jnp.where(col <= row + (r * rt + q_minus_k), s_t, NEG_INF)
            p = jnp.exp2(s_t - m_new[r * rt:(r + 1) * rt])
            if tb % LANES == 0:
                parts.append(functools.reduce(
                    jnp.add, [p[:, c * LANES:(c + 1) * LANES] for c in range(tb // LANES)]))
            else:
                lane = lax.broadcasted_iota(jnp.int32, (rt, LANES), 1)
                parts.append(jnp.where(lane == 0, jnp.sum(p, axis=1, keepdims=True), 0.0))
            ps.append(p.astype(BF16))
        l = alpha * l + jnp.concatenate(parts, axis=0)
        acc = alpha * acc + _dot(jnp.concatenate(ps, axis=0), vb)
        return m_new, l, acc

    def q_block(i, carry_unused):
        rows = pl.ds(_aligned(i * t, t), t)
        q = q_ref[rows, :]
        carry = (jnp.full((t, 1), NEG_INF, F32), jnp.zeros((t, LANES), F32), jnp.zeros((t, DH), F32))

        if has_past:
            def past_body(j, c):
                start = pl.multiple_of(j * tp, tp)
                kb = kp_ref[pl.ds(start, tp), :].astype(BF16)
                vb = vp_ref[pl.ds(start, tp), :].astype(BF16)
                return update(_nt_dot(q, kb), fk_ref[:, pl.ds(start, tp)], vb, c, None)

            carry = lax.fori_loop(0, t_past // tp, past_body, carry)

        def own_start(j):
            return _aligned(j * t, t)

        def scores(j):
            return _nt_dot(q, k_ref[pl.ds(own_start(j), t), :].astype(BF16))

        def own_update(s, j, c, diag):
            vb = v_ref[pl.ds(own_start(j), t), :].astype(BF16)
            fk = fk_ref[:, pl.ds(t_past + own_start(j), t)]
            return update(s, fk, vb, c, 0 if diag else None)

        def body(j, c):
            s_cur, inner = c
            s_next = scores(j + 1)
            return s_next, own_update(s_cur, j, inner, False)

        s = scores(0)
        if n_q > 1:
            s, carry = lax.fori_loop(0, i, body, (s, carry))
        _, l, acc = own_update(s, i, carry, True)
        o_ref[rows, :] = (acc / jnp.sum(l, axis=1, keepdims=True)).astype(o_ref.dtype)
        return carry_unused

    if n_q == 1:
        q_block(0, 0)
    else:
        lax.fori_loop(0, n_q, q_block, 0)


def _attn_b_blocks_kernel(q_ref, k_ref, v_ref, fk_ref, o_ref,
                          s0_ref, s1_ref, p_ref, m_ref, l_ref, acc_ref, *, t):
    n_q = q_ref.shape[0] // t
    rt = BF16_ROWS

    def q_block(i, carry):
        rows = pl.ds(pl.multiple_of(i * t, t), t)
        q = q_ref[rows, :]
        m_ref[...] = jnp.full(m_ref.shape, NEG_INF, F32)
        l_ref[...] = jnp.zeros(l_ref.shape, F32)
        acc_ref[...] = jnp.zeros(acc_ref.shape, F32)

        def keys(j):
            return pl.ds(pl.multiple_of(j * t, t), t)

        def put_scores(dst, j):
            dst[...] = _nt_dot(q, k_ref[keys(j), :]) - fk_ref[:, keys(j)]

        def consume(src, j, diag):
            sb = src[...]
            if diag:
                row_b = lax.broadcasted_iota(jnp.int32, (t, t), 0)
                col_b = lax.broadcasted_iota(jnp.int32, (t, t), 1)
                sb = jnp.where(col_b <= row_b, sb, NEG_INF)
                row = lax.broadcasted_iota(jnp.int32, (rt, t), 0)
                col = lax.broadcasted_iota(jnp.int32, (rt, t), 1)
            m_old = m_ref[...]
            m_new = jnp.maximum(m_old, jnp.max(sb, axis=1, keepdims=True))
            alpha = jnp.exp2(m_old - m_new)
            m_ref[...] = m_new
            for r in range(t // rt):
                sl = slice(r * rt, (r + 1) * rt)
                s_t = src[sl, :]
                if diag:
                    s_t = jnp.where(col <= row + r * rt, s_t, NEG_INF)
                p = jnp.exp2(s_t - m_new[sl])
                part = functools.reduce(jnp.add, [p[:, c * LANES:(c + 1) * LANES] for c in range(t // LANES)])
                l_ref[sl, :] = alpha[sl] * l_ref[sl, :] + part
                p_ref[sl, :] = p.astype(BF16)
            acc_ref[...] = alpha * acc_ref[...] + _dot(p_ref[...], v_ref[keys(j), :])

        put_scores(s0_ref, 0)

        def pair(pp, c):
            j = 2 * pp
            put_scores(s1_ref, j + 1)
            consume(s0_ref, j, False)
            put_scores(s0_ref, j + 2)
            consume(s1_ref, j + 1, False)
            return c

        lax.fori_loop(0, i // 2, pair, 0)

        @pl.when(i % 2 == 1)
        def _():
            put_scores(s1_ref, i)
            consume(s0_ref, i - 1, False)
            consume(s1_ref, i, True)

        @pl.when(i % 2 == 0)
        def _():
            consume(s0_ref, i, True)

        o_ref[rows, :] = (acc_ref[...] / jnp.sum(l_ref[...], axis=1, keepdims=True)).astype(o_ref.dtype)
        return carry

    lax.fori_loop(0, n_q, q_block, 0)


def _attn_b(q3, own, past, frow, t, tp):
    kv = list(own) + (list(past) if past else [])
    fspec = pl.BlockSpec((None, None, 1, frow.shape[3]), lambda bi, h: (bi, h, 0, 0))
    if not past and t % LANES == 0 and own[0][0].dtype == BF16:
        scratch = [pltpu.VMEM((t, t), F32), pltpu.VMEM((t, t), F32), pltpu.VMEM((t, t), BF16),
                   pltpu.VMEM((t, 1), F32), pltpu.VMEM((t, LANES), F32), pltpu.VMEM((t, DH), F32)]
        return _attn_call(functools.partial(_attn_b_blocks_kernel, t=t),
                          q3, H_A, kv, [(frow, fspec)], H_B, scratch)
    return _attn_call(functools.partial(_attn_b_kernel, t=t, tp=tp, has_past=bool(past)),
                      q3, H_A, kv, [(frow, fspec)], H_B)


def _band_table(rel_bias, r):
    w = C_PAST + r
    lh = w + r - 1
    dist = C_PAST + r - 1 - jnp.arange(lh)
    hvec = rel_bias.astype(F32)[:, jnp.clip(dist, -REL_CLIP, REL_CLIP) + REL_CLIP]
    q = lh + 1
    hq = jnp.pad(hvec, ((0, 0), (0, 1)))
    skew = jnp.tile(hq, (1, r + 1))[:, :r * (q + 1)].reshape(H_C, r, q + 1)[:, :, :w]
    bias = skew[:, ::-1, :]
    qi = jnp.arange(r)[:, None] // CHUNK
    kj = jnp.arange(w)[None, :] // CHUNK
    ok = (kj >= qi) & (kj <= qi + C_PREV)
    return jnp.where(ok[None], bias * LOG2E, NEG_INF)


def _attn_c_kernel(*refs, r, has_past):
    if has_past:
        q_ref, k_ref, v_ref, kp_ref, vp_ref, tab_ref, o_ref = refs
    else:
        q_ref, k_ref, v_ref, tab_ref, o_ref = refs
    w = C_PAST + r

    def finish(pieces, rows):
        m = functools.reduce(jnp.maximum, [jnp.max(s, axis=1, keepdims=True) for s, _ in pieces])
        ps = [jnp.exp2(s - m) for s, _ in pieces]
        l = functools.reduce(jnp.add, [jnp.sum(p, axis=1, keepdims=True) for p in ps])
        o = functools.reduce(jnp.add, [_dot(p.astype(BF16), v) for p, (_, v) in zip(ps, pieces)])
        o_ref[rows, :] = (o / l).astype(o_ref.dtype)

    if has_past:
        q = q_ref[...]
        s_p = _nt_dot(q, kp_ref[...].astype(BF16)) + tab_ref[:, 0:C_PAST]
        s_o = _nt_dot(q, k_ref[...].astype(BF16)) + tab_ref[:, C_PAST:w]
        finish([(s_p, vp_ref[...].astype(BF16)), (s_o, v_ref[...].astype(BF16))], pl.ds(0, r))
        return

    def tile(i, start, width, tcol0):
        rows = pl.ds(_aligned(i * r, r), r)
        q = q_ref[rows, :]
        kb = k_ref[pl.ds(start, width), :].astype(BF16)
        vb = v_ref[pl.ds(start, width), :].astype(BF16)
        finish([(_nt_dot(q, kb) + tab_ref[:, tcol0:tcol0 + width], vb)], rows)

    n_tiles = q_ref.shape[0] // r
    n_short = min(C_PAST // r, n_tiles)
    for t in range(n_short):
        tile(t, 0, (t + 1) * r, C_PAST - t * r)

    def full_tile(i, carry):
        tile(i, pl.multiple_of(i * r - C_PAST, SUBLANES), w, 0)
        return carry

    lax.fori_loop(n_short, n_tiles, full_tile, 0, unroll=2)


def _attn_c(q3, own, past, table, r):
    assert C_PAST % r == 0
    kv = list(own) + (list(past) if past else [])
    tspec = pl.BlockSpec((None, r, C_PAST + r), lambda bi, h: (h, 0, 0))
    return _attn_call(functools.partial(_attn_c_kernel, r=r, has_past=bool(past)),
                      q3, H_A + H_B, kv, [(table, tspec)], H_C)


def _outproj_kernel(x_ref, oa_ref, ob_ref, oc_ref, wo_ref, g_ref, wr_ref, br_ref,
                    h_ref, hn_ref, route_ref):
    a0, a1 = H_A * DH, (H_A + H_B) * DH
    acc = _dot(oa_ref[...], wo_ref[0:a0, :])
    acc = acc + _dot(ob_ref[...], wo_ref[a0:a1, :])
    acc = acc + _dot(oc_ref[...], wo_ref[a1:MIX, :])
    h = x_ref[...] + acc
    h_ref[...] = h
    ms = jnp.mean(h * h, axis=-1, keepdims=True)
    hn = h * lax.rsqrt(ms + EPS) * g_ref[...]
    hn_ref[...] = hn

    h1, h2, _ = _split3(hn)
    w1 = wr_ref[0]
    w2 = wr_ref[1]
    lg = _dot(h2, w1) + _dot(h1, w2) + _dot(h1, w1) + br_ref[...]

    lane = lax.broadcasted_iota(jnp.int32, lg.shape, 1)
    lane_f = lane.astype(F32)
    is_g = lane < N_GROUPS
    lgm = jnp.where(is_g, lg, NEG_INF)
    gmax = jnp.max(lgm, axis=1, keepdims=True)
    gidx = jnp.min(jnp.where(lgm == gmax, lane_f, float(LANES)), axis=1, keepdims=True)
    pg = 1.0 / jnp.sum(jnp.where(is_g, jnp.exp(lgm - gmax), 0.0), axis=1, keepdims=True)
    grp = jnp.floor((lane_f - N_GROUPS) * (1.0 / EXP_PER_GROUP))
    in_grp = (lane >= N_GROUPS) & (lane < N_GROUPS + N_EXPERTS) & (grp == gidx)
    le = jnp.where(in_grp, lg, NEG_INF)
    t1 = jnp.max(le, axis=1, keepdims=True)
    i1 = jnp.min(jnp.where(le == t1, lane_f, float(LANES)), axis=1, keepdims=True)
    le2 = jnp.where(lane_f == i1, NEG_INF, le)
    t2 = jnp.max(le2, axis=1, keepdims=True)
    i2 = jnp.min(jnp.where(le2 == t2, lane_f, float(LANES)), axis=1, keepdims=True)
    e = jnp.exp(t2 - t1)
    g1 = pg / (1.0 + e)
    g2 = pg * e / (1.0 + e)
    route = jnp.where(lane == 0, i1 - N_GROUPS,
                      jnp.where(lane == 1, i2 - N_GROUPS,
                                jnp.where(lane == 2, g1, jnp.where(lane == 3, g2, 0.0))))
    route_ref[...] = route


def _outproj(x2d, x_block_off, oa, ob, oc, wo_bf, g, wr2, br, n_total, out_block_off, prev):
    n_rows, d = oa.shape[0], x2d.shape[1]
    tm = PROJ_TM
    nb = n_rows // tm
    in_specs = [
        pl.BlockSpec((tm, d), lambda i: (i + x_block_off, 0)),
        pl.BlockSpec((tm, H_A * DH), lambda i: (i, 0)),
        pl.BlockSpec((tm, H_B * DH), lambda i: (i, 0)),
        pl.BlockSpec((tm, H_C * DH), lambda i: (i, 0)),
        pl.BlockSpec((MIX, d), lambda i: (0, 0)),
        pl.BlockSpec((1, d), lambda i: (0, 0)),
        pl.BlockSpec((2, d, LANES), lambda i: (0, 0, 0)),
        pl.BlockSpec((1, LANES), lambda i: (0, 0)),
    ]
    args = [x2d, oa, ob, oc, wo_bf, g, wr2, br]
    aliases = {}
    if prev is not None:
        for t, arr in enumerate(prev):
            in_specs.append(pl.BlockSpec(memory_space=pl.ANY))
            args.append(arr)
            aliases[8 + t] = t

    def body(*refs):
        _outproj_kernel(*refs[:8], *refs[len(args):len(args) + 3])

    return pl.pallas_call(
        body,
        grid=(nb,),
        in_specs=in_specs,
        out_specs=[pl.BlockSpec((tm, d), lambda i: (i + out_block_off, 0)),
                   pl.BlockSpec((tm, d), lambda i: (i + out_block_off, 0)),
                   pl.BlockSpec((tm, LANES), lambda i: (i + out_block_off, 0))],
        out_shape=[jax.ShapeDtypeStruct((n_total, d), F32),
                   jax.ShapeDtypeStruct((n_total, d), F32),
                   jax.ShapeDtypeStruct((n_total, LANES), F32)],
        input_output_aliases=aliases,
        compiler_params=pltpu.CompilerParams(
            dimension_semantics=("parallel",), vmem_limit_bytes=VMEM_LIMIT),
    )(*args)


def _moe_kernel(be_ref, src_ref, srcn_ref, hn_hbm, wg_ref, wu_ref, wd_ref, yb_ref,
                xbuf, xb_ref, wg16, wu16, wd16, sem):
    i = pl.program_id(0)
    last = pl.num_programs(0) - 1
    blk = xbuf.shape[1]
    slot = i % 2

    def gather(idx_ref, dst_slot, lo=0, hi=None):
        for r in range(lo, blk if hi is None else hi):
            pltpu.make_async_copy(hn_hbm.at[pl.ds(idx_ref[0, 0, r], 1), :],
                                  xbuf.at[dst_slot, pl.ds(r, 1), :], sem.at[dst_slot]).start()

    def wait_rows(s):
        pltpu.make_async_copy(hn_hbm.at[pl.ds(0, blk), :], xbuf.at[s], sem.at[s]).wait()

    @pl.when(i == 0)
    def _():
        gather(src_ref, 0)

    @pl.when(jnp.logical_or(i == 0, be_ref[i] != be_ref[jnp.maximum(i - 1, 0)]))
    def _():
        wg16[...] = wg_ref[...].astype(BF16)
        wu16[...] = wu_ref[...].astype(BF16)
        wd16[...] = wd_ref[...].astype(BF16)

    wait_rows(slot)
    d = xbuf.shape[2]
    de = wg16.shape[1]
    n_grp = 8
    per = blk // n_grp
    groups = iter(range(n_grp))

    def next_rows():
        g = next(groups)
        gather(srcn_ref, 1 - slot, g * per, (g + 1) * per)

    xb_ref[...] = xbuf[slot].astype(BF16)
    half = de // 2
    acts = []
    for c in range(2):
        cols = slice(c * half, (c + 1) * half)
        gate = _dot(xb_ref[...], wg16[:, cols])
        next_rows()
        up = _dot(xb_ref[...], wu16[:, cols])
        next_rows()
        acts.append((gate * jax.nn.sigmoid(gate) * up).astype(BF16))
    act = jnp.concatenate(acts, axis=1)
    quarter = d // 4
    for c in range(4):
        cols = slice(c * quarter, (c + 1) * quarter)
        yb_ref[:, cols] = _dot(act, wd16[:, cols])
        next_rows()

    @pl.when(i == last)
    def _():
        wait_rows(1 - slot)


def _moe(hn, src, block_e, wg, wu, wd, layer, n_blocks):
    d = hn.shape[1]
    de = wg.shape[3]
    src3 = src.reshape(n_blocks, 1, MOE_BLK)
    last = n_blocks - 1
    grid_spec = pltpu.PrefetchScalarGridSpec(
        num_scalar_prefetch=1,
        grid=(n_blocks,),
        in_specs=[
            pl.BlockSpec((1, 1, MOE_BLK), lambda i, be: (i, 0, 0), memory_space=pltpu.SMEM),
            pl.BlockSpec((1, 1, MOE_BLK), lambda i, be: (jnp.minimum(i + 1, last), 0, 0),
                         memory_space=pltpu.SMEM),
            pl.BlockSpec(memory_space=pl.ANY),
            pl.BlockSpec((None, None, d, de), lambda i, be: (layer, be[i], 0, 0)),
            pl.BlockSpec((None, None, d, de), lambda i, be: (layer, be[i], 0, 0)),
            pl.BlockSpec((None, None, de, d), lambda i, be: (layer, be[i], 0, 0)),
        ],
        out_specs=pl.BlockSpec((MOE_BLK, d), lambda i, be: (i, 0)),
        scratch_shapes=[pltpu.VMEM((2, MOE_BLK, d), F32), pltpu.VMEM((MOE_BLK, d), BF16),
                        pltpu.VMEM((d, de), BF16), pltpu.VMEM((d, de), BF16), pltpu.VMEM((de, d), BF16),
                        pltpu.SemaphoreType.DMA((2,))],
    )
    return pl.pallas_call(
        _moe_kernel,
        grid_spec=grid_spec,
        out_shape=jax.ShapeDtypeStruct((n_blocks * MOE_BLK, d), F32),
        compiler_params=pltpu.CompilerParams(
            dimension_semantics=("arbitrary",), vmem_limit_bytes=VMEM_LIMIT),
    )(block_e, src3, src3, hn, wg, wu, wd)


def _dispatch(eid, n_blocks):
    p = eid.shape[0] * eid.shape[1]
    e = eid.reshape(-1)
    onehot = (e[:, None] == jnp.arange(N_EXPERTS, dtype=jnp.int32)[None, :]).astype(jnp.int32)
    cs = jnp.cumsum(onehot, axis=0)
    rank = jnp.sum(cs * onehot, axis=1) - 1
    counts = cs[-1]
    padded = (counts + MOE_BLK - 1) // MOE_BLK * MOE_BLK
    pend = jnp.cumsum(padded)
    pstart = pend - padded
    dest = (jnp.sum(onehot * pstart[None, :], axis=1) + rank).astype(jnp.int32)
    src = jnp.zeros((n_blocks * MOE_BLK,), jnp.int32).at[dest].set(
        jnp.arange(p, dtype=jnp.int32) // eid.shape[1])
    blk_start = jnp.arange(n_blocks, dtype=jnp.int32) * MOE_BLK
    block_e = jnp.minimum(jnp.sum((pend[None, :] <= blk_start[:, None]).astype(jnp.int32), axis=1),
                          N_EXPERTS - 1).astype(jnp.int32)
    return dest, src, block_e


def _combine_kernel(pos_ref, h_ref, route_ref, yb_hbm, g_ref, o_ref, buf, sem, *, final):
    tm = h_ref.shape[0]

    def group(gi, carry):
        base = pl.multiple_of(gi * SUBLANES, SUBLANES)
        for u in range(SUBLANES):
            for s in range(2):
                pltpu.make_async_copy(yb_hbm.at[pl.ds(pos_ref[0, 0, 2 * (base + u) + s], 1), :],
                                      buf.at[s, pl.ds(base + u, 1), :], sem).start()
        return carry

    lax.fori_loop(0, tm // SUBLANES, group, 0)
    pltpu.make_async_copy(yb_hbm.at[pl.ds(0, tm), :], buf.at[0], sem).wait()
    pltpu.make_async_copy(yb_hbm.at[pl.ds(0, tm), :], buf.at[1], sem).wait()
    route = route_ref[...]
    g1 = _lane_pick(route, 2)
    g2 = _lane_pick(route, 3)
    y = h_ref[...] + (g1 * buf[0] + g2 * buf[1])
    if final:
        ms = jnp.mean(y * y, axis=-1, keepdims=True)
        y = y * lax.rsqrt(ms + EPS) * g_ref[...]
    o_ref[...] = y


def _combine(pos3, h, route, yb, g, block_off, n_rows, final):
    d = h.shape[1]
    tm = COMBINE_TM
    return pl.pallas_call(
        functools.partial(_combine_kernel, final=final),
        grid=(n_rows // tm,),
        in_specs=[
            pl.BlockSpec((1, 1, 2 * tm), lambda i: (i + block_off, 0, 0), memory_space=pltpu.SMEM),
            pl.BlockSpec((tm, d), lambda i: (i + block_off, 0)),
            pl.BlockSpec((tm, LANES), lambda i: (i + block_off, 0)),
            pl.BlockSpec(memory_space=pl.ANY),
            pl.BlockSpec((1, d), lambda i: (0, 0)),
        ],
        out_specs=pl.BlockSpec((tm, d), lambda i: (i, 0)),
        out_shape=jax.ShapeDtypeStruct((n_rows, d), F32),
        scratch_shapes=[pltpu.VMEM((2, tm, d), F32), pltpu.SemaphoreType.DMA(())],
        compiler_params=pltpu.CompilerParams(
            dimension_semantics=("arbitrary",), vmem_limit_bytes=VMEM_LIMIT),
    )(pos3, h, route, yb, g)


def _pad_lanes(a):
    return jnp.pad(a, [(0, 0)] * (a.ndim - 1) + [(0, LANES - a.shape[-1])])


def kernel(x_prompt, x_sample, cache_a_k, cache_a_v, cache_b_k, cache_b_v, cache_b_logf,
           cache_c_k, cache_c_v, norm1_g, norm2_g, w_in, b_f, rel_bias, w_o,
           w_rg, b_rg, w_re, b_re, w_gate, w_up, w_down, final_g):
    bp, sp, d = x_prompt.shape
    bs, ss, _ = x_sample.shape
    depth = w_in.shape[0]
    past = cache_a_k.shape[2]
    keep = cache_c_k.shape[2]
    n_p, n_s = bp * sp, bs * ss
    n_tot = n_p + n_s
    assert d == MIX and n_p % PROJ_TM == 0 and n_s % PROJ_TM == 0
    assert ss == CHUNK and keep == C_PAST

    t_a = min(256, sp)
    t_b = min(512, sp)
    r_p = min(256, sp)
    tp_a = min(256, past)
    tp_b = min(512, past)
    assert past % tp_a == 0 and past % tp_b == 0
    n_blocks = (2 * n_tot) // MOE_BLK + N_EXPERTS

    cb_k, cb_v = jnp.swapaxes(cache_b_k, 2, 3), jnp.swapaxes(cache_b_v, 2, 3)
    cc_k, cc_v = jnp.swapaxes(cache_c_k, 2, 3), jnp.swapaxes(cache_c_v, 2, 3)
    ca_k = cache_a_k.reshape(depth, bs, past * H_A, DH)
    ca_v = cache_a_v.reshape(depth, bs, past * H_A, DH)

    xs_p = (x_prompt.reshape(n_p, d), 0)
    xs_s = (x_sample.reshape(n_s, d), 0)
    stacks_p = stacks_s = None
    logf_p, logf_s = [], []
    y_prompt = y_sample = None

    for l in range(depth):
        w_bf = w_in[l, :, :3 * MIX].astype(BF16)
        wf_bf = _pad_lanes(w_in[l, :, 3 * MIX:]).astype(BF16)
        bfp = _pad_lanes(b_f[l][None, :].astype(F32))
        g1 = norm1_g[l][None, :].astype(F32)
        q_p, stacks_p, k16_p, v16_p, lf_p = _inproj(xs_p[0], xs_p[1] * (PROJ_TM // INPROJ_TM), bp, sp, g1,
                                                    w_bf, wf_bf, bfp, l, depth, stacks_p)
        q_s, stacks_s, _, _, lf_s = _inproj(xs_s[0], xs_s[1] * (PROJ_TM // INPROJ_TM), bs, ss, g1,
                                            w_bf, wf_bf, bfp, l, depth, stacks_s)
        logf_p.append(lf_p[:, :H_B].reshape(bp, sp, H_B))
        logf_s.append(lf_s[:, :H_B].reshape(bs, ss, H_B))
        q_p3 = q_p.reshape(bp, sp, MIX)
        q_s3 = q_s.reshape(bs, ss, MIX)
        k16_p, v16_p = k16_p[None], v16_p[None]
        ka_s = stacks_s["ka"].reshape(depth, bs, ss * H_A, DH)
        va_s = stacks_s["va"].reshape(depth, bs, ss * H_A, DH)

        oa_p = _attn_a(q_p3, [(k16_p, 0, 0), (v16_p, 0, 0)], None, t_a, t_a)
        oa_s = _attn_a(q_s3, [(ka_s, l, 0), (va_s, l, 0)], [(ca_k, l, 0), (ca_v, l, 0)], ss, tp_a)

        def row_layout(fcol):
            return (jnp.swapaxes(fcol[:, :, :H_B], 1, 2) * LOG2E)[:, :, None, :]

        fcol_p = _cumsum_seq(lf_p.reshape(bp, sp, LANES), min(512, sp))
        lf_cat = jnp.concatenate([_pad_lanes(cache_b_logf[l].astype(F32)),
                                  lf_s.reshape(bs, ss, LANES)], axis=1)
        t_cat = past + ss
        tc_s = t_cat // 3 if (t_cat % 24 == 0) else t_cat
        fcol_s = _cumsum_seq(lf_cat, tc_s)
        ob_p = _attn_b(q_p3, [(k16_p, 0, H_A), (v16_p, 0, H_A)], None, row_layout(fcol_p), t_b, t_b)
        ob_s = _attn_b(q_s3, [(stacks_s["kb"], l, 0), (stacks_s["vb"], l, 0)],
                       [(cb_k, l, 0), (cb_v, l, 0)], row_layout(fcol_s), ss, tp_b)

        oc_p = _attn_c(q_p3, [(k16_p, 0, H_A + H_B), (v16_p, 0, H_A + H_B)], None,
                       _band_table(rel_bias[l], r_p), r_p)
        oc_s = _attn_c(q_s3, [(stacks_s["kc"], l, 0), (stacks_s["vc"], l, 0)],
                       [(cc_k, l, 0), (cc_v, l, 0)], _band_table(rel_bias[l], ss), ss)

        wo_bf = w_o[l].astype(BF16)
        g2 = norm2_g[l][None, :].astype(F32)
        wr = _pad_lanes(jnp.concatenate([w_rg[l], w_re[l]], axis=1).astype(F32))
        wr1 = wr.astype(BF16)
        wr2 = jnp.stack([wr1, (wr - wr1.astype(F32)).astype(BF16)])
        br = _pad_lanes(jnp.concatenate([b_rg[l], b_re[l]])[None, :].astype(F32))
        shared = _outproj(xs_p[0], xs_p[1], oa_p.reshape(n_p, -1), ob_p.reshape(n_p, -1),
                          oc_p.reshape(n_p, -1), wo_bf, g2, wr2, br, n_tot, 0, None)
        h, hn, route = _outproj(xs_s[0], xs_s[1], oa_s.reshape(n_s, -1), ob_s.reshape(n_s, -1),
                                oc_s.reshape(n_s, -1), wo_bf, g2, wr2, br, n_tot,
                                n_p // PROJ_TM, shared)

        eid = route[:, :2].astype(jnp.int32)
        dest, src, block_e = _dispatch(eid, n_blocks)
        yb = _moe(hn, src, block_e, w_gate, w_up, w_down, l, n_blocks)
        pos3 = dest.reshape(n_tot // COMBINE_TM, 1, 2 * COMBINE_TM)
        fg = final_g[None, :].astype(F32)
        if l + 1 < depth:
            y = _combine(pos3, h, route, yb, fg, 0, n_tot, False)
            xs_p = (y, 0)
            xs_s = (y, n_p // PROJ_TM)
        else:
            y_prompt = _combine(pos3, h, route, yb, fg, 0, n_p, True).reshape(bp, sp, d)
            y_sample = _combine(pos3, h, route, yb, fg, n_p // COMBINE_TM, n_s, True).reshape(bs, ss, d)

    def tok_state(stacks, nm, b, s):
        return stacks[nm]

    def hm_state(stacks, nm):
        return jnp.swapaxes(stacks[nm], 2, 3)

    keep_p = min(C_PAST, sp)
    return (y_prompt, y_sample,
            tok_state(stacks_p, "ka", bp, sp), tok_state(stacks_p, "va", bp, sp),
            hm_state(stacks_p, "kb"), hm_state(stacks_p, "vb"),
            jnp.stack(logf_p, axis=0),
            hm_state(stacks_p, "kc")[:, :, sp - keep_p:], hm_state(stacks_p, "vc")[:, :, sp - keep_p:],
            tok_state(stacks_s, "ka", bs, ss), tok_state(stacks_s, "va", bs, ss),
            hm_state(stacks_s, "kb"), hm_state(stacks_s, "vb"),
            jnp.stack(logf_s, axis=0),
            hm_state(stacks_s, "kc"), hm_state(stacks_s, "vc"))
```

```python
import functools
import math

import jax
import jax.numpy as jnp
from jax import lax
from jax.experimental import pallas as pl
from jax.experimental.pallas import tpu as pltpu

F32 = jnp.float32
BF16 = jnp.bfloat16

DH = 128
H_A, H_B, H_C = 4, 6, 6
N_HEADS = H_A + H_B + H_C
MIX = N_HEADS * DH
CHUNK = 64
C_PREV = 8
C_PAST = C_PREV * CHUNK
REL_CLIP = 128
N_GROUPS = 4
EXP_PER_GROUP = 8
N_EXPERTS = N_GROUPS * EXP_PER_GROUP
EPS = 1e-6
NEG_INF = -1e30
SCALE = DH ** -0.5
LOG2E = math.log2(math.e)

LANES = 128
SUBLANES = 8
BF16_ROWS = 16
INPROJ_TM = 256
PROJ_TM = 512
PROJ_TN = 512
MOE_BLK = 256
COMBINE_TM = 256
SB_CUTOFF = -104.0
VMEM_LIMIT = 56 * 1024 * 1024


def _nt_dot(a, b):
    return lax.dot_general(a, b, (((1,), (1,)), ((), ())), preferred_element_type=F32)


def _dot(a, b):
    return jnp.dot(a, b, preferred_element_type=F32)


def _split3(x):
    x1 = x.astype(BF16)
    r1 = x - x1.astype(F32)
    x2 = r1.astype(BF16)
    x3 = (r1 - x2.astype(F32)).astype(BF16)
    return x1, x2, x3


def _aligned(x, m):
    return x if isinstance(x, int) else pl.multiple_of(x, m)


def _lane_pick(x, idx):
    lane = lax.broadcasted_iota(jnp.int32, x.shape, 1)
    return jnp.sum(jnp.where(lane == idx, x, 0.0), axis=1, keepdims=True)


def _inproj_kernel(x_ref, g_ref, w_hbm, wf_ref, bf_ref,
                   q_ref, ka_ref, va_ref, kb_ref, vb_ref, kc_ref, vc_ref, k16_ref, v16_ref, lf_ref,
                   w_vmem, xn_ref, sem, *, rb):
    tm = x_ref.shape[0]
    nbb = tm // rb
    heads_per_chunk = PROJ_TN // DH
    chunks_per_part = MIX // PROJ_TN

    @pl.when(pl.program_id(0) == 0)
    def _():
        cp = pltpu.make_async_copy(w_hbm, w_vmem, sem)
        cp.start()
        cp.wait()

    x = x_ref[...]
    ms = jnp.mean(x * x, axis=-1, keepdims=True)
    xn_ref[...] = (x * lax.rsqrt(ms + EPS) * g_ref[...]).astype(BF16)
    lf_ref[...] = jax.nn.log_sigmoid(_dot(xn_ref[...], wf_ref[...]) + bf_ref[...])

    def write_kv(jj, acc, tok_ref, b_ref, c_ref, h16_ref):
        for u in range(heads_per_chunk):
            hh = (jj * heads_per_chunk + u) % N_HEADS
            piece = acc[:, u * DH:(u + 1) * DH]
            for bb in range(nbb):
                rows = piece[bb * rb:(bb + 1) * rb]
                h16_ref[bb, hh] = rows.astype(BF16)
                if hh < H_A:
                    tok_ref[bb, :, hh, :] = rows
                elif hh < H_A + H_B:
                    b_ref[bb, hh - H_A] = rows
                elif hh >= H_A + H_B:
                    c_ref[bb, hh - H_A - H_B] = rows

    for jj in range(3 * chunks_per_part):
        acc = _dot(xn_ref[...], w_vmem[:, jj * PROJ_TN:(jj + 1) * PROJ_TN])
        part = jj // chunks_per_part
        if part == 0:
            sc = SCALE if (jj + 1) * heads_per_chunk <= H_A else SCALE * LOG2E
            q_ref[:, jj * PROJ_TN:(jj + 1) * PROJ_TN] = (acc * sc).astype(BF16)
        elif part == 1:
            write_kv(jj, acc, ka_ref, kb_ref, kc_ref, k16_ref)
        else:
            write_kv(jj, acc, va_ref, vb_ref, vc_ref, v16_ref)


_KV_NAMES = ["ka", "va", "kb", "vb", "kc", "vc"]


def _inproj(x2d, row_block_off, nb_batch, seq, g, w_bf, wf_bf, bfp, layer, depth, prev):
    assert H_A * DH == PROJ_TN
    d = x2d.shape[1]
    tm = INPROJ_TM
    n_rows = nb_batch * seq
    nb = n_rows // tm
    rb = min(seq, tm)
    nbb = tm // rb
    spb = seq // rb

    def hm_map(i):
        return (layer, i // spb, 0, i % spb, 0)

    def hm16_map(i):
        return (i // spb, 0, i % spb, 0)

    out_shape = [jax.ShapeDtypeStruct((n_rows, MIX), BF16)]
    out_specs = [pl.BlockSpec((tm, MIX), lambda i: (i, 0))]
    for nm in _KV_NAMES:
        if nm in ("ka", "va"):
            out_shape.append(jax.ShapeDtypeStruct((depth, nb_batch, seq, H_A, DH), F32))
            out_specs.append(pl.BlockSpec((None, nbb, rb, H_A, DH),
                                          lambda i: (layer, i // spb, i % spb, 0, 0)))
        else:
            hn = H_B if nm[1] == "b" else H_C
            out_shape.append(jax.ShapeDtypeStruct((depth, nb_batch, hn, seq, DH), F32))
            out_specs.append(pl.BlockSpec((None, nbb, hn, rb, DH), hm_map))
    for _ in range(2):
        out_shape.append(jax.ShapeDtypeStruct((nb_batch, N_HEADS, seq, DH), BF16))
        out_specs.append(pl.BlockSpec((nbb, N_HEADS, rb, DH), hm16_map))
    out_shape.append(jax.ShapeDtypeStruct((n_rows, LANES), F32))
    out_specs.append(pl.BlockSpec((tm, LANES), lambda i: (i, 0)))
    n_out = len(out_shape)

    in_specs = [
        pl.BlockSpec((tm, d), lambda i: (i + row_block_off, 0)),
        pl.BlockSpec((1, d), lambda i: (0, 0)),
        pl.BlockSpec(memory_space=pl.ANY),
        pl.BlockSpec((d, LANES), lambda i: (0, 0)),
        pl.BlockSpec((1, LANES), lambda i: (0, 0)),
    ]
    args = [x2d, g, w_bf, wf_bf, bfp]
    n_in = len(args)
    aliases = {}
    if prev is not None:
        for t, nm in enumerate(_KV_NAMES):
            in_specs.append(pl.BlockSpec(memory_space=pl.ANY))
            args.append(prev[nm])
            aliases[n_in + t] = 1 + t

    def body(*refs):
        _inproj_kernel(*refs[:n_in], *refs[len(args):len(args) + n_out], *refs[len(args) + n_out:], rb=rb)

    res = pl.pallas_call(
        body,
        grid=(nb,),
        in_specs=in_specs,
        out_specs=out_specs,
        out_shape=out_shape,
        scratch_shapes=[pltpu.VMEM(w_bf.shape, BF16), pltpu.VMEM((tm, d), BF16),
                        pltpu.SemaphoreType.DMA(())],
        input_output_aliases=aliases,
        compiler_params=pltpu.CompilerParams(
            dimension_semantics=("arbitrary",), vmem_limit_bytes=VMEM_LIMIT),
    )(*args)
    stacks = dict(zip(_KV_NAMES, res[1:7]))
    return res[0], stacks, res[7], res[8], res[9]


def _cumsum_kernel(x_ref, f_ref, carry_ref, *, tc):
    @pl.when(pl.program_id(1) == 0)
    def _():
        carry_ref[...] = jnp.zeros_like(carry_ref)

    x = x_ref[...]
    row = lax.broadcasted_iota(jnp.int32, (tc, tc), 0)
    col = lax.broadcasted_iota(jnp.int32, (tc, tc), 1)
    lower = jnp.where(row >= col, 1.0, 0.0).astype(BF16)
    x1, x2, x3 = _split3(x)
    cs = _dot(lower, x3) + _dot(lower, x2) + _dot(lower, x1) + carry_ref[...]
    f_ref[...] = cs
    carry_ref[...] = cs[tc - 1:tc, :]


def _cumsum_seq(x, tc):
    b, t, _ = x.shape
    return pl.pallas_call(
        functools.partial(_cumsum_kernel, tc=tc),
        grid=(b, t // tc),
        in_specs=[pl.BlockSpec((None, tc, LANES), lambda i, j: (i, j, 0))],
        out_specs=pl.BlockSpec((None, tc, LANES), lambda i, j: (i, j, 0)),
        out_shape=jax.ShapeDtypeStruct((b, t, LANES), F32),
        scratch_shapes=[pltpu.VMEM((1, LANES), F32)],
        compiler_params=pltpu.CompilerParams(
            dimension_semantics=("parallel", "arbitrary")),
    )(x)


def _kv_spec(arr, layer, head_off):
    if arr.ndim == 5:
        return pl.BlockSpec((None, None, None, arr.shape[3], DH),
                            lambda bi, h: (layer, bi, head_off + h, 0, 0))
    return pl.BlockSpec((None, None, arr.shape[2], DH), lambda bi, h: (layer, bi, 0, 0))


def _attn_call(body, q3, q_head_off, kv, extra, n_heads, scratch=()):
    b, tq, _ = q3.shape
    in_specs = [pl.BlockSpec((None, tq, DH), lambda bi, h: (bi, 0, q_head_off + h))]
    args = [q3]
    for arr, layer, head_off in kv:
        in_specs.append(_kv_spec(arr, layer, head_off))
        args.append(arr)
    for arr, spec in extra:
        in_specs.append(spec)
        args.append(arr)
    return pl.pallas_call(
        body,
        grid=(b, n_heads),
        in_specs=in_specs,
        out_specs=pl.BlockSpec((None, tq, DH), lambda bi, h: (bi, 0, h)),
        out_shape=jax.ShapeDtypeStruct((b, tq, n_heads * DH), BF16),
        scratch_shapes=list(scratch),
        compiler_params=pltpu.CompilerParams(
            dimension_semantics=("parallel", "parallel"), vmem_limit_bytes=VMEM_LIMIT),
    )(*args)


def _attn_a_kernel(*refs, t, tp, has_past, interleaved):
    if has_past:
        q_ref, k_ref, v_ref, kp_ref, vp_ref, o_ref = refs
    else:
        q_ref, k_ref, v_ref, o_ref = refs
    n_q = q_ref.shape[0] // t

    def later_matrix(n):
        row = lax.broadcasted_iota(jnp.int32, (n, n), 0)
        col = lax.broadcasted_iota(jnp.int32, (n, n), 1)
        return jnp.where(row > col, 1.0, 0.0).astype(BF16)

    def cond(carry):
        j, c, _ = carry
        return jnp.logical_and(j >= 0, jnp.max(c) > SB_CUTOFF)

    def block(q, kr, vr, tb, j, c, acc, diag, valid=None):
        start = _aligned(j * tb, tb)
        if interleaved:
            keys = pl.ds(start * H_A + pl.program_id(1), tb, stride=H_A)
        else:
            keys = pl.ds(start, tb)
        kb = kr[keys, :].astype(BF16)
        vb = vr[keys, :].astype(BF16)
        later = later_matrix(tb)
        z = _nt_dot(q, kb)
        sp = jnp.maximum(z, 0.0) + jnp.log1p(jnp.exp(-jnp.abs(z)))
        lm = -sp
        if diag:
            row = lax.broadcasted_iota(jnp.int32, (t, tb), 0)
            col = lax.broadcasted_iota(jnp.int32, (t, tb), 1)
            mask = col < row
            lm = jnp.where(mask, lm, 0.0)
        if valid is not None:
            lm = lm * valid
        hi = lm.astype(BF16)
        lo = (lm - hi.astype(F32)).astype(BF16)
        suffix = _dot(lo, later) + _dot(hi, later)
        w = jnp.exp(z - sp + suffix + c)
        if diag:
            w = jnp.where(mask, w, 0.0)
        if valid is not None:
            w = w * valid
        return c + jnp.sum(lm, axis=1, keepdims=True), acc + _dot(w.astype(BF16), vb)

    def walk(q, kr, vr, tb, j0, c, acc):
        def body(carry):
            j, c_in, acc_in = carry
            c_out, acc_out = block(q, kr, vr, tb, j, c_in, acc_in, False)
            return j - 1, c_out, acc_out

        _, c, acc = lax.while_loop(cond, body, (jnp.asarray(j0, jnp.int32), c, acc))
        return c, acc

    def q_block(i, carry):
        rows = pl.ds(_aligned(i * t, t), t)
        q = q_ref[rows, :]
        c, acc = jnp.zeros((t, 1), F32), jnp.zeros((t, DH), F32)
        c, acc = block(q, k_ref, v_ref, t, i, c, acc, True)
        if n_q > 1:
            has_prev = jnp.where(i >= 1, 1.0, 0.0)
            c, acc = block(q, k_ref, v_ref, t, jnp.maximum(i - 1, 0), c, acc, False, has_prev)
            c, acc = walk(q, k_ref, v_ref, t, i - 2, c, acc)
        if has_past:
            n_past = kp_ref.shape[0] // (tp * (H_A if interleaved else 1))
            c, acc = walk(q, kp_ref, vp_ref, tp, n_past - 1, c, acc)
        o_ref[rows, :] = acc.astype(o_ref.dtype)
        return carry

    if n_q == 1:
        q_block(0, 0)
    else:
        lax.fori_loop(0, n_q, q_block, 0)


def _attn_a(q3, own, past, t, tp):
    kv = list(own) + (list(past) if past else [])
    interleaved = own[0][0].ndim == 4
    return _attn_call(functools.partial(_attn_a_kernel, t=t, tp=tp, has_past=bool(past),
                                        interleaved=interleaved),
                      q3, 0, kv, [], H_A)


def _attn_b_kernel(*refs, t, tp, has_past):
    if has_past:
        q_ref, k_ref, v_ref, kp_ref, vp_ref, fk_ref, o_ref = refs
        t_past = kp_ref.shape[0]
    else:
        q_ref, k_ref, v_ref, fk_ref, o_ref = refs
        t_past = 0
    n_q = q_ref.shape[0] // t
    rt = BF16_ROWS

    def update(s, fk, vb, carry, q_minus_k):
        m, l, acc = carry
        tb = s.shape[1]
        if q_minus_k is not None:
            row = lax.broadcasted_iota(jnp.int32, (rt, tb), 0)
            col = lax.broadcasted_iota(jnp.int32, (rt, tb), 1)
        sb = s - fk
        if q_minus_k is not None:
            row_b = lax.broadcasted_iota(jnp.int32, (t, tb), 0)
            col_b = lax.broadcasted_iota(jnp.int32, (t, tb), 1)
            sb = jnp.where(col_b <= row_b + q_minus_k, sb, NEG_INF)
        m_new = jnp.maximum(m, jnp.max(sb, axis=1, keepdims=True))
        alpha = jnp.exp2(m - m_new)
        ps, parts = [], []
        for r in range(t // rt):
            s_t = s[r * rt:(r + 1) * rt] - fk
            if q_minus_k is not None:
                s_t = jnp.where(col <= row + (r * rt + q_minus_k), s_t, NEG_INF)
            p = jnp.exp2(s_t - m_new[r * rt:(r + 1) * rt])
            if tb % LANES == 0:
                parts.append(functools.reduce(
                    jnp.add, [p[:, c * LANES:(c + 1) * LANES] for c in range(tb // LANES)]))
            else:
                lane = lax.broadcasted_iota(jnp.int32, (rt, LANES), 1)
                parts.append(jnp.where(lane == 0, jnp.sum(p, axis=1, keepdims=True), 0.0))
            ps.append(p.astype(BF16))
        l = alpha * l + jnp.concatenate(parts, axis=0)
        acc = alpha * acc + _dot(jnp.concatenate(ps, axis=0), vb)
        return m_new, l, acc

    def q_block(i, carry_unused):
        rows = pl.ds(_aligned(i * t, t), t)
        q = q_ref[rows, :]
        carry = (jnp.full((t, 1), NEG_INF, F32), jnp.zeros((t, LANES), F32), jnp.zeros((t, DH), F32))

        if has_past:
            def past_body(j, c):
                start = pl.multiple_of(j * tp, tp)
                kb = kp_ref[pl.ds(start, tp), :].astype(BF16)
                vb = vp_ref[pl.ds(start, tp), :].astype(BF16)
                return update(_nt_dot(q, kb), fk_ref[:, pl.ds(start, tp)], vb, c, None)

            carry = lax.fori_loop(0, t_past // tp, past_body, carry)

        def own_start(j):
            return _aligned(j * t, t)

        def scores(j):
            return _nt_dot(q, k_ref[pl.ds(own_start(j), t), :].astype(BF16))

        def own_update(s, j, c, diag):
            vb = v_ref[pl.ds(own_start(j), t), :].astype(BF16)
            fk = fk_ref[:, pl.ds(t_past + own_start(j), t)]
            return update(s, fk, vb, c, 0 if diag else None)

        def body(j, c):
            s_cur, inner = c
            s_next = scores(j + 1)
            return s_next, own_update(s_cur, j, inner, False)

        s = scores(0)
        if n_q > 1:
            s, carry = lax.fori_loop(0, i, body, (s, carry))
        _, l, acc = own_update(s, i, carry, True)
        o_ref[rows, :] = (acc / jnp.sum(l, axis=1, keepdims=True)).astype(o_ref.dtype)
        return carry_unused

    if n_q == 1:
        q_block(0, 0)
    else:
        lax.fori_loop(0, n_q, q_block, 0)


def _attn_b_blocks_kernel(q_ref, k_ref, v_ref, fk_ref, o_ref,
                          s0_ref, s1_ref, p_ref, m_ref, l_ref, acc_ref, *, t):
    n_q = q_ref.shape[0] // t
    rt = BF16_ROWS

    def q_block(i, carry):
        rows = pl.ds(pl.multiple_of(i * t, t), t)
        q = q_ref[rows, :]
        m_ref[...] = jnp.full(m_ref.shape, NEG_INF, F32)
        l_ref[...] = jnp.zeros(l_ref.shape, F32)
        acc_ref[...] = jnp.zeros(acc_ref.shape, F32)

        def keys(j):
            return pl.ds(pl.multiple_of(j * t, t), t)

        def put_scores(dst, j):
            dst[...] = _nt_dot(q, k_ref[keys(j), :]) - fk_ref[:, keys(j)]

        def consume(src, j, diag):
            sb = src[...]
            if diag:
                row_b = lax.broadcasted_iota(jnp.int32, (t, t), 0)
                col_b = lax.broadcasted_iota(jnp.int32, (t, t), 1)
                sb = jnp.where(col_b <= row_b, sb, NEG_INF)
                row = lax.broadcasted_iota(jnp.int32, (rt, t), 0)
                col = lax.broadcasted_iota(jnp.int32, (rt, t), 1)
            m_old = m_ref[...]
            m_new = jnp.maximum(m_old, jnp.max(sb, axis=1, keepdims=True))
            alpha = jnp.exp2(m_old - m_new)
            m_ref[...] = m_new
            for r in range(t // rt):
                sl = slice(r * rt, (r + 1) * rt)
                s_t = src[sl, :]
                if diag:
                    s_t = jnp.where(col <= row + r * rt, s_t, NEG_INF)
                p = jnp.exp2(s_t - m_new[sl])
                part = functools.reduce(jnp.add, [p[:, c * LANES:(c + 1) * LANES] for c in range(t // LANES)])
                l_ref[sl, :] = alpha[sl] * l_ref[sl, :] + part
                p_ref[sl, :] = p.astype(BF16)
            acc_ref[...] = alpha * acc_ref[...] + _dot(p_ref[...], v_ref[keys(j), :])

        put_scores(s0_ref, 0)

        def pair(pp, c):
            j = 2 * pp
            put_scores(s1_ref, j + 1)
            consume(s0_ref, j, False)
            put_scores(s0_ref, j + 2)
            consume(s1_ref, j + 1, False)
            return c

        lax.fori_loop(0, i // 2, pair, 0)

        @pl.when(i % 2 == 1)
        def _():
            put_scores(s1_ref, i)
            consume(s0_ref, i - 1, False)
            consume(s1_ref, i, True)

        @pl.when(i % 2 == 0)
        def _():
            consume(s0_ref, i, True)

        o_ref[rows, :] = (acc_ref[...] / jnp.sum(l_ref[...], axis=1, keepdims=True)).astype(o_ref.dtype)
        return carry

    lax.fori_loop(0, n_q, q_block, 0)


def _attn_b(q3, own, past, frow, t, tp):
    kv = list(own) + (list(past) if past else [])
    fspec = pl.BlockSpec((None, None, 1, frow.shape[3]), lambda bi, h: (bi, h, 0, 0))
    if not past and t % LANES == 0 and own[0][0].dtype == BF16:
        scratch = [pltpu.VMEM((t, t), F32), pltpu.VMEM((t, t), F32), pltpu.VMEM((t, t), BF16),
                   pltpu.VMEM((t, 1), F32), pltpu.VMEM((t, LANES), F32), pltpu.VMEM((t, DH), F32)]
        return _attn_call(functools.partial(_attn_b_blocks_kernel, t=t),
                          q3, H_A, kv, [(frow, fspec)], H_B, scratch)
    return _attn_call(functools.partial(_attn_b_kernel, t=t, tp=tp, has_past=bool(past)),
                      q3, H_A, kv, [(frow, fspec)], H_B)


def _band_table(rel_bias, r):
    w = C_PAST + r
    lh = w + r - 1
    dist = C_PAST + r - 1 - jnp.arange(lh)
    hvec = rel_bias.astype(F32)[:, jnp.clip(dist, -REL_CLIP, REL_CLIP) + REL_CLIP]
    q = lh + 1
    hq = jnp.pad(hvec, ((0, 0), (0, 1)))
    skew = jnp.tile(hq, (1, r + 1))[:, :r * (q + 1)].reshape(H_C, r, q + 1)[:, :, :w]
    bias = skew[:, ::-1, :]
    qi = jnp.arange(r)[:, None] // CHUNK
    kj = jnp.arange(w)[None, :] // CHUNK
    ok = (kj >= qi) & (kj <= qi + C_PREV)
    return jnp.where(ok[None], bias * LOG2E, NEG_INF)


def _attn_c_kernel(*refs, r, has_past):
    if has_past:
        q_ref, k_ref, v_ref, kp_ref, vp_ref, tab_ref, o_ref = refs
    else:
        q_ref, k_ref, v_ref, tab_ref, o_ref = refs
    w = C_PAST + r

    def finish(pieces, rows):
        m = functools.reduce(jnp.maximum, [jnp.max(s, axis=1, keepdims=True) for s, _ in pieces])
        ps = [jnp.exp2(s - m) for s, _ in pieces]
        l = functools.reduce(jnp.add, [jnp.sum(p, axis=1, keepdims=True) for p in ps])
        o = functools.reduce(jnp.add, [_dot(p.astype(BF16), v) for p, (_, v) in zip(ps, pieces)])
        o_ref[rows, :] = (o / l).astype(o_ref.dtype)

    if has_past:
        q = q_ref[...]
        s_p = _nt_dot(q, kp_ref[...].astype(BF16)) + tab_ref[:, 0:C_PAST]
        s_o = _nt_dot(q, k_ref[...].astype(BF16)) + tab_ref[:, C_PAST:w]
        finish([(s_p, vp_ref[...].astype(BF16)), (s_o, v_ref[...].astype(BF16))], pl.ds(0, r))
        return

    def tile(i, start, width, tcol0):
        rows = pl.ds(_aligned(i * r, r), r)
        q = q_ref[rows, :]
        kb = k_ref[pl.ds(start, width), :].astype(BF16)
        vb = v_ref[pl.ds(start, width), :].astype(BF16)
        finish([(_nt_dot(q, kb) + tab_ref[:, tcol0:tcol0 + width], vb)], rows)

    n_tiles = q_ref.shape[0] // r
    n_short = min(C_PAST // r, n_tiles)
    for t in range(n_short):
        tile(t, 0, (t + 1) * r, C_PAST - t * r)

    def full_tile(i, carry):
        tile(i, pl.multiple_of(i * r - C_PAST, SUBLANES), w, 0)
        return carry

    lax.fori_loop(n_short, n_tiles, full_tile, 0, unroll=2)


def _attn_c(q3, own, past, table, r):
    assert C_PAST % r == 0
    kv = list(own) + (list(past) if past else [])
    tspec = pl.BlockSpec((None, r, C_PAST + r), lambda bi, h: (h, 0, 0))
    return _attn_call(functools.partial(_attn_c_kernel, r=r, has_past=bool(past)),
                      q3, H_A + H_B, kv, [(table, tspec)], H_C)


def _outproj_kernel(x_ref, oa_ref, ob_ref, oc_ref, wo_ref, g_ref, wr_ref, br_ref,
                    h_ref, hn_ref, route_ref):
    a0, a1 = H_A * DH, (H_A + H_B) * DH
    acc = _dot(oa_ref[...], wo_ref[0:a0, :])
    acc = acc + _dot(ob_ref[...], wo_ref[a0:a1, :])
    acc = acc + _dot(oc_ref[...], wo_ref[a1:MIX, :])
    h = x_ref[...] + acc
    h_ref[...] = h
    ms = jnp.mean(h * h, axis=-1, keepdims=True)
    hn = h * lax.rsqrt(ms + EPS) * g_ref[...]
    hn_ref[...] = hn

    h1, h2, _ = _split3(hn)
    w1 = wr_ref[0]
    w2 = wr_ref[1]
    lg = _dot(h2, w1) + _dot(h1, w2) + _dot(h1, w1) + br_ref[...]

    lane = lax.broadcasted_iota(jnp.int32, lg.shape, 1)
    lane_f = lane.astype(F32)
    is_g = lane < N_GROUPS
    lgm = jnp.where(is_g, lg, NEG_INF)
    gmax = jnp.max(lgm, axis=1, keepdims=True)
    gidx = jnp.min(jnp.where(lgm == gmax, lane_f, float(LANES)), axis=1, keepdims=True)
    pg = 1.0 / jnp.sum(jnp.where(is_g, jnp.exp(lgm - gmax), 0.0), axis=1, keepdims=True)
    grp = jnp.floor((lane_f - N_GROUPS) * (1.0 / EXP_PER_GROUP))
    in_grp = (lane >= N_GROUPS) & (lane < N_GROUPS + N_EXPERTS) & (grp == gidx)
    le = jnp.where(in_grp, lg, NEG_INF)
    t1 = jnp.max(le, axis=1, keepdims=True)
    i1 = jnp.min(jnp.where(le == t1, lane_f, float(LANES)), axis=1, keepdims=True)
    le2 = jnp.where(lane_f == i1, NEG_INF, le)
    t2 = jnp.max(le2, axis=1, keepdims=True)
    i2 = jnp.min(jnp.where(le2 == t2, lane_f, float(LANES)), axis=1, keepdims=True)
    e = jnp.exp(t2 - t1)
    g1 = pg / (1.0 + e)
    g2 = pg * e / (1.0 + e)
    route = jnp.where(lane == 0, i1 - N_GROUPS,
                      jnp.where(lane == 1, i2 - N_GROUPS,
                                jnp.where(lane == 2, g1, jnp.where(lane == 3, g2, 0.0))))
    route_ref[...] = route


def _outproj(x2d, x_block_off, oa, ob, oc, wo_bf, g, wr2, br, n_total, out_block_off, prev):
    n_rows, d = oa.shape[0], x2d.shape[1]
    tm = PROJ_TM
    nb = n_rows // tm
    in_specs = [
        pl.BlockSpec((tm, d), lambda i: (i + x_block_off, 0)),
        pl.BlockSpec((tm, H_A * DH), lambda i: (i, 0)),
        pl.BlockSpec((tm, H_B * DH), lambda i: (i, 0)),
        pl.BlockSpec((tm, H_C * DH), lambda i: (i, 0)),
        pl.BlockSpec((MIX, d), lambda i: (0, 0)),
        pl.BlockSpec((1, d), lambda i: (0, 0)),
        pl.BlockSpec((2, d, LANES), lambda i: (0, 0, 0)),
        pl.BlockSpec((1, LANES), lambda i: (0, 0)),
    ]
    args = [x2d, oa, ob, oc, wo_bf, g, wr2, br]
    aliases = {}
    if prev is not None:
        for t, arr in enumerate(prev):
            in_specs.append(pl.BlockSpec(memory_space=pl.ANY))
            args.append(arr)
            aliases[8 + t] = t

    def body(*refs):
        _outproj_kernel(*refs[:8], *refs[len(args):len(args) + 3])

    return pl.pallas_call(
        body,
        grid=(nb,),
        in_specs=in_specs,
        out_specs=[pl.BlockSpec((tm, d), lambda i: (i + out_block_off, 0)),
                   pl.BlockSpec((tm, d), lambda i: (i + out_block_off, 0)),
                   pl.BlockSpec((tm, LANES), lambda i: (i + out_block_off, 0))],
        out_shape=[jax.ShapeDtypeStruct((n_total, d), F32),
                   jax.ShapeDtypeStruct((n_total, d), F32),
                   jax.ShapeDtypeStruct((n_total, LANES), F32)],
        input_output_aliases=aliases,
        compiler_params=pltpu.CompilerParams(
            dimension_semantics=("parallel",), vmem_limit_bytes=VMEM_LIMIT),
    )(*args)


def _moe_kernel(be_ref, src_ref, srcn_ref, hn_hbm, wg_ref, wu_ref, wd_ref, yb_ref,
                xbuf, xb_ref, wg16, wu16, wd16, sem):
    i = pl.program_id(0)
    last = pl.num_programs(0) - 1
    blk = xbuf.shape[1]
    slot = i % 2

    def gather(idx_ref, dst_slot, lo=0, hi=None):
        for r in range(lo, blk if hi is None else hi):
            pltpu.make_async_copy(hn_hbm.at[pl.ds(idx_ref[0, 0, r], 1), :],
                                  xbuf.at[dst_slot, pl.ds(r, 1), :], sem.at[dst_slot]).start()

    def wait_rows(s):
        pltpu.make_async_copy(hn_hbm.at[pl.ds(0, blk), :], xbuf.at[s], sem.at[s]).wait()

    @pl.when(i == 0)
    def _():
        gather(src_ref, 0)

    @pl.when(jnp.logical_or(i == 0, be_ref[i] != be_ref[jnp.maximum(i - 1, 0)]))
    def _():
        wg16[...] = wg_ref[...].astype(BF16)
        wu16[...] = wu_ref[...].astype(BF16)
        wd16[...] = wd_ref[...].astype(BF16)

    wait_rows(slot)
    d = xbuf.shape[2]
    de = wg16.shape[1]
    n_grp = 8
    per = blk // n_grp
    groups = iter(range(n_grp))

    def next_rows():
        g = next(groups)
        gather(srcn_ref, 1 - slot, g * per, (g + 1) * per)

    xb_ref[...] = xbuf[slot].astype(BF16)
    half = de // 2
    acts = []
    for c in range(2):
        cols = slice(c * half, (c + 1) * half)
        gate = _dot(xb_ref[...], wg16[:, cols])
        next_rows()
        up = _dot(xb_ref[...], wu16[:, cols])
        next_rows()
        acts.append((gate * jax.nn.sigmoid(gate) * up).astype(BF16))
    act = jnp.concatenate(acts, axis=1)
    quarter = d // 4
    for c in range(4):
        cols = slice(c * quarter, (c + 1) * quarter)
        yb_ref[:, cols] = _dot(act, wd16[:, cols])
        next_rows()

    @pl.when(i == last)
    def _():
        wait_rows(1 - slot)


def _moe(hn, src, block_e, wg, wu, wd, layer, n_blocks):
    d = hn.shape[1]
    de = wg.shape[3]
    src3 = src.reshape(n_blocks, 1, MOE_BLK)
    last = n_blocks - 1
    grid_spec = pltpu.PrefetchScalarGridSpec(
        num_scalar_prefetch=1,
        grid=(n_blocks,),
        in_specs=[
            pl.BlockSpec((1, 1, MOE_BLK), lambda i, be: (i, 0, 0), memory_space=pltpu.SMEM),
            pl.BlockSpec((1, 1, MOE_BLK), lambda i, be: (jnp.minimum(i + 1, last), 0, 0),
                         memory_space=pltpu.SMEM),
            pl.BlockSpec(memory_space=pl.ANY),
            pl.BlockSpec((None, None, d, de), lambda i, be: (layer, be[i], 0, 0)),
            pl.BlockSpec((None, None, d, de), lambda i, be: (layer, be[i], 0, 0)),
            pl.BlockSpec((None, None, de, d), lambda i, be: (layer, be[i], 0, 0)),
        ],
        out_specs=pl.BlockSpec((MOE_BLK, d), lambda i, be: (i, 0)),
        scratch_shapes=[pltpu.VMEM((2, MOE_BLK, d), F32), pltpu.VMEM((MOE_BLK, d), BF16),
                        pltpu.VMEM((d, de), BF16), pltpu.VMEM((d, de), BF16), pltpu.VMEM((de, d), BF16),
                        pltpu.SemaphoreType.DMA((2,))],
    )
    return pl.pallas_call(
        _moe_kernel,
        grid_spec=grid_spec,
        out_shape=jax.ShapeDtypeStruct((n_blocks * MOE_BLK, d), F32),
        compiler_params=pltpu.CompilerParams(
            dimension_semantics=("arbitrary",), vmem_limit_bytes=VMEM_LIMIT),
    )(block_e, src3, src3, hn, wg, wu, wd)


def _dispatch(eid, n_blocks):
    p = eid.shape[0] * eid.shape[1]
    e = eid.reshape(-1)
    onehot = (e[:, None] == jnp.arange(N_EXPERTS, dtype=jnp.int32)[None, :]).astype(jnp.int32)
    cs = jnp.cumsum(onehot, axis=0)
    rank = jnp.sum(cs * onehot, axis=1) - 1
    counts = cs[-1]
    padded = (counts + MOE_BLK - 1) // MOE_BLK * MOE_BLK
    pend = jnp.cumsum(padded)
    pstart = pend - padded
    dest = (jnp.sum(onehot * pstart[None, :], axis=1) + rank).astype(jnp.int32)
    src = jnp.zeros((n_blocks * MOE_BLK,), jnp.int32).at[dest].set(
        jnp.arange(p, dtype=jnp.int32) // eid.shape[1])
    blk_start = jnp.arange(n_blocks, dtype=jnp.int32) * MOE_BLK
    block_e = jnp.minimum(jnp.sum((pend[None, :] <= blk_start[:, None]).astype(jnp.int32), axis=1),
                          N_EXPERTS - 1).astype(jnp.int32)
    return dest, src, block_e


def _combine_kernel(pos_ref, h_ref, route_ref, yb_hbm, g_ref, o_ref, buf, sem, *, final):
    tm = h_ref.shape[0]

    def group(gi, carry):
        base = pl.multiple_of(gi * SUBLANES, SUBLANES)
        for u in range(SUBLANES):
            for s in range(2):
                pltpu.make_async_copy(yb_hbm.at[pl.ds(pos_ref[0, 0, 2 * (base + u) + s], 1), :],
                                      buf.at[s, pl.ds(base + u, 1), :], sem).start()
        return carry

    lax.fori_loop(0, tm // SUBLANES, group, 0)
    pltpu.make_async_copy(yb_hbm.at[pl.ds(0, tm), :], buf.at[0], sem).wait()
    pltpu.make_async_copy(yb_hbm.at[pl.ds(0, tm), :], buf.at[1], sem).wait()
    route = route_ref[...]
    g1 = _lane_pick(route, 2)
    g2 = _lane_pick(route, 3)
    y = h_ref[...] + (g1 * buf[0] + g2 * buf[1])
    if final:
        ms = jnp.mean(y * y, axis=-1, keepdims=True)
        y = y * lax.rsqrt(ms + EPS) * g_ref[...]
    o_ref[...] = y


def _combine(pos3, h, route, yb, g, block_off, n_rows, final):
    d = h.shape[1]
    tm = COMBINE_TM
    return pl.pallas_call(
        functools.partial(_combine_kernel, final=final),
        grid=(n_rows // tm,),
        in_specs=[
            pl.BlockSpec((1, 1, 2 * tm), lambda i: (i + block_off, 0, 0), memory_space=pltpu.SMEM),
            pl.BlockSpec((tm, d), lambda i: (i + block_off, 0)),
            pl.BlockSpec((tm, LANES), lambda i: (i + block_off, 0)),
            pl.BlockSpec(memory_space=pl.ANY),
            pl.BlockSpec((1, d), lambda i: (0, 0)),
        ],
        out_specs=pl.BlockSpec((tm, d), lambda i: (i, 0)),
        out_shape=jax.ShapeDtypeStruct((n_rows, d), F32),
        scratch_shapes=[pltpu.VMEM((2, tm, d), F32), pltpu.SemaphoreType.DMA(())],
        compiler_params=pltpu.CompilerParams(
            dimension_semantics=("arbitrary",), vmem_limit_bytes=VMEM_LIMIT),
    )(pos3, h, route, yb, g)


def _pad_lanes(a):
    return jnp.pad(a, [(0, 0)] * (a.ndim - 1) + [(0, LANES - a.shape[-1])])


def kernel(x_prompt, x_sample, cache_a_k, cache_a_v, cache_b_k, cache_b_v, cache_b_logf,
           cache_c_k, cache_c_v, norm1_g, norm2_g, w_in, b_f, rel_bias, w_o,
           w_rg, b_rg, w_re, b_re, w_gate, w_up, w_down, final_g):
    bp, sp, d = x_prompt.shape
    bs, ss, _ = x_sample.shape
    depth = w_in.shape[0]
    past = cache_a_k.shape[2]
    keep = cache_c_k.shape[2]
    n_p, n_s = bp * sp, bs * ss
    n_tot = n_p + n_s
    assert d == MIX and n_p % PROJ_TM == 0 and n_s % PROJ_TM == 0
    assert ss == CHUNK and keep == C_PAST

    t_a = min(256, sp)
    t_b = min(512, sp)
    r_p = min(256, sp)
    tp_a = min(256, past)
    tp_b = min(512, past)
    assert past % tp_a == 0 and past % tp_b == 0
    n_blocks = (2 * n_tot) // MOE_BLK + N_EXPERTS

    cb_k, cb_v = jnp.swapaxes(cache_b_k, 2, 3), jnp.swapaxes(cache_b_v, 2, 3)
    cc_k, cc_v = jnp.swapaxes(cache_c_k, 2, 3), jnp.swapaxes(cache_c_v, 2, 3)
    ca_k = cache_a_k.reshape(depth, bs, past * H_A, DH)
    ca_v = cache_a_v.reshape(depth, bs, past * H_A, DH)

    xs_p = (x_prompt.reshape(n_p, d), 0)
    xs_s = (x_sample.reshape(n_s, d), 0)
    stacks_p = stacks_s = None
    logf_p, logf_s = [], []
    y_prompt = y_sample = None

    for l in range(depth):
        w_bf = w_in[l, :, :3 * MIX].astype(BF16)
        wf_bf = _pad_lanes(w_in[l, :, 3 * MIX:]).astype(BF16)
        bfp = _pad_lanes(b_f[l][None, :].astype(F32))
        g1 = norm1_g[l][None, :].astype(F32)
        q_p, stacks_p, k16_p, v16_p, lf_p = _inproj(xs_p[0], xs_p[1] * (PROJ_TM // INPROJ_TM), bp, sp, g1,
                                                    w_bf, wf_bf, bfp, l, depth, stacks_p)
        q_s, stacks_s, _, _, lf_s = _inproj(xs_s[0], xs_s[1] * (PROJ_TM // INPROJ_TM), bs, ss, g1,
                                            w_bf, wf_bf, bfp, l, depth, stacks_s)
        logf_p.append(lf_p[:, :H_B].reshape(bp, sp, H_B))
        logf_s.append(lf_s[:, :H_B].reshape(bs, ss, H_B))
        q_p3 = q_p.reshape(bp, sp, MIX)
        q_s3 = q_s.reshape(bs, ss, MIX)
        k16_p, v16_p = k16_p[None], v16_p[None]
        ka_s = stacks_s["ka"].reshape(depth, bs, ss * H_A, DH)
        va_s = stacks_s["va"].reshape(depth, bs, ss * H_A, DH)

        oa_p = _attn_a(q_p3, [(k16_p, 0, 0), (v16_p, 0, 0)], None, t_a, t_a)
        oa_s = _attn_a(q_s3, [(ka_s, l, 0), (va_s, l, 0)], [(ca_k, l, 0), (ca_v, l, 0)], ss, tp_a)

        def row_layout(fcol):
            return (jnp.swapaxes(fcol[:, :, :H_B], 1, 2) * LOG2E)[:, :, None, :]

        fcol_p = _cumsum_seq(lf_p.reshape(bp, sp, LANES), min(512, sp))
        lf_cat = jnp.concatenate([_pad_lanes(cache_b_logf[l].astype(F32)),
                                  lf_s.reshape(bs, ss, LANES)], axis=1)
        t_cat = past + ss
        tc_s = t_cat // 3 if (t_cat % 24 == 0) else t_cat
        fcol_s = _cumsum_seq(lf_cat, tc_s)
        ob_p = _attn_b(q_p3, [(k16_p, 0, H_A), (v16_p, 0, H_A)], None, row_layout(fcol_p), t_b, t_b)
        ob_s = _attn_b(q_s3, [(stacks_s["kb"], l, 0), (stacks_s["vb"], l, 0)],
                       [(cb_k, l, 0), (cb_v, l, 0)], row_layout(fcol_s), ss, tp_b)

        oc_p = _attn_c(q_p3, [(k16_p, 0, H_A + H_B), (v16_p, 0, H_A + H_B)], None,
                       _band_table(rel_bias[l], r_p), r_p)
        oc_s = _attn_c(q_s3, [(stacks_s["kc"], l, 0), (stacks_s["vc"], l, 0)],
                       [(cc_k, l, 0), (cc_v, l, 0)], _band_table(rel_bias[l], ss), ss)

        wo_bf = w_o[l].astype(BF16)
        g2 = norm2_g[l][None, :].astype(F32)
        wr = _pad_lanes(jnp.concatenate([w_rg[l], w_re[l]], axis=1).astype(F32))
        wr1 = wr.astype(BF16)
        wr2 = jnp.stack([wr1, (wr - wr1.astype(F32)).astype(BF16)])
        br = _pad_lanes(jnp.concatenate([b_rg[l], b_re[l]])[None, :].astype(F32))
        shared = _outproj(xs_p[0], xs_p[1], oa_p.reshape(n_p, -1), ob_p.reshape(n_p, -1),
                          oc_p.reshape(n_p, -1), wo_bf, g2, wr2, br, n_tot, 0, None)
        h, hn, route = _outproj(xs_s[0], xs_s[1], oa_s.reshape(n_s, -1), ob_s.reshape(n_s, -1),
                                oc_s.reshape(n_s, -1), wo_bf, g2, wr2, br, n_tot,
                                n_p // PROJ_TM, shared)

        eid = route[:, :2].astype(jnp.int32)
        dest, src, block_e = _dispatch(eid, n_blocks)
        yb = _moe(hn, src, block_e, w_gate, w_up, w_down, l, n_blocks)
        pos3 = dest.reshape(n_tot // COMBINE_TM, 1, 2 * COMBINE_TM)
        fg = final_g[None, :].astype(F32)
        if l + 1 < depth:
            y = _combine(pos3, h, route, yb, fg, 0, n_tot, False)
            xs_p = (y, 0)
            xs_s = (y, n_p // PROJ_TM)
        else:
            y_prompt = _combine(pos3, h, route, yb, fg, 0, n_p, True).reshape(bp, sp, d)
            y_sample = _combine(pos3, h, route, yb, fg, n_p // COMBINE_TM, n_s, True).reshape(bs, ss, d)

    def tok_state(stacks, nm, b, s):
        return stacks[nm]

    def hm_state(stacks, nm):
        return jnp.swapaxes(stacks[nm], 2, 3)

    keep_p = min(C_PAST, sp)
    return (y_prompt, y_sample,
            tok_state(stacks_p, "ka", bp, sp), tok_state(stacks_p, "va", bp, sp),
            hm_state(stacks_p, "kb"), hm_state(stacks_p, "vb"),
            jnp.stack(logf_p, axis=0),
            hm_state(stacks_p, "kc")[:, :, sp - keep_p:], hm_state(stacks_p, "vc")[:, :, sp - keep_p:],
            tok_state(stacks_s, "ka", bs, ss), tok_state(stacks_s, "va", bs, ss),
            hm_state(stacks_s, "kb"), hm_state(stacks_s, "vb"),
            jnp.stack(logf_s, axis=0),
            hm_state(stacks_s, "kc"), hm_state(stacks_s, "vc"))
```

```python
import functools
import math

import jax
import jax.numpy as jnp
from jax import lax
from jax.experimental import pallas as pl
from jax.experimental.pallas import tpu as pltpu

F32 = jnp.float32
BF16 = jnp.bfloat16

DH = 128
H_A, H_B, H_C = 4, 6, 6
N_HEADS = H_A + H_B + H_C
MIX = N_HEADS * DH
CHUNK = 64
C_PREV = 8
C_PAST = C_PREV * CHUNK
REL_CLIP = 128
N_GROUPS = 4
EXP_PER_GROUP = 8
N_EXPERTS = N_GROUPS * EXP_PER_GROUP
EPS = 1e-6
NEG_INF = -1e30
SCALE = DH ** -0.5
LOG2E = math.log2(math.e)

LANES = 128
SUBLANES = 8
BF16_ROWS = 16
INPROJ_TM = 256
PROJ_TM = 512
PROJ_TN = 512
MOE_BLK = 256
COMBINE_TM = 256
SB_CUTOFF = -104.0
VMEM_LIMIT = 56 * 1024 * 1024


def _nt_dot(a, b):
    return lax.dot_general(a, b, (((1,), (1,)), ((), ())), preferred_element_type=F32)


def _dot(a, b):
    return jnp.dot(a, b, preferred_element_type=F32)


def _split3(x):
    x1 = x.astype(BF16)
    r1 = x - x1.astype(F32)
    x2 = r1.astype(BF16)
    x3 = (r1 - x2.astype(F32)).astype(BF16)
    return x1, x2, x3


def _aligned(x, m):
    return x if isinstance(x, int) else pl.multiple_of(x, m)


def _lane_pick(x, idx):
    lane = lax.broadcasted_iota(jnp.int32, x.shape, 1)
    return jnp.sum(jnp.where(lane == idx, x, 0.0), axis=1, keepdims=True)


def _inproj_kernel(x_ref, g_ref, w_hbm, wf_ref, bf_ref,
                   q_ref, ka_ref, va_ref, kb_ref, vb_ref, kc_ref, vc_ref, k16_ref, v16_ref, lf_ref,
                   w_vmem, xn_ref, sem, *, rb):
    tm = x_ref.shape[0]
    nbb = tm // rb
    heads_per_chunk = PROJ_TN // DH
    chunks_per_part = MIX // PROJ_TN

    @pl.when(pl.program_id(0) == 0)
    def _():
        cp = pltpu.make_async_copy(w_hbm, w_vmem, sem)
        cp.start()
        cp.wait()

    x = x_ref[...]
    ms = jnp.mean(x * x, axis=-1, keepdims=True)
    xn_ref[...] = (x * lax.rsqrt(ms + EPS) * g_ref[...]).astype(BF16)
    lf_ref[...] = jax.nn.log_sigmoid(_dot(xn_ref[...], wf_ref[...]) + bf_ref[...])

    def write_kv(jj, acc, tok_ref, b_ref, c_ref, h16_ref):
        for u in range(heads_per_chunk):
            hh = (jj * heads_per_chunk + u) % N_HEADS
            piece = acc[:, u * DH:(u + 1) * DH]
            for bb in range(nbb):
                rows = piece[bb * rb:(bb + 1) * rb]
                h16_ref[bb, hh] = rows.astype(BF16)
                if hh < H_A:
                    tok_ref[bb, :, hh, :] = rows
                elif hh < H_A + H_B:
                    b_ref[bb, hh - H_A] = rows
                elif hh >= H_A + H_B:
                    c_ref[bb, hh - H_A - H_B] = rows

    for jj in range(3 * chunks_per_part):
        acc = _dot(xn_ref[...], w_vmem[:, jj * PROJ_TN:(jj + 1) * PROJ_TN])
        part = jj // chunks_per_part
        if part == 0:
            sc = SCALE if (jj + 1) * heads_per_chunk <= H_A else SCALE * LOG2E
            q_ref[:, jj * PROJ_TN:(jj + 1) * PROJ_TN] = (acc * sc).astype(BF16)
        elif part == 1:
            write_kv(jj, acc, ka_ref, kb_ref, kc_ref, k16_ref)
        else:
            write_kv(jj, acc, va_ref, vb_ref, vc_ref, v16_ref)


_KV_NAMES = ["ka", "va", "kb", "vb", "kc", "vc"]


def _inproj(x2d, row_block_off, nb_batch, seq, g, w_bf, wf_bf, bfp, layer, depth, prev):
    assert H_A * DH == PROJ_TN
    d = x2d.shape[1]
    tm = INPROJ_TM
    n_rows = nb_batch * seq
    nb = n_rows // tm
    rb = min(seq, tm)
    nbb = tm // rb
    spb = seq // rb

    def hm_map(i):
        return (layer, i // spb, 0, i % spb, 0)

    def hm16_map(i):
        return (i // spb, 0, i % spb, 0)

    out_shape = [jax.ShapeDtypeStruct((n_rows, MIX), BF16)]
    out_specs = [pl.BlockSpec((tm, MIX), lambda i: (i, 0))]
    for nm in _KV_NAMES:
        if nm in ("ka", "va"):
            out_shape.append(jax.ShapeDtypeStruct((depth, nb_batch, seq, H_A, DH), F32))
            out_specs.append(pl.BlockSpec((None, nbb, rb, H_A, DH),
                                          lambda i: (layer, i // spb, i % spb, 0, 0)))
        else:
            hn = H_B if nm[1] == "b" else H_C
            out_shape.append(jax.ShapeDtypeStruct((depth, nb_batch, hn, seq, DH), F32))
            out_specs.append(pl.BlockSpec((None, nbb, hn, rb, DH), hm_map))
    for _ in range(2):
        out_shape.append(jax.ShapeDtypeStruct((nb_batch, N_HEADS, seq, DH), BF16))
        out_specs.append(pl.BlockSpec((nbb, N_HEADS, rb, DH), hm16_map))
    out_shape.append(jax.ShapeDtypeStruct((n_rows, LANES), F32))
    out_specs.append(pl.BlockSpec((tm, LANES), lambda i: (i, 0)))
    n_out = len(out_shape)

    in_specs = [
        pl.BlockSpec((tm, d), lambda i: (i + row_block_off, 0)),
        pl.BlockSpec((1, d), lambda i: (0, 0)),
        pl.BlockSpec(memory_space=pl.ANY),
        pl.BlockSpec((d, LANES), lambda i: (0, 0)),
        pl.BlockSpec((1, LANES), lambda i: (0, 0)),
    ]
    args = [x2d, g, w_bf, wf_bf, bfp]
    n_in = len(args)
    aliases = {}
    if prev is not None:
        for t, nm in enumerate(_KV_NAMES):
            in_specs.append(pl.BlockSpec(memory_space=pl.ANY))
            args.append(prev[nm])
            aliases[n_in + t] = 1 + t

    def body(*refs):
        _inproj_kernel(*refs[:n_in], *refs[len(args):len(args) + n_out], *refs[len(args) + n_out:], rb=rb)

    res = pl.pallas_call(
        body,
        grid=(nb,),
        in_specs=in_specs,
        out_specs=out_specs,
        out_shape=out_shape,
        scratch_shapes=[pltpu.VMEM(w_bf.shape, BF16), pltpu.VMEM((tm, d), BF16),
                        pltpu.SemaphoreType.DMA(())],
        input_output_aliases=aliases,
        compiler_params=pltpu.CompilerParams(
            dimension_semantics=("arbitrary",), vmem_limit_bytes=VMEM_LIMIT),
    )(*args)
    stacks = dict(zip(_KV_NAMES, res[1:7]))
    return res[0], stacks, res[7], res[8], res[9]


def _cumsum_kernel(x_ref, f_ref, carry_ref, *, tc):
    @pl.when(pl.program_id(1) == 0)
    def _():
        carry_ref[...] = jnp.zeros_like(carry_ref)

    x = x_ref[...]
    row = lax.broadcasted_iota(jnp.int32, (tc, tc), 0)
    col = lax.broadcasted_iota(jnp.int32, (tc, tc), 1)
    lower = jnp.where(row >= col, 1.0, 0.0).astype(BF16)
    x1, x2, x3 = _split3(x)
    cs = _dot(lower, x3) + _dot(lower, x2) + _dot(lower, x1) + carry_ref[...]
    f_ref[...] = cs
    carry_ref[...] = cs[tc - 1:tc, :]


def _cumsum_seq(x, tc):
    b, t, _ = x.shape
    return pl.pallas_call(
        functools.partial(_cumsum_kernel, tc=tc),
        grid=(b, t // tc),
        in_specs=[pl.BlockSpec((None, tc, LANES), lambda i, j: (i, j, 0))],
        out_specs=pl.BlockSpec((None, tc, LANES), lambda i, j: (i, j, 0)),
        out_shape=jax.ShapeDtypeStruct((b, t, LANES), F32),
        scratch_shapes=[pltpu.VMEM((1, LANES), F32)],
        compiler_params=pltpu.CompilerParams(
            dimension_semantics=("parallel", "arbitrary")),
    )(x)


def _kv_spec(arr, layer, head_off):
    if arr.ndim == 5:
        return pl.BlockSpec((None, None, None, arr.shape[3], DH),
                            lambda bi, h: (layer, bi, head_off + h, 0, 0))
    return pl.BlockSpec((None, None, arr.shape[2], DH), lambda bi, h: (layer, bi, 0, 0))


def _attn_call(body, q3, q_head_off, kv, extra, n_heads, scratch=()):
    b, tq, _ = q3.shape
    in_specs = [pl.BlockSpec((None, tq, DH), lambda bi, h: (bi, 0, q_head_off + h))]
    args = [q3]
    for arr, layer, head_off in kv:
        in_specs.append(_kv_spec(arr, layer, head_off))
        args.append(arr)
    for arr, spec in extra:
        in_specs.append(spec)
        args.append(arr)
    return pl.pallas_call(
        body,
        grid=(b, n_heads),
        in_specs=in_specs,
        out_specs=pl.BlockSpec((None, tq, DH), lambda bi, h: (bi, 0, h)),
        out_shape=jax.ShapeDtypeStruct((b, tq, n_heads * DH), BF16),
        scratch_shapes=list(scratch),
        compiler_params=pltpu.CompilerParams(
            dimension_semantics=("parallel", "parallel"), vmem_limit_bytes=VMEM_LIMIT),
    )(*args)


def _attn_a_kernel(*refs, t, tp, has_past, interleaved):
    if has_past:
        q_ref, k_ref, v_ref, kp_ref, vp_ref, o_ref = refs
    else:
        q_ref, k_ref, v_ref, o_ref = refs
    n_q = q_ref.shape[0] // t

    def later_matrix(n):
        row = lax.broadcasted_iota(jnp.int32, (n, n), 0)
        col = lax.broadcasted_iota(jnp.int32, (n, n), 1)
        return jnp.where(row > col, 1.0, 0.0).astype(BF16)

    def cond(carry):
        j, c, _ = carry
        return jnp.logical_and(j >= 0, jnp.max(c) > SB_CUTOFF)

    def block(q, kr, vr, tb, j, c, acc, diag, valid=None):
        start = _aligned(j * tb, tb)
        if interleaved:
            keys = pl.ds(start * H_A + pl.program_id(1), tb, stride=H_A)
        else:
            keys = pl.ds(start, tb)
        kb = kr[keys, :].astype(BF16)
        vb = vr[keys, :].astype(BF16)
        later = later_matrix(tb)
        z = _nt_dot(q, kb)
        sp = jnp.maximum(z, 0.0) + jnp.log1p(jnp.exp(-jnp.abs(z)))
        lm = -sp
        if diag:
            row = lax.broadcasted_iota(jnp.int32, (t, tb), 0)
            col = lax.broadcasted_iota(jnp.int32, (t, tb), 1)
            mask = col < row
            lm = jnp.where(mask, lm, 0.0)
        if valid is not None:
            lm = lm * valid
        hi = lm.astype(BF16)
        lo = (lm - hi.astype(F32)).astype(BF16)
        suffix = _dot(lo, later) + _dot(hi, later)
        w = jnp.exp(z - sp + suffix + c)
        if diag:
            w = jnp.where(mask, w, 0.0)
        if valid is not None:
            w = w * valid
        return c + jnp.sum(lm, axis=1, keepdims=True), acc + _dot(w.astype(BF16), vb)

    def walk(q, kr, vr, tb, j0, c, acc):
        def body(carry):
            j, c_in, acc_in = carry
            c_out, acc_out = block(q, kr, vr, tb, j, c_in, acc_in, False)
            return j - 1, c_out, acc_out

        _, c, acc = lax.while_loop(cond, body, (jnp.asarray(j0, jnp.int32), c, acc))
        return c, acc

    def q_block(i, carry):
        rows = pl.ds(_aligned(i * t, t), t)
        q = q_ref[rows, :]
        c, acc = jnp.zeros((t, 1), F32), jnp.zeros((t, DH), F32)
        c, acc = block(q, k_ref, v_ref, t, i, c, acc, True)
        if n_q > 1:
            has_prev = jnp.where(i >= 1, 1.0, 0.0)
            c, acc = block(q, k_ref, v_ref, t, jnp.maximum(i - 1, 0), c, acc, False, has_prev)
            c, acc = walk(q, k_ref, v_ref, t, i - 2, c, acc)
        if has_past:
            n_past = kp_ref.shape[0] // (tp * (H_A if interleaved else 1))
            c, acc = walk(q, kp_ref, vp_ref, tp, n_past - 1, c, acc)
        o_ref[rows, :] = acc.astype(o_ref.dtype)
        return carry

    if n_q == 1:
        q_block(0, 0)
    else:
        lax.fori_loop(0, n_q, q_block, 0)


def _attn_a(q3, own, past, t, tp):
    kv = list(own) + (list(past) if past else [])
    interleaved = own[0][0].ndim == 4
    return _attn_call(functools.partial(_attn_a_kernel, t=t, tp=tp, has_past=bool(past),
                                        interleaved=interleaved),
                      q3, 0, kv, [], H_A)


def _attn_b_kernel(*refs, t, tp, has_past):
    if has_past:
        q_ref, k_ref, v_ref, kp_ref, vp_ref, fk_ref, o_ref = refs
        t_past = kp_ref.shape[0]
    else:
        q_ref, k_ref, v_ref, fk_ref, o_ref = refs
        t_past = 0
    n_q = q_ref.shape[0] // t
    rt = BF16_ROWS

    def update(s, fk, vb, carry, q_minus_k):
        m, l, acc = carry
        tb = s.shape[1]
        if q_minus_k is not None:
            row = lax.broadcasted_iota(jnp.int32, (rt, tb), 0)
            col = lax.broadcasted_iota(jnp.int32, (rt, tb), 1)
        sb = s - fk
        if q_minus_k is not None:
            row_b = lax.broadcasted_iota(jnp.int32, (t, tb), 0)
            col_b = lax.broadcasted_iota(jnp.int32, (t, tb), 1)
            sb = jnp.where(col_b <= row_b + q_minus_k, sb, NEG_INF)
        m_new = jnp.maximum(m, jnp.max(sb, axis=1, keepdims=True))
        alpha = jnp.exp2(m - m_new)
        ps, parts = [], []
        for r in range(t // rt):
            s_t = s[r * rt:(r + 1) * rt] - fk
            if q_minus_k is not None:
                s_t = jnp.where(col <= row + (r * rt + q_minus_k), s_t, NEG_INF)
            p = jnp.exp2(s_t - m_new[r * rt:(r + 1) * rt])
            if tb % LANES == 0:
                parts.append(functools.reduce(
                    jnp.add, [p[:, c * LANES:(c + 1) * LANES] for c in range(tb // LANES)]))
            else:
                lane = lax.broadcasted_iota(jnp.int32, (rt, LANES), 1)
                parts.append(jnp.where(lane == 0, jnp.sum(p, axis=1, keepdims=True), 0.0))
            ps.append(p.astype(BF16))
        l = alpha * l + jnp.concatenate(parts, axis=0)
        acc = alpha * acc + _dot(jnp.concatenate(ps, axis=0), vb)
        return m_new, l, acc

    def q_block(i, carry_unused):
        rows = pl.ds(_aligned(i * t, t), t)
        q = q_ref[rows, :]
        carry = (jnp.full((t, 1), NEG_INF, F32), jnp.zeros((t, LANES), F32), jnp.zeros((t, DH), F32))

        if has_past:
            def past_body(j, c):
                start = pl.multiple_of(j * tp, tp)
                kb = kp_ref[pl.ds(start, tp), :].astype(BF16)
                vb = vp_ref[pl.ds(start, tp), :].astype(BF16)
                return update(_nt_dot(q, kb), fk_ref[:, pl.ds(start, tp)], vb, c, None)

            carry = lax.fori_loop(0, t_past // tp, past_body, carry)

        def own_start(j):
            return _aligned(j * t, t)

        def scores(j):
            return _nt_dot(q, k_ref[pl.ds(own_start(j), t), :].astype(BF16))

        def own_update(s, j, c, diag):
            vb = v_ref[pl.ds(own_start(j), t), :].astype(BF16)
            fk = fk_ref[:, pl.ds(t_past + own_start(j), t)]
            return update(s, fk, vb, c, 0 if diag else None)

        def body(j, c):
            s_cur, inner = c
            s_next = scores(j + 1)
            return s_next, own_update(s_cur, j, inner, False)

        s = scores(0)
        if n_q > 1:
            s, carry = lax.fori_loop(0, i, body, (s, carry))
        _, l, acc = own_update(s, i, carry, True)
        o_ref[rows, :] = (acc / jnp.sum(l, axis=1, keepdims=True)).astype(o_ref.dtype)
        return carry_unused

    if n_q == 1:
        q_block(0, 0)
    else:
        lax.fori_loop(0, n_q, q_block, 0)


def _attn_b_blocks_kernel(q_ref, k_ref, v_ref, fk_ref, o_ref,
                          s0_ref, s1_ref, p_ref, m_ref, l_ref, acc_ref, *, t):
    n_q = q_ref.shape[0] // t
    rt = BF16_ROWS

    def q_block(i, carry):
        rows = pl.ds(pl.multiple_of(i * t, t), t)
        q = q_ref[rows, :]
        m_ref[...] = jnp.full(m_ref.shape, NEG_INF, F32)
        l_ref[...] = jnp.zeros(l_ref.shape, F32)
        acc_ref[...] = jnp.zeros(acc_ref.shape, F32)

        def keys(j):
            return pl.ds(pl.multiple_of(j * t, t), t)

        def put_scores(dst, j):
            dst[...] = _nt_dot(q, k_ref[keys(j), :]) - fk_ref[:, keys(j)]

        def consume(src, j, diag):
            sb = src[...]
            if diag:
                row_b = lax.broadcasted_iota(jnp.int32, (t, t), 0)
                col_b = lax.broadcasted_iota(jnp.int32, (t, t), 1)
                sb = jnp.where(col_b <= row_b, sb, NEG_INF)
                row = lax.broadcasted_iota(jnp.int32, (rt, t), 0)
                col = lax.broadcasted_iota(jnp.int32, (rt, t), 1)
            m_old = m_ref[...]
            m_new = jnp.maximum(m_old, jnp.max(sb, axis=1, keepdims=True))
            alpha = jnp.exp2(m_old - m_new)
            m_ref[...] = m_new
            for r in range(t // rt):
                sl = slice(r * rt, (r + 1) * rt)
                s_t = src[sl, :]
                if diag:
                    s_t = jnp.where(col <= row + r * rt, s_t, NEG_INF)
                p = jnp.exp2(s_t - m_new[sl])
                part = functools.reduce(jnp.add, [p[:, c * LANES:(c + 1) * LANES] for c in range(t // LANES)])
                l_ref[sl, :] = alpha[sl] * l_ref[sl, :] + part
                p_ref[sl, :] = p.astype(BF16)
            acc_ref[...] = alpha * acc_ref[...] + _dot(p_ref[...], v_ref[keys(j), :])

        put_scores(s0_ref, 0)

        def pair(pp, c):
            j = 2 * pp
            put_scores(s1_ref, j + 1)
            consume(s0_ref, j, False)
            put_scores(s0_ref, j + 2)
            consume(s1_ref, j + 1, False)
            return c

        lax.fori_loop(0, i // 2, pair, 0)

        @pl.when(i % 2 == 1)
        def _():
            put_scores(s1_ref, i)
            consume(s0_ref, i - 1, False)
            consume(s1_ref, i, True)

        @pl.when(i % 2 == 0)
        def _():
            consume(s0_ref, i, True)

        o_ref[rows, :] = (acc_ref[...] / jnp.sum(l_ref[...], axis=1, keepdims=True)).astype(o_ref.dtype)
        return carry

    lax.fori_loop(0, n_q, q_block, 0)


def _attn_b(q3, own, past, frow, t, tp):
    kv = list(own) + (list(past) if past else [])
    fspec = pl.BlockSpec((None, None, 1, frow.shape[3]), lambda bi, h: (bi, h, 0, 0))
    if not past and t % LANES == 0 and own[0][0].dtype == BF16:
        scratch = [pltpu.VMEM((t, t), F32), pltpu.VMEM((t, t), F32), pltpu.VMEM((t, t), BF16),
                   pltpu.VMEM((t, 1), F32), pltpu.VMEM((t, LANES), F32), pltpu.VMEM((t, DH), F32)]
        return _attn_call(functools.partial(_attn_b_blocks_kernel, t=t),
                          q3, H_A, kv, [(frow, fspec)], H_B, scratch)
    return _attn_call(functools.partial(_attn_b_kernel, t=t, tp=tp, has_past=bool(past)),
                      q3, H_A, kv, [(frow, fspec)], H_B)


def _band_table(rel_bias, r):
    w = C_PAST + r
    lh = w + r - 1
    dist = C_PAST + r - 1 - jnp.arange(lh)
    hvec = rel_bias.astype(F32)[:, jnp.clip(dist, -REL_CLIP, REL_CLIP) + REL_CLIP]
    q = lh + 1
    hq = jnp.pad(hvec, ((0, 0), (0, 1)))
    skew = jnp.tile(hq, (1, r + 1))[:, :r * (q + 1)].reshape(H_C, r, q + 1)[:, :, :w]
    bias = skew[:, ::-1, :]
    qi = jnp.arange(r)[:, None] // CHUNK
    kj = jnp.arange(w)[None, :] // CHUNK
    ok = (kj >= qi) & (kj <= qi + C_PREV)
    return jnp.where(ok[None], bias * LOG2E, NEG_INF)


def _attn_c_kernel(*refs, r, has_past):
    if has_past:
        q_ref, k_ref, v_ref, kp_ref, vp_ref, tab_ref, o_ref = refs
    else:
        q_ref, k_ref, v_ref, tab_ref, o_ref = refs
    w = C_PAST + r

    def finish(pieces, rows):
        m = functools.reduce(jnp.maximum, [jnp.max(s, axis=1, keepdims=True) for s, _ in pieces])
        ps = [jnp.exp2(s - m) for s, _ in pieces]
        l = functools.reduce(jnp.add, [jnp.sum(p, axis=1, keepdims=True) for p in ps])
        o = functools.reduce(jnp.add, [_dot(p.astype(BF16), v) for p, (_, v) in zip(ps, pieces)])
        o_ref[rows, :] = (o / l).astype(o_ref.dtype)

    if has_past:
        q = q_ref[...]
        s_p = _nt_dot(q, kp_ref[...].astype(BF16)) + tab_ref[:, 0:C_PAST]
        s_o = _nt_dot(q, k_ref[...].astype(BF16)) + tab_ref[:, C_PAST:w]
        finish([(s_p, vp_ref[...].astype(BF16)), (s_o, v_ref[...].astype(BF16))], pl.ds(0, r))
        return

    def tile(i, start, width, tcol0):
        rows = pl.ds(_aligned(i * r, r), r)
        q = q_ref[rows, :]
        kb = k_ref[pl.ds(start, width), :].astype(BF16)
        vb = v_ref[pl.ds(start, width), :].astype(BF16)
        finish([(_nt_dot(q, kb) + tab_ref[:, tcol0:tcol0 + width], vb)], rows)

    n_tiles = q_ref.shape[0] // r
    n_short = min(C_PAST // r, n_tiles)
    for t in range(n_short):
        tile(t, 0, (t + 1) * r, C_PAST - t * r)

    def full_tile(i, carry):
        tile(i, pl.multiple_of(i * r - C_PAST, SUBLANES), w, 0)
        return carry

    lax.fori_loop(n_short, n_tiles, full_tile, 0, unroll=2)


def _attn_c(q3, own, past, table, r):
    assert C_PAST % r == 0
    kv = list(own) + (list(past) if past else [])
    tspec = pl.BlockSpec((None, r, C_PAST + r), lambda bi, h: (h, 0, 0))
    return _attn_call(functools.partial(_attn_c_kernel, r=r, has_past=bool(past)),
                      q3, H_A + H_B, kv, [(table, tspec)], H_C)


def _outproj_kernel(x_ref, oa_ref, ob_ref, oc_ref, wo_ref, g_ref, wr_ref, br_ref, cnt_in_ref,
                    h_ref, hn_ref, route_ref, cnt_out_ref, cnt_ref):
    @pl.when(pl.program_id(0) == 0)
    def _():
        cnt_ref[...] = cnt_in_ref[...]

    a0, a1 = H_A * DH, (H_A + H_B) * DH
    acc = _dot(oa_ref[...], wo_ref[0:a0, :])
    acc = acc + _dot(ob_ref[...], wo_ref[a0:a1, :])
    acc = acc + _dot(oc_ref[...], wo_ref[a1:MIX, :])
    h = x_ref[...] + acc
    h_ref[...] = h
    ms = jnp.mean(h * h, axis=-1, keepdims=True)
    hn = h * lax.rsqrt(ms + EPS) * g_ref[...]
    hn_ref[...] = hn

    h1, h2, _ = _split3(hn)
    w1 = wr_ref[0]
    w2 = wr_ref[1]
    lg = _dot(h2, w1) + _dot(h1, w2) + _dot(h1, w1) + br_ref[...]

    lane = lax.broadcasted_iota(jnp.int32, lg.shape, 1)
    lane_f = lane.astype(F32)
    is_g = lane < N_GROUPS
    lgm = jnp.where(is_g, lg, NEG_INF)
    gmax = jnp.max(lgm, axis=1, keepdims=True)
    gidx = jnp.min(jnp.where(lgm == gmax, lane_f, float(LANES)), axis=1, keepdims=True)
    pg = 1.0 / jnp.sum(jnp.where(is_g, jnp.exp(lgm - gmax), 0.0), axis=1, keepdims=True)
    grp = jnp.floor((lane_f - N_GROUPS) * (1.0 / EXP_PER_GROUP))
    in_grp = (lane >= N_GROUPS) & (lane < N_GROUPS + N_EXPERTS) & (grp == gidx)
    le = jnp.where(in_grp, lg, NEG_INF)
    t1 = jnp.max(le, axis=1, keepdims=True)
    i1 = jnp.min(jnp.where(le == t1, lane_f, float(LANES)), axis=1, keepdims=True)
    le2 = jnp.where(lane_f == i1, NEG_INF, le)
    t2 = jnp.max(le2, axis=1, keepdims=True)
    i2 = jnp.min(jnp.where(le2 == t2, lane_f, float(LANES)), axis=1, keepdims=True)
    e = jnp.exp(t2 - t1)
    g1 = pg / (1.0 + e)
    g2 = pg * e / (1.0 + e)
    e1, e2 = i1 - N_GROUPS, i2 - N_GROUPS
    oh1 = jnp.where(lane_f == e1, 1.0, 0.0)
    oh2 = jnp.where(lane_f == e2, 1.0, 0.0)
    both = oh1 + oh2
    tm = lg.shape[0]
    row = lax.broadcasted_iota(jnp.int32, (tm, tm), 0)
    col = lax.broadcasted_iota(jnp.int32, (tm, tm), 1)
    earlier = jnp.where(row > col, 1.0, 0.0).astype(BF16)
    base = _dot(earlier, both.astype(BF16)) + cnt_ref[...]
    r1 = jnp.sum(oh1 * base, axis=1, keepdims=True)
    r2 = jnp.sum(oh2 * base, axis=1, keepdims=True)
    cnt_ref[...] = cnt_ref[...] + jnp.sum(both, axis=0, keepdims=True)
    cnt_out_ref[...] = cnt_ref[...]

    route = jnp.where(lane == 0, e1, jnp.where(lane == 1, e2, jnp.where(lane == 2, g1, jnp.where(
        lane == 3, g2, jnp.where(lane == 4, r1, jnp.where(lane == 5, r2, 0.0))))))
    route_ref[...] = route


def _outproj(x2d, x_block_off, oa, ob, oc, wo_bf, g, wr2, br, counts, n_total, out_block_off, prev):
    n_rows, d = oa.shape[0], x2d.shape[1]
    tm = PROJ_TM
    nb = n_rows // tm
    in_specs = [
        pl.BlockSpec((tm, d), lambda i: (i + x_block_off, 0)),
        pl.BlockSpec((tm, H_A * DH), lambda i: (i, 0)),
        pl.BlockSpec((tm, H_B * DH), lambda i: (i, 0)),
        pl.BlockSpec((tm, H_C * DH), lambda i: (i, 0)),
        pl.BlockSpec((MIX, d), lambda i: (0, 0)),
        pl.BlockSpec((1, d), lambda i: (0, 0)),
        pl.BlockSpec((2, d, LANES), lambda i: (0, 0, 0)),
        pl.BlockSpec((1, LANES), lambda i: (0, 0)),
        pl.BlockSpec((1, LANES), lambda i: (0, 0)),
    ]
    args = [x2d, oa, ob, oc, wo_bf, g, wr2, br, counts]
    n_in = len(args)
    aliases = {}
    if prev is not None:
        for t, arr in enumerate(prev):
            in_specs.append(pl.BlockSpec(memory_space=pl.ANY))
            args.append(arr)
            aliases[n_in + t] = t

    def body(*refs):
        _outproj_kernel(*refs[:n_in], *refs[len(args):])

    return pl.pallas_call(
        body,
        grid=(nb,),
        in_specs=in_specs,
        out_specs=[pl.BlockSpec((tm, d), lambda i: (i + out_block_off, 0)),
                   pl.BlockSpec((tm, d), lambda i: (i + out_block_off, 0)),
                   pl.BlockSpec((tm, LANES), lambda i: (i + out_block_off, 0)),
                   pl.BlockSpec((1, LANES), lambda i: (0, 0))],
        out_shape=[jax.ShapeDtypeStruct((n_total, d), F32),
                   jax.ShapeDtypeStruct((n_total, d), F32),
                   jax.ShapeDtypeStruct((n_total, LANES), F32),
                   jax.ShapeDtypeStruct((1, LANES), F32)],
        scratch_shapes=[pltpu.VMEM((1, LANES), F32)],
        input_output_aliases=aliases,
        compiler_params=pltpu.CompilerParams(
            dimension_semantics=("arbitrary",), vmem_limit_bytes=VMEM_LIMIT),
    )(*args)


def _moe_kernel(be_ref, src_ref, srcn_ref, hn_hbm, wg_ref, wu_ref, wd_ref, yb_ref,
                xbuf, xb_ref, wg16, wu16, wd16, sem):
    i = pl.program_id(0)
    last = pl.num_programs(0) - 1
    blk = xbuf.shape[1]
    slot = i % 2

    def gather(idx_ref, dst_slot, lo=0, hi=None):
        for r in range(lo, blk if hi is None else hi):
            pltpu.make_async_copy(hn_hbm.at[pl.ds(idx_ref[0, 0, r], 1), :],
                                  xbuf.at[dst_slot, pl.ds(r, 1), :], sem.at[dst_slot]).start()

    def wait_rows(s):
        pltpu.make_async_copy(hn_hbm.at[pl.ds(0, blk), :], xbuf.at[s], sem.at[s]).wait()

    @pl.when(i == 0)
    def _():
        gather(src_ref, 0)

    @pl.when(jnp.logical_or(i == 0, be_ref[i] != be_ref[jnp.maximum(i - 1, 0)]))
    def _():
        wg16[...] = wg_ref[...].astype(BF16)
        wu16[...] = wu_ref[...].astype(BF16)
        wd16[...] = wd_ref[...].astype(BF16)

    wait_rows(slot)
    d = xbuf.shape[2]
    de = wg16.shape[1]
    n_grp = 8
    per = blk // n_grp
    groups = iter(range(n_grp))

    def next_rows():
        g = next(groups)
        gather(srcn_ref, 1 - slot, g * per, (g + 1) * per)

    xb_ref[...] = xbuf[slot].astype(BF16)
    half = de // 2
    acts = []
    for c in range(2):
        cols = slice(c * half, (c + 1) * half)
        gate = _dot(xb_ref[...], wg16[:, cols])
        next_rows()
        up = _dot(xb_ref[...], wu16[:, cols])
        next_rows()
        acts.append((gate * jax.nn.sigmoid(gate) * up).astype(BF16))
    act = jnp.concatenate(acts, axis=1)
    quarter = d // 4
    for c in range(4):
        cols = slice(c * quarter, (c + 1) * quarter)
        yb_ref[:, cols] = _dot(act, wd16[:, cols])
        next_rows()

    @pl.when(i == last)
    def _():
        wait_rows(1 - slot)


def _moe(hn, src, block_e, wg, wu, wd, layer, n_blocks):
    d = hn.shape[1]
    de = wg.shape[3]
    src3 = src.reshape(n_blocks, 1, MOE_BLK)
    last = n_blocks - 1
    grid_spec = pltpu.PrefetchScalarGridSpec(
        num_scalar_prefetch=1,
        grid=(n_blocks,),
        in_specs=[
            pl.BlockSpec((1, 1, MOE_BLK), lambda i, be: (i, 0, 0), memory_space=pltpu.SMEM),
            pl.BlockSpec((1, 1, MOE_BLK), lambda i, be: (jnp.minimum(i + 1, last), 0, 0),
                         memory_space=pltpu.SMEM),
            pl.BlockSpec(memory_space=pl.ANY),
            pl.BlockSpec((None, None, d, de), lambda i, be: (layer, be[i], 0, 0)),
            pl.BlockSpec((None, None, d, de), lambda i, be: (layer, be[i], 0, 0)),
            pl.BlockSpec((None, None, de, d), lambda i, be: (layer, be[i], 0, 0)),
        ],
        out_specs=pl.BlockSpec((MOE_BLK, d), lambda i, be: (i, 0)),
        scratch_shapes=[pltpu.VMEM((2, MOE_BLK, d), F32), pltpu.VMEM((MOE_BLK, d), BF16),
                        pltpu.VMEM((d, de), BF16), pltpu.VMEM((d, de), BF16), pltpu.VMEM((de, d), BF16),
                        pltpu.SemaphoreType.DMA((2,))],
    )
    return pl.pallas_call(
        _moe_kernel,
        grid_spec=grid_spec,
        out_shape=jax.ShapeDtypeStruct((n_blocks * MOE_BLK, d), F32),
        compiler_params=pltpu.CompilerParams(
            dimension_semantics=("arbitrary",), vmem_limit_bytes=VMEM_LIMIT),
    )(block_e, src3, src3, hn, wg, wu, wd)


def _dispatch(eid, rank, counts, n_blocks):
    p = eid.shape[0] * eid.shape[1]
    e = eid.reshape(-1)
    onehot = (e[:, None] == jnp.arange(N_EXPERTS, dtype=jnp.int32)[None, :]).astype(jnp.int32)
    padded = (counts + MOE_BLK - 1) // MOE_BLK * MOE_BLK
    pend = jnp.cumsum(padded)
    pstart = pend - padded
    dest = (jnp.sum(onehot * pstart[None, :], axis=1) + rank.reshape(-1)).astype(jnp.int32)
    src = jnp.zeros((n_blocks * MOE_BLK,), jnp.int32).at[dest].set(
        jnp.arange(p, dtype=jnp.int32) // eid.shape[1])
    blk_start = jnp.arange(n_blocks, dtype=jnp.int32) * MOE_BLK
    block_e = jnp.minimum(jnp.sum((pend[None, :] <= blk_start[:, None]).astype(jnp.int32), axis=1),
                          N_EXPERTS - 1).astype(jnp.int32)
    return dest, src, block_e


def _combine_kernel(pos_ref, h_ref, route_ref, yb_hbm, g_ref, o_ref, buf, sem, *, final):
    tm = h_ref.shape[0]

    def group(gi, carry):
        base = pl.multiple_of(gi * SUBLANES, SUBLANES)
        for u in range(SUBLANES):
            for s in range(2):
                pltpu.make_async_copy(yb_hbm.at[pl.ds(pos_ref[0, 0, 2 * (base + u) + s], 1), :],
                                      buf.at[s, pl.ds(base + u, 1), :], sem).start()
        return carry

    lax.fori_loop(0, tm // SUBLANES, group, 0)
    pltpu.make_async_copy(yb_hbm.at[pl.ds(0, tm), :], buf.at[0], sem).wait()
    pltpu.make_async_copy(yb_hbm.at[pl.ds(0, tm), :], buf.at[1], sem).wait()
    route = route_ref[...]
    g1 = _lane_pick(route, 2)
    g2 = _lane_pick(route, 3)
    y = h_ref[...] + (g1 * buf[0] + g2 * buf[1])
    if final:
        ms = jnp.mean(y * y, axis=-1, keepdims=True)
        y = y * lax.rsqrt(ms + EPS) * g_ref[...]
    o_ref[...] = y


def _combine(pos3, h, route, yb, g, block_off, n_rows, final):
    d = h.shape[1]
    tm = COMBINE_TM
    return pl.pallas_call(
        functools.partial(_combine_kernel, final=final),
        grid=(n_rows // tm,),
        in_specs=[
            pl.BlockSpec((1, 1, 2 * tm), lambda i: (i + block_off, 0, 0), memory_space=pltpu.SMEM),
            pl.BlockSpec((tm, d), lambda i: (i + block_off, 0)),
            pl.BlockSpec((tm, LANES), lambda i: (i + block_off, 0)),
            pl.BlockSpec(memory_space=pl.ANY),
            pl.BlockSpec((1, d), lambda i: (0, 0)),
        ],
        out_specs=pl.BlockSpec((tm, d), lambda i: (i, 0)),
        out_shape=jax.ShapeDtypeStruct((n_rows, d), F32),
        scratch_shapes=[pltpu.VMEM((2, tm, d), F32), pltpu.SemaphoreType.DMA(())],
        compiler_params=pltpu.CompilerParams(
            dimension_semantics=("arbitrary",), vmem_limit_bytes=VMEM_LIMIT),
    )(pos3, h, route, yb, g)


def _pad_lanes(a):
    return jnp.pad(a, [(0, 0)] * (a.ndim - 1) + [(0, LANES - a.shape[-1])])


def kernel(x_prompt, x_sample, cache_a_k, cache_a_v, cache_b_k, cache_b_v, cache_b_logf,
           cache_c_k, cache_c_v, norm1_g, norm2_g, w_in, b_f, rel_bias, w_o,
           w_rg, b_rg, w_re, b_re, w_gate, w_up, w_down, final_g):
    bp, sp, d = x_prompt.shape
    bs, ss, _ = x_sample.shape
    depth = w_in.shape[0]
    past = cache_a_k.shape[2]
    keep = cache_c_k.shape[2]
    n_p, n_s = bp * sp, bs * ss
    n_tot = n_p + n_s
    assert d == MIX and n_p % PROJ_TM == 0 and n_s % PROJ_TM == 0
    assert ss == CHUNK and keep == C_PAST

    t_a = min(256, sp)
    t_b = min(512, sp)
    r_p = min(256, sp)
    tp_a = min(256, past)
    tp_b = min(512, past)
    assert past % tp_a == 0 and past % tp_b == 0
    n_blocks = (2 * n_tot) // MOE_BLK + N_EXPERTS

    cb_k, cb_v = jnp.swapaxes(cache_b_k, 2, 3), jnp.swapaxes(cache_b_v, 2, 3)
    cc_k, cc_v = jnp.swapaxes(cache_c_k, 2, 3), jnp.swapaxes(cache_c_v, 2, 3)
    ca_k = cache_a_k.reshape(depth, bs, past * H_A, DH)
    ca_v = cache_a_v.reshape(depth, bs, past * H_A, DH)

    xs_p = (x_prompt.reshape(n_p, d), 0)
    xs_s = (x_sample.reshape(n_s, d), 0)
    stacks_p = stacks_s = None
    logf_p, logf_s = [], []
    y_prompt = y_sample = None

    for l in range(depth):
        w_bf = w_in[l, :, :3 * MIX].astype(BF16)
        wf_bf = _pad_lanes(w_in[l, :, 3 * MIX:]).astype(BF16)
        bfp = _pad_lanes(b_f[l][None, :].astype(F32))
        g1 = norm1_g[l][None, :].astype(F32)
        q_p, stacks_p, k16_p, v16_p, lf_p = _inproj(xs_p[0], xs_p[1] * (PROJ_TM // INPROJ_TM), bp, sp, g1,
                                                    w_bf, wf_bf, bfp, l, depth, stacks_p)
        q_s, stacks_s, _, _, lf_s = _inproj(xs_s[0], xs_s[1] * (PROJ_TM // INPROJ_TM), bs, ss, g1,
                                            w_bf, wf_bf, bfp, l, depth, stacks_s)
        logf_p.append(lf_p[:, :H_B].reshape(bp, sp, H_B))
        logf_s.append(lf_s[:, :H_B].reshape(bs, ss, H_B))
        q_p3 = q_p.reshape(bp, sp, MIX)
        q_s3 = q_s.reshape(bs, ss, MIX)
        k16_p, v16_p = k16_p[None], v16_p[None]
        ka_s = stacks_s["ka"].reshape(depth, bs, ss * H_A, DH)
        va_s = stacks_s["va"].reshape(depth, bs, ss * H_A, DH)

        oa_p = _attn_a(q_p3, [(k16_p, 0, 0), (v16_p, 0, 0)], None, t_a, t_a)
        oa_s = _attn_a(q_s3, [(ka_s, l, 0), (va_s, l, 0)], [(ca_k, l, 0), (ca_v, l, 0)], ss, tp_a)

        def row_layout(fcol):
            return (jnp.swapaxes(fcol[:, :, :H_B], 1, 2) * LOG2E)[:, :, None, :]

        fcol_p = _cumsum_seq(lf_p.reshape(bp, sp, LANES), min(512, sp))
        lf_cat = jnp.concatenate([_pad_lanes(cache_b_logf[l].astype(F32)),
                                  lf_s.reshape(bs, ss, LANES)], axis=1)
        t_cat = past + ss
        tc_s = t_cat // 3 if (t_cat % 24 == 0) else t_cat
        fcol_s = _cumsum_seq(lf_cat, tc_s)
        ob_p = _attn_b(q_p3, [(k16_p, 0, H_A), (v16_p, 0, H_A)], None, row_layout(fcol_p), t_b, t_b)
        ob_s = _attn_b(q_s3, [(stacks_s["kb"], l, 0), (stacks_s["vb"], l, 0)],
                       [(cb_k, l, 0), (cb_v, l, 0)], row_layout(fcol_s), ss, tp_b)

        oc_p = _attn_c(q_p3, [(k16_p, 0, H_A + H_B), (v16_p, 0, H_A + H_B)], None,
                       _band_table(rel_bias[l], r_p), r_p)
        oc_s = _attn_c(q_s3, [(stacks_s["kc"], l, 0), (stacks_s["vc"], l, 0)],
                       [(cc_k, l, 0), (cc_v, l, 0)], _band_table(rel_bias[l], ss), ss)

        wo_bf = w_o[l].astype(BF16)
        g2 = norm2_g[l][None, :].astype(F32)
        wr = _pad_lanes(jnp.concatenate([w_rg[l], w_re[l]], axis=1).astype(F32))
        wr1 = wr.astype(BF16)
        wr2 = jnp.stack([wr1, (wr - wr1.astype(F32)).astype(BF16)])
        br = _pad_lanes(jnp.concatenate([b_rg[l], b_re[l]])[None, :].astype(F32))
        *shared, cnt_p = _outproj(xs_p[0], xs_p[1], oa_p.reshape(n_p, -1), ob_p.reshape(n_p, -1),
                                  oc_p.reshape(n_p, -1), wo_bf, g2, wr2, br,
                                  jnp.zeros((1, LANES), F32), n_tot, 0, None)
        h, hn, route, cnt = _outproj(xs_s[0], xs_s[1], oa_s.reshape(n_s, -1), ob_s.reshape(n_s, -1),
                                     oc_s.reshape(n_s, -1), wo_bf, g2, wr2, br, cnt_p, n_tot,
                                     n_p // PROJ_TM, shared)

        eid = route[:, :2].astype(jnp.int32)
        rank = route[:, 4:6].astype(jnp.int32)
        dest, src, block_e = _dispatch(eid, rank, cnt[0, :N_EXPERTS].astype(jnp.int32), n_blocks)
        yb = _moe(hn, src, block_e, w_gate, w_up, w_down, l, n_blocks)
        pos3 = dest.reshape(n_tot // COMBINE_TM, 1, 2 * COMBINE_TM)
        fg = final_g[None, :].astype(F32)
        if l + 1 < depth:
            y = _combine(pos3, h, route, yb, fg, 0, n_tot, False)
            xs_p = (y, 0)
            xs_s = (y, n_p // PROJ_TM)
        else:
            y_prompt = _combine(pos3, h, route, yb, fg, 0, n_p, True).reshape(bp, sp, d)
            y_sample = _combine(pos3, h, route, yb, fg, n_p // COMBINE_TM, n_s, True).reshape(bs, ss, d)

    def tok_state(stacks, nm, b, s):
        return stacks[nm]

    def hm_state(stacks, nm):
        return jnp.swapaxes(stacks[nm], 2, 3)

    keep_p = min(C_PAST, sp)
    return (y_prompt, y_sample,
            tok_state(stacks_p, "ka", bp, sp), tok_state(stacks_p, "va", bp, sp),
            hm_state(stacks_p, "kb"), hm_state(stacks_p, "vb"),
            jnp.stack(logf_p, axis=0),
            hm_state(stacks_p, "kc")[:, :, sp - keep_p:], hm_state(stacks_p, "vc")[:, :, sp - keep_p:],
            tok_state(stacks_s, "ka", bs, ss), tok_state(stacks_s, "va", bs, ss),
            hm_state(stacks_s, "kb"), hm_state(stacks_s, "vb"),
            jnp.stack(logf_s, axis=0),
            hm_state(stacks_s, "kc"), hm_state(stacks_s, "vc"))
```

```python
import functools
import math

import jax
import jax.numpy as jnp
from jax import lax
from jax.experimental import pallas as pl
from jax.experimental.pallas import tpu as pltpu

F32 = jnp.float32
BF16 = jnp.bfloat16

DH = 128
H_A, H_B, H_C = 4, 6, 6
N_HEADS = H_A + H_B + H_C
MIX = N_HEADS * DH
CHUNK = 64
C_PREV = 8
C_PAST = C_PREV * CHUNK
REL_CLIP = 128
N_GROUPS = 4
EXP_PER_GROUP = 8
N_EXPERTS = N_GROUPS * EXP_PER_GROUP
EPS = 1e-6
NEG_INF = -1e30
SCALE = DH ** -0.5
LOG2E = math.log2(math.e)

LANES = 128
SUBLANES = 8
BF16_ROWS = 16
INPROJ_TM = 256
PROJ_TM = 512
PROJ_TN = 512
MOE_BLK = 256
COMBINE_TM = 256
SB_CUTOFF = -104.0
VMEM_LIMIT = 56 * 1024 * 1024


def _nt_dot(a, b):
    return lax.dot_general(a, b, (((1,), (1,)), ((), ())), preferred_element_type=F32)


def _dot(a, b):
    return jnp.dot(a, b, preferred_element_type=F32)


def _split3(x):
    x1 = x.astype(BF16)
    r1 = x - x1.astype(F32)
    x2 = r1.astype(BF16)
    x3 = (r1 - x2.astype(F32)).astype(BF16)
    return x1, x2, x3


def _aligned(x, m):
    return x if isinstance(x, int) else pl.multiple_of(x, m)


def _lane_pick(x, idx):
    lane = lax.broadcasted_iota(jnp.int32, x.shape, 1)
    return jnp.sum(jnp.where(lane == idx, x, 0.0), axis=1, keepdims=True)


def _inproj_kernel(x_ref, g_ref, w_hbm, wf_ref, bf_ref,
                   q_ref, ka_ref, va_ref, kb_ref, vb_ref, kc_ref, vc_ref, k16_ref, v16_ref, lf_ref,
                   w_vmem, xn_ref, sem, *, rb):
    tm = x_ref.shape[0]
    nbb = tm // rb
    heads_per_chunk = PROJ_TN // DH
    chunks_per_part = MIX // PROJ_TN

    @pl.when(pl.program_id(0) == 0)
    def _():
        cp = pltpu.make_async_copy(w_hbm, w_vmem, sem)
        cp.start()
        cp.wait()

    x = x_ref[...]
    ms = jnp.mean(x * x, axis=-1, keepdims=True)
    xn_ref[...] = (x * lax.rsqrt(ms + EPS) * g_ref[...]).astype(BF16)
    lf_ref[...] = jax.nn.log_sigmoid(_dot(xn_ref[...], wf_ref[...]) + bf_ref[...])

    def write_kv(jj, acc, tok_ref, b_ref, c_ref, h16_ref):
        for u in range(heads_per_chunk):
            hh = (jj * heads_per_chunk + u) % N_HEADS
            piece = acc[:, u * DH:(u + 1) * DH]
            for bb in range(nbb):
                rows = piece[bb * rb:(bb + 1) * rb]
                h16_ref[bb, hh] = rows.astype(BF16)
                if hh < H_A:
                    tok_ref[bb, :, hh, :] = rows
                elif hh < H_A + H_B:
                    b_ref[bb, hh - H_A] = rows
                elif hh >= H_A + H_B:
                    c_ref[bb, hh - H_A - H_B] = rows

    for jj in range(3 * chunks_per_part):
        acc = _dot(xn_ref[...], w_vmem[:, jj * PROJ_TN:(jj + 1) * PROJ_TN])
        part = jj // chunks_per_part
        if part == 0:
            sc = SCALE if (jj + 1) * heads_per_chunk <= H_A else SCALE * LOG2E
            q_ref[:, jj * PROJ_TN:(jj + 1) * PROJ_TN] = (acc * sc).astype(BF16)
        elif part == 1:
            write_kv(jj, acc, ka_ref, kb_ref, kc_ref, k16_ref)
        else:
            write_kv(jj, acc, va_ref, vb_ref, vc_ref, v16_ref)


_KV_NAMES = ["ka", "va", "kb", "vb", "kc", "vc"]


def _inproj(x2d, row_block_off, nb_batch, seq, g, w_bf, wf_bf, bfp, layer, depth, prev):
    assert H_A * DH == PROJ_TN
    d = x2d.shape[1]
    tm = INPROJ_TM
    n_rows = nb_batch * seq
    nb = n_rows // tm
    rb = min(seq, tm)
    nbb = tm // rb
    spb = seq // rb

    def hm_map(i):
        return (layer, i // spb, 0, i % spb, 0)

    def hm16_map(i):
        return (i // spb, 0, i % spb, 0)

    out_shape = [jax.ShapeDtypeStruct((n_rows, MIX), BF16)]
    out_specs = [pl.BlockSpec((tm, MIX), lambda i: (i, 0))]
    for nm in _KV_NAMES:
        if nm in ("ka", "va"):
            out_shape.append(jax.ShapeDtypeStruct((depth, nb_batch, seq, H_A, DH), F32))
            out_specs.append(pl.BlockSpec((None, nbb, rb, H_A, DH),
                                          lambda i: (layer, i // spb, i % spb, 0, 0)))
        else:
            hn = H_B if nm[1] == "b" else H_C
            out_shape.append(jax.ShapeDtypeStruct((depth, nb_batch, hn, seq, DH), F32))
            out_specs.append(pl.BlockSpec((None, nbb, hn, rb, DH), hm_map))
    for _ in range(2):
        out_shape.append(jax.ShapeDtypeStruct((nb_batch, N_HEADS, seq, DH), BF16))
        out_specs.append(pl.BlockSpec((nbb, N_HEADS, rb, DH), hm16_map))
    out_shape.append(jax.ShapeDtypeStruct((n_rows, LANES), F32))
    out_specs.append(pl.BlockSpec((tm, LANES), lambda i: (i, 0)))
    n_out = len(out_shape)

    in_specs = [
        pl.BlockSpec((tm, d), lambda i: (i + row_block_off, 0)),
        pl.BlockSpec((1, d), lambda i: (0, 0)),
        pl.BlockSpec(memory_space=pl.ANY),
        pl.BlockSpec((d, LANES), lambda i: (0, 0)),
        pl.BlockSpec((1, LANES), lambda i: (0, 0)),
    ]
    args = [x2d, g, w_bf, wf_bf, bfp]
    n_in = len(args)
    aliases = {}
    if prev is not None:
        for t, nm in enumerate(_KV_NAMES):
            in_specs.append(pl.BlockSpec(memory_space=pl.ANY))
            args.append(prev[nm])
            aliases[n_in + t] = 1 + t

    def body(*refs):
        _inproj_kernel(*refs[:n_in], *refs[len(args):len(args) + n_out], *refs[len(args) + n_out:], rb=rb)

    res = pl.pallas_call(
        body,
        grid=(nb,),
        in_specs=in_specs,
        out_specs=out_specs,
        out_shape=out_shape,
        scratch_shapes=[pltpu.VMEM(w_bf.shape, BF16), pltpu.VMEM((tm, d), BF16),
                        pltpu.SemaphoreType.DMA(())],
        input_output_aliases=aliases,
        compiler_params=pltpu.CompilerParams(
            dimension_semantics=("arbitrary",), vmem_limit_bytes=VMEM_LIMIT),
    )(*args)
    stacks = dict(zip(_KV_NAMES, res[1:7]))
    return res[0], stacks, res[7], res[8], res[9]


def _cumsum_kernel(x_ref, f_ref, carry_ref, *, tc):
    @pl.when(pl.program_id(1) == 0)
    def _():
        carry_ref[...] = jnp.zeros_like(carry_ref)

    x = x_ref[...]
    row = lax.broadcasted_iota(jnp.int32, (tc, tc), 0)
    col = lax.broadcasted_iota(jnp.int32, (tc, tc), 1)
    lower = jnp.where(row >= col, 1.0, 0.0).astype(BF16)
    x1, x2, x3 = _split3(x)
    cs = _dot(lower, x3) + _dot(lower, x2) + _dot(lower, x1) + carry_ref[...]
    f_ref[...] = cs
    carry_ref[...] = cs[tc - 1:tc, :]


def _cumsum_seq(x, tc):
    b, t, _ = x.shape
    return pl.pallas_call(
        functools.partial(_cumsum_kernel, tc=tc),
        grid=(b, t // tc),
        in_specs=[pl.BlockSpec((None, tc, LANES), lambda i, j: (i, j, 0))],
        out_specs=pl.BlockSpec((None, tc, LANES), lambda i, j: (i, j, 0)),
        out_shape=jax.ShapeDtypeStruct((b, t, LANES), F32),
        scratch_shapes=[pltpu.VMEM((1, LANES), F32)],
        compiler_params=pltpu.CompilerParams(
            dimension_semantics=("parallel", "arbitrary")),
    )(x)


def _kv_spec(arr, layer, head_off):
    if arr.ndim == 5:
        return pl.BlockSpec((None, None, None, arr.shape[3], DH),
                            lambda bi, h: (layer, bi, head_off + h, 0, 0))
    return pl.BlockSpec((None, None, arr.shape[2], DH), lambda bi, h: (layer, bi, 0, 0))


def _attn_call(body, q3, q_head_off, kv, extra, n_heads, scratch=()):
    b, tq, _ = q3.shape
    in_specs = [pl.BlockSpec((None, tq, DH), lambda bi, h: (bi, 0, q_head_off + h))]
    args = [q3]
    for arr, layer, head_off in kv:
        in_specs.append(_kv_spec(arr, layer, head_off))
        args.append(arr)
    for arr, spec in extra:
        in_specs.append(spec)
        args.append(arr)
    return pl.pallas_call(
        body,
        grid=(b, n_heads),
        in_specs=in_specs,
        out_specs=pl.BlockSpec((None, tq, DH), lambda bi, h: (bi, 0, h)),
        out_shape=jax.ShapeDtypeStruct((b, tq, n_heads * DH), BF16),
        scratch_shapes=list(scratch),
        compiler_params=pltpu.CompilerParams(
            dimension_semantics=("parallel", "parallel"), vmem_limit_bytes=VMEM_LIMIT),
    )(*args)


def _attn_a_kernel(*refs, t, tp, has_past, interleaved):
    if has_past:
        q_ref, k_ref, v_ref, kp_ref, vp_ref, o_ref = refs
    else:
        q_ref, k_ref, v_ref, o_ref = refs
    n_q = q_ref.shape[0] // t

    def later_matrix(n):
        row = lax.broadcasted_iota(jnp.int32, (n, n), 0)
        col = lax.broadcasted_iota(jnp.int32, (n, n), 1)
        return jnp.where(row > col, 1.0, 0.0).astype(BF16)

    def cond(carry):
        j, c, _ = carry
        return jnp.logical_and(j >= 0, jnp.max(c) > SB_CUTOFF)

    def block(q, kr, vr, tb, j, c, acc, diag, valid=None):
        start = _aligned(j * tb, tb)
        if interleaved:
            keys = pl.ds(start * H_A + pl.program_id(1), tb, stride=H_A)
        else:
            keys = pl.ds(start, tb)
        kb = kr[keys, :].astype(BF16)
        vb = vr[keys, :].astype(BF16)
        later = later_matrix(tb)
        z = _nt_dot(q, kb)
        sp = jnp.maximum(z, 0.0) + jnp.log1p(jnp.exp(-jnp.abs(z)))
        lm = -sp
        if diag:
            row = lax.broadcasted_iota(jnp.int32, (t, tb), 0)
            col = lax.broadcasted_iota(jnp.int32, (t, tb), 1)
            mask = col < row
            lm = jnp.where(mask, lm, 0.0)
        if valid is not None:
            lm = lm * valid
        hi = lm.astype(BF16)
        lo = (lm - hi.astype(F32)).astype(BF16)
        suffix = _dot(lo, later) + _dot(hi, later)
        w = jnp.exp(z - sp + suffix + c)
        if diag:
            w = jnp.where(mask, w, 0.0)
        if valid is not None:
            w = w * valid
        return c + jnp.sum(lm, axis=1, keepdims=True), acc + _dot(w.astype(BF16), vb)

    def walk(q, kr, vr, tb, j0, c, acc):
        def body(carry):
            j, c_in, acc_in = carry
            c_out, acc_out = block(q, kr, vr, tb, j, c_in, acc_in, False)
            return j - 1, c_out, acc_out

        _, c, acc = lax.while_loop(cond, body, (jnp.asarray(j0, jnp.int32), c, acc))
        return c, acc

    def q_block(i, carry):
        rows = pl.ds(_aligned(i * t, t), t)
        q = q_ref[rows, :]
        c, acc = jnp.zeros((t, 1), F32), jnp.zeros((t, DH), F32)
        c, acc = block(q, k_ref, v_ref, t, i, c, acc, True)
        if n_q > 1:
            has_prev = jnp.where(i >= 1, 1.0, 0.0)
            c, acc = block(q, k_ref, v_ref, t, jnp.maximum(i - 1, 0), c, acc, False, has_prev)
            c, acc = walk(q, k_ref, v_ref, t, i - 2, c, acc)
        if has_past:
            n_past = kp_ref.shape[0] // (tp * (H_A if interleaved else 1))
            c, acc = walk(q, kp_ref, vp_ref, tp, n_past - 1, c, acc)
        o_ref[rows, :] = acc.astype(o_ref.dtype)
        return carry

    if n_q == 1:
        q_block(0, 0)
    else:
        lax.fori_loop(0, n_q, q_block, 0)


def _attn_a(q3, own, past, t, tp):
    kv = list(own) + (list(past) if past else [])
    interleaved = own[0][0].ndim == 4
    return _attn_call(functools.partial(_attn_a_kernel, t=t, tp=tp, has_past=bool(past),
                                        interleaved=interleaved),
                      q3, 0, kv, [], H_A)


def _attn_b_kernel(*refs, t, tp, has_past):
    if has_past:
        q_ref, k_ref, v_ref, kp_ref, vp_ref, fk_ref, o_ref = refs
        t_past = kp_ref.shape[0]
    else:
        q_ref, k_ref, v_ref, fk_ref, o_ref = refs
        t_past = 0
    n_q = q_ref.shape[0] // t
    rt = BF16_ROWS

    def update(s, fk, vb, carry, q_minus_k):
        m, l, acc = carry
        tb = s.shape[1]
        if q_minus_k is not None:
            row = lax.broadcasted_iota(jnp.int32, (rt, tb), 0)
            col = lax.broadcasted_iota(jnp.int32, (rt, tb), 1)
        sb = s - fk
        if q_minus_k is not None:
            row_b = lax.broadcasted_iota(jnp.int32, (t, tb), 0)
            col_b = lax.broadcasted_iota(jnp.int32, (t, tb), 1)
            sb = jnp.where(col_b <= row_b + q_minus_k, sb, NEG_INF)
        m_new = jnp.maximum(m, jnp.max(sb, axis=1, keepdims=True))
        alpha = jnp.exp2(m - m_new)
        ps, parts = [], []
        for r in range(t // rt):
            s_t = s[r * rt:(r + 1) * rt] - fk
            if q_minus_k is not None:
                s_t = jnp.where(col <= row + (r * rt + q_minus_k), s_t, NEG_INF)
            p = jnp.exp2(s_t - m_new[r * rt:(r + 1) * rt])
            if tb % LANES == 0:
                parts.append(functools.reduce(
                    jnp.add, [p[:, c * LANES:(c + 1) * LANES] for c in range(tb // LANES)]))
            else:
                lane = lax.broadcasted_iota(jnp.int32, (rt, LANES), 1)
                parts.append(jnp.where(lane == 0, jnp.sum(p, axis=1, keepdims=True), 0.0))
            ps.append(p.astype(BF16))
        l = alpha * l + jnp.concatenate(parts, axis=0)
        acc = alpha * acc + _dot(jnp.concatenate(ps, axis=0), vb)
        return m_new, l, acc

    def q_block(i, carry_unused):
        rows = pl.ds(_aligned(i * t, t), t)
        q = q_ref[rows, :]
        carry = (jnp.full((t, 1), NEG_INF, F32), jnp.zeros((t, LANES), F32), jnp.zeros((t, DH), F32))

        if has_past:
            def past_body(j, c):
                start = pl.multiple_of(j * tp, tp)
                kb = kp_ref[pl.ds(start, tp), :].astype(BF16)
                vb = vp_ref[pl.ds(start, tp), :].astype(BF16)
                return update(_nt_dot(q, kb), fk_ref[:, pl.ds(start, tp)], vb, c, None)

            carry = lax.fori_loop(0, t_past // tp, past_body, carry)

        def own_start(j):
            return _aligned(j * t, t)

        def scores(j):
            return _nt_dot(q, k_ref[pl.ds(own_start(j), t), :].astype(BF16))

        def own_update(s, j, c, diag):
            vb = v_ref[pl.ds(own_start(j), t), :].astype(BF16)
            fk = fk_ref[:, pl.ds(t_past + own_start(j), t)]
            return update(s, fk, vb, c, 0 if diag else None)

        def body(j, c):
            s_cur, inner = c
            s_next = scores(j + 1)
            return s_next, own_update(s_cur, j, inner, False)

        s = scores(0)
        if n_q > 1:
            s, carry = lax.fori_loop(0, i, body, (s, carry))
        _, l, acc = own_update(s, i, carry, True)
        o_ref[rows, :] = (acc / jnp.sum(l, axis=1, keepdims=True)).astype(o_ref.dtype)
        return carry_unused

    if n_q == 1:
        q_block(0, 0)
    else:
        lax.fori_loop(0, n_q, q_block, 0)


def _attn_b_blocks_kernel(q_ref, k_ref, v_ref, fk_ref, o_ref,
                          s0_ref, s1_ref, p_ref, m_ref, l_ref, acc_ref, *, t):
    n_q = q_ref.shape[0] // t
    rt = BF16_ROWS

    def q_block(i, carry):
        rows = pl.ds(pl.multiple_of(i * t, t), t)
        q = q_ref[rows, :]
        m_ref[...] = jnp.full(m_ref.shape, NEG_INF, F32)
        l_ref[...] = jnp.zeros(l_ref.shape, F32)
        acc_ref[...] = jnp.zeros(acc_ref.shape, F32)

        def keys(j):
            return pl.ds(pl.multiple_of(j * t, t), t)

        def put_scores(dst, j):
            dst[...] = _nt_dot(q, k_ref[keys(j), :]) - fk_ref[:, keys(j)]

        def consume(src, j, diag):
            sb = src[...]
            if diag:
                row_b = lax.broadcasted_iota(jnp.int32, (t, t), 0)
                col_b = lax.broadcasted_iota(jnp.int32, (t, t), 1)
                sb = jnp.where(col_b <= row_b, sb, NEG_INF)
                row = lax.broadcasted_iota(jnp.int32, (rt, t), 0)
                col = lax.broadcasted_iota(jnp.int32, (rt, t), 1)
            m_old = m_ref[...]
            m_new = jnp.maximum(m_old, jnp.max(sb, axis=1, keepdims=True))
            alpha = jnp.exp2(m_old - m_new)
            m_ref[...] = m_new
            for r in range(t // rt):
                sl = slice(r * rt, (r + 1) * rt)
                s_t = src[sl, :]
                if diag:
                    s_t = jnp.where(col <= row + r * rt, s_t, NEG_INF)
                p = jnp.exp2(s_t - m_new[sl])
                part = functools.reduce(jnp.add, [p[:, c * LANES:(c + 1) * LANES] for c in range(t // LANES)])
                l_ref[sl, :] = alpha[sl] * l_ref[sl, :] + part
                p_ref[sl, :] = p.astype(BF16)
            acc_ref[...] = alpha * acc_ref[...] + _dot(p_ref[...], v_ref[keys(j), :])

        put_scores(s0_ref, 0)

        def pair(pp, c):
            j = 2 * pp
            put_scores(s1_ref, j + 1)
            consume(s0_ref, j, False)
            put_scores(s0_ref, j + 2)
            consume(s1_ref, j + 1, False)
            return c

        lax.fori_loop(0, i // 2, pair, 0)

        @pl.when(i % 2 == 1)
        def _():
            put_scores(s1_ref, i)
            consume(s0_ref, i - 1, False)
            consume(s1_ref, i, True)

        @pl.when(i % 2 == 0)
        def _():
            consume(s0_ref, i, True)

        o_ref[rows, :] = (acc_ref[...] / jnp.sum(l_ref[...], axis=1, keepdims=True)).astype(o_ref.dtype)
        return carry

    lax.fori_loop(0, n_q, q_block, 0)


def _attn_b(q3, own, past, frow, t, tp):
    kv = list(own) + (list(past) if past else [])
    fspec = pl.BlockSpec((None, None, 1, frow.shape[3]), lambda bi, h: (bi, h, 0, 0))
    if not past and t % LANES == 0 and own[0][0].dtype == BF16:
        scratch = [pltpu.VMEM((t, t), F32), pltpu.VMEM((t, t), F32), pltpu.VMEM((t, t), BF16),
                   pltpu.VMEM((t, 1), F32), pltpu.VMEM((t, LANES), F32), pltpu.VMEM((t, DH), F32)]
        return _attn_call(functools.partial(_attn_b_blocks_kernel, t=t),
                          q3, H_A, kv, [(frow, fspec)], H_B, scratch)
    return _attn_call(functools.partial(_attn_b_kernel, t=t, tp=tp, has_past=bool(past)),
                      q3, H_A, kv, [(frow, fspec)], H_B)


def _band_table(rel_bias, r):
    w = C_PAST + r
    lh = w + r - 1
    dist = C_PAST + r - 1 - jnp.arange(lh)
    hvec = rel_bias.astype(F32)[:, jnp.clip(dist, -REL_CLIP, REL_CLIP) + REL_CLIP]
    q = lh + 1
    hq = jnp.pad(hvec, ((0, 0), (0, 1)))
    skew = jnp.tile(hq, (1, r + 1))[:, :r * (q + 1)].reshape(H_C, r, q + 1)[:, :, :w]
    bias = skew[:, ::-1, :]
    qi = jnp.arange(r)[:, None] // CHUNK
    kj = jnp.arange(w)[None, :] // CHUNK
    ok = (kj >= qi) & (kj <= qi + C_PREV)
    return jnp.where(ok[None], bias * LOG2E, NEG_INF)


def _attn_c_kernel(*refs, r, has_past):
    if has_past:
        q_ref, k_ref, v_ref, kp_ref, vp_ref, tab_ref, o_ref = refs
    else:
        q_ref, k_ref, v_ref, tab_ref, o_ref = refs
    w = C_PAST + r

    def finish(pieces, rows):
        m = functools.reduce(jnp.maximum, [jnp.max(s, axis=1, keepdims=True) for s, _ in pieces])
        ps = [jnp.exp2(s - m) for s, _ in pieces]
        l = functools.reduce(jnp.add, [jnp.sum(p, axis=1, keepdims=True) for p in ps])
        o = functools.reduce(jnp.add, [_dot(p.astype(BF16), v) for p, (_, v) in zip(ps, pieces)])
        o_ref[rows, :] = (o / l).astype(o_ref.dtype)

    if has_past:
        q = q_ref[...]
        s_p = _nt_dot(q, kp_ref[...].astype(BF16)) + tab_ref[:, 0:C_PAST]
        s_o = _nt_dot(q, k_ref[...].astype(BF16)) + tab_ref[:, C_PAST:w]
        finish([(s_p, vp_ref[...].astype(BF16)), (s_o, v_ref[...].astype(BF16))], pl.ds(0, r))
        return

    def tile(i, start, width, tcol0):
        rows = pl.ds(_aligned(i * r, r), r)
        q = q_ref[rows, :]
        kb = k_ref[pl.ds(start, width), :].astype(BF16)
        vb = v_ref[pl.ds(start, width), :].astype(BF16)
        finish([(_nt_dot(q, kb) + tab_ref[:, tcol0:tcol0 + width], vb)], rows)

    n_tiles = q_ref.shape[0] // r
    n_short = min(C_PAST // r, n_tiles)
    for t in range(n_short):
        tile(t, 0, (t + 1) * r, C_PAST - t * r)

    def full_tile(i, carry):
        tile(i, pl.multiple_of(i * r - C_PAST, SUBLANES), w, 0)
        return carry

    lax.fori_loop(n_short, n_tiles, full_tile, 0, unroll=2)


def _attn_c(q3, own, past, table, r):
    assert C_PAST % r == 0
    kv = list(own) + (list(past) if past else [])
    tspec = pl.BlockSpec((None, r, C_PAST + r), lambda bi, h: (h, 0, 0))
    return _attn_call(functools.partial(_attn_c_kernel, r=r, has_past=bool(past)),
                      q3, H_A + H_B, kv, [(table, tspec)], H_C)


def _outproj_kernel(x_ref, oa_ref, ob_ref, oc_ref, wo_ref, g_ref, wr_ref, br_ref, cnt_in_ref,
                    h_ref, hn_ref, route_ref, cnt_out_ref, cnt_ref):
    @pl.when(pl.program_id(0) == 0)
    def _():
        cnt_ref[...] = cnt_in_ref[...]

    a0, a1 = H_A * DH, (H_A + H_B) * DH
    acc = _dot(oa_ref[...], wo_ref[0:a0, :])
    acc = acc + _dot(ob_ref[...], wo_ref[a0:a1, :])
    acc = acc + _dot(oc_ref[...], wo_ref[a1:MIX, :])
    h = x_ref[...] + acc
    h_ref[...] = h
    ms = jnp.mean(h * h, axis=-1, keepdims=True)
    hn = h * lax.rsqrt(ms + EPS) * g_ref[...]
    hn_ref[...] = hn

    h1, h2, _ = _split3(hn)
    w1 = wr_ref[0]
    w2 = wr_ref[1]
    lg = _dot(h2, w1) + _dot(h1, w2) + _dot(h1, w1) + br_ref[...]

    lane = lax.broadcasted_iota(jnp.int32, lg.shape, 1)
    lane_f = lane.astype(F32)
    is_g = lane < N_GROUPS
    lgm = jnp.where(is_g, lg, NEG_INF)
    gmax = jnp.max(lgm, axis=1, keepdims=True)
    gidx = jnp.min(jnp.where(lgm == gmax, lane_f, float(LANES)), axis=1, keepdims=True)
    pg = 1.0 / jnp.sum(jnp.where(is_g, jnp.exp(lgm - gmax), 0.0), axis=1, keepdims=True)
    grp = jnp.floor((lane_f - N_GROUPS) * (1.0 / EXP_PER_GROUP))
    in_grp = (lane >= N_GROUPS) & (lane < N_GROUPS + N_EXPERTS) & (grp == gidx)
    le = jnp.where(in_grp, lg, NEG_INF)
    t1 = jnp.max(le, axis=1, keepdims=True)
    i1 = jnp.min(jnp.where(le == t1, lane_f, float(LANES)), axis=1, keepdims=True)
    le2 = jnp.where(lane_f == i1, NEG_INF, le)
    t2 = jnp.max(le2, axis=1, keepdims=True)
    i2 = jnp.min(jnp.where(le2 == t2, lane_f, float(LANES)), axis=1, keepdims=True)
    e = jnp.exp(t2 - t1)
    g1 = pg / (1.0 + e)
    g2 = pg * e / (1.0 + e)
    e1, e2 = i1 - N_GROUPS, i2 - N_GROUPS
    oh1 = jnp.where(lane_f == e1, 1.0, 0.0)
    oh2 = jnp.where(lane_f == e2, 1.0, 0.0)
    both = oh1 + oh2
    tm = lg.shape[0]
    row = lax.broadcasted_iota(jnp.int32, (tm, tm), 0)
    col = lax.broadcasted_iota(jnp.int32, (tm, tm), 1)
    earlier = jnp.where(row > col, 1.0, 0.0).astype(BF16)
    base = _dot(earlier, both.astype(BF16)) + cnt_ref[...]
    r1 = jnp.sum(oh1 * base, axis=1, keepdims=True)
    r2 = jnp.sum(oh2 * base, axis=1, keepdims=True)
    cnt_ref[...] = cnt_ref[...] + jnp.sum(both, axis=0, keepdims=True)
    cnt_out_ref[...] = cnt_ref[...]

    route = jnp.where(lane == 0, e1, jnp.where(lane == 1, e2, jnp.where(lane == 2, g1, jnp.where(
        lane == 3, g2, jnp.where(lane == 4, r1, jnp.where(lane == 5, r2, 0.0))))))
    route_ref[...] = route


def _outproj(x2d, x_block_off, oa, ob, oc, wo_bf, g, wr2, br, counts, n_total, out_block_off, prev):
    n_rows, d = oa.shape[0], x2d.shape[1]
    tm = PROJ_TM
    nb = n_rows // tm
    in_specs = [
        pl.BlockSpec((tm, d), lambda i: (i + x_block_off, 0)),
        pl.BlockSpec((tm, H_A * DH), lambda i: (i, 0)),
        pl.BlockSpec((tm, H_B * DH), lambda i: (i, 0)),
        pl.BlockSpec((tm, H_C * DH), lambda i: (i, 0)),
        pl.BlockSpec((MIX, d), lambda i: (0, 0)),
        pl.BlockSpec((1, d), lambda i: (0, 0)),
        pl.BlockSpec((2, d, LANES), lambda i: (0, 0, 0)),
        pl.BlockSpec((1, LANES), lambda i: (0, 0)),
        pl.BlockSpec((1, LANES), lambda i: (0, 0)),
    ]
    args = [x2d, oa, ob, oc, wo_bf, g, wr2, br, counts]
    n_in = len(args)
    aliases = {}
    if prev is not None:
        for t, arr in enumerate(prev):
            in_specs.append(pl.BlockSpec(memory_space=pl.ANY))
            args.append(arr)
            aliases[n_in + t] = t

    def body(*refs):
        _outproj_kernel(*refs[:n_in], *refs[len(args):])

    return pl.pallas_call(
        body,
        grid=(nb,),
        in_specs=in_specs,
        out_specs=[pl.BlockSpec((tm, d), lambda i: (i + out_block_off, 0)),
                   pl.BlockSpec((tm, d), lambda i: (i + out_block_off, 0)),
                   pl.BlockSpec((tm, LANES), lambda i: (i + out_block_off, 0)),
                   pl.BlockSpec((1, LANES), lambda i: (0, 0))],
        out_shape=[jax.ShapeDtypeStruct((n_total, d), F32),
                   jax.ShapeDtypeStruct((n_total, d), F32),
                   jax.ShapeDtypeStruct((n_total, LANES), F32),
                   jax.ShapeDtypeStruct((1, LANES), F32)],
        scratch_shapes=[pltpu.VMEM((1, LANES), F32)],
        input_output_aliases=aliases,
        compiler_params=pltpu.CompilerParams(
            dimension_semantics=("arbitrary",), vmem_limit_bytes=VMEM_LIMIT),
    )(*args)


def _scatter_rows_kernel(cnt_ref, pst_ref, dest_ref, hn_ref, xs_hbm, sem, fill_sem):
    i = pl.program_id(0)
    n_pairs = dest_ref.shape[2]
    tm = n_pairs // 2

    def fill_copy(row):
        return pltpu.make_async_copy(hn_ref.at[pl.ds(0, 1), :], xs_hbm.at[pl.ds(row, 1), :], fill_sem)

    @pl.when(i == 0)
    def _():
        def per_expert(e, total):
            n_pad = (MOE_BLK - cnt_ref[e] % MOE_BLK) % MOE_BLK
            base = pst_ref[e] + cnt_ref[e]

            def one(r, c):
                fill_copy(base + r).start()
                return c

            lax.fori_loop(0, n_pad, one, 0)
            return total + n_pad

        total = lax.fori_loop(0, N_EXPERTS, per_expert, 0)

        def drain(r, c):
            fill_copy(0).wait()
            return c

        lax.fori_loop(0, total, drain, 0)

    def group(g, carry):
        base = pl.multiple_of(g * SUBLANES, SUBLANES)
        for u in range(SUBLANES):
            pltpu.make_async_copy(hn_ref.at[pl.ds(base // 2 + u // 2, 1), :],
                                  xs_hbm.at[pl.ds(dest_ref[0, 0, base + u], 1), :], sem).start()
        return carry

    lax.fori_loop(0, n_pairs // SUBLANES, group, 0)
    for _ in range(2):
        pltpu.make_async_copy(hn_ref, xs_hbm.at[pl.ds(0, tm), :], sem).wait()


def _scatter_rows(hn, dest, counts, pstart, n_blocks):
    n, d = hn.shape
    tm = COMBINE_TM
    grid_spec = pltpu.PrefetchScalarGridSpec(
        num_scalar_prefetch=2,
        grid=(n // tm,),
        in_specs=[pl.BlockSpec((1, 1, 2 * tm), lambda i, c, p: (i, 0, 0), memory_space=pltpu.SMEM),
                  pl.BlockSpec((tm, d), lambda i, c, p: (i, 0))],
        out_specs=pl.BlockSpec(memory_space=pl.ANY),
        scratch_shapes=[pltpu.SemaphoreType.DMA(()), pltpu.SemaphoreType.DMA(())],
    )
    return pl.pallas_call(
        _scatter_rows_kernel,
        grid_spec=grid_spec,
        out_shape=jax.ShapeDtypeStruct((n_blocks * MOE_BLK, d), F32),
        compiler_params=pltpu.CompilerParams(dimension_semantics=("arbitrary",)),
    )(counts, pstart, dest.reshape(n // tm, 1, 2 * tm), hn)


def _moe_kernel(be_ref, nused_ref, xs_ref, wg_ref, wu_ref, wd_ref, yb_ref, wg16, wu16, wd16):
    i = pl.program_id(0)

    @pl.when(jnp.logical_or(i == 0, be_ref[i] != be_ref[jnp.maximum(i - 1, 0)]))
    def _():
        wg16[...] = wg_ref[...].astype(BF16)
        wu16[...] = wu_ref[...].astype(BF16)
        wd16[...] = wd_ref[...].astype(BF16)

    @pl.when(i < nused_ref[0])
    def _():
        xb = xs_ref[...].astype(BF16)
        gate = _dot(xb, wg16[...])
        up = _dot(xb, wu16[...])
        act = (gate * jax.nn.sigmoid(gate) * up).astype(BF16)
        yb_ref[...] = _dot(act, wd16[...])

    @pl.when(i >= nused_ref[0])
    def _():
        yb_ref[...] = jnp.zeros_like(yb_ref)


def _moe(xs, block_e, nused, wg, wu, wd, layer, n_blocks):
    d = xs.shape[1]
    de = wg.shape[3]

    def row_map(i, be, nu):
        return (jnp.minimum(i, nu[0] - 1), 0)

    grid_spec = pltpu.PrefetchScalarGridSpec(
        num_scalar_prefetch=2,
        grid=(n_blocks,),
        in_specs=[
            pl.BlockSpec((MOE_BLK, d), row_map),
            pl.BlockSpec((None, None, d, de), lambda i, be, nu: (layer, be[i], 0, 0)),
            pl.BlockSpec((None, None, d, de), lambda i, be, nu: (layer, be[i], 0, 0)),
            pl.BlockSpec((None, None, de, d), lambda i, be, nu: (layer, be[i], 0, 0)),
        ],
        out_specs=pl.BlockSpec((MOE_BLK, d), lambda i, be, nu: (i, 0)),
        scratch_shapes=[pltpu.VMEM((d, de), BF16), pltpu.VMEM((d, de), BF16), pltpu.VMEM((de, d), BF16)],
    )
    return pl.pallas_call(
        _moe_kernel,
        grid_spec=grid_spec,
        out_shape=jax.ShapeDtypeStruct((n_blocks * MOE_BLK, d), F32),
        compiler_params=pltpu.CompilerParams(
            dimension_semantics=("arbitrary",), vmem_limit_bytes=VMEM_LIMIT),
    )(block_e, nused, xs, wg, wu, wd)


def _dispatch(eid, rank, counts, n_blocks):
    e = eid.reshape(-1)
    onehot = (e[:, None] == jnp.arange(N_EXPERTS, dtype=jnp.int32)[None, :]).astype(jnp.int32)
    padded = (counts + MOE_BLK - 1) // MOE_BLK * MOE_BLK
    pend = jnp.cumsum(padded)
    pstart = (pend - padded).astype(jnp.int32)
    dest = (jnp.sum(onehot * pstart[None, :], axis=1) + rank.reshape(-1)).astype(jnp.int32)
    blk_start = jnp.arange(n_blocks, dtype=jnp.int32) * MOE_BLK
    block_e = jnp.minimum(jnp.sum((pend[None, :] <= blk_start[:, None]).astype(jnp.int32), axis=1),
                          N_EXPERTS - 1).astype(jnp.int32)
    nused = (pend[-1:] // MOE_BLK).astype(jnp.int32)
    return dest, pstart, block_e, nused


def _combine_kernel(pos_ref, h_ref, route_ref, yb_hbm, g_ref, o_ref, buf, sem, *, final):
    tm = h_ref.shape[0]

    def group(gi, carry):
        base = pl.multiple_of(gi * SUBLANES, SUBLANES)
        for u in range(SUBLANES):
            for s in range(2):
                pltpu.make_async_copy(yb_hbm.at[pl.ds(pos_ref[0, 0, 2 * (base + u) + s], 1), :],
                                      buf.at[s, pl.ds(base + u, 1), :], sem).start()
        return carry

    lax.fori_loop(0, tm // SUBLANES, group, 0)
    pltpu.make_async_copy(yb_hbm.at[pl.ds(0, tm), :], buf.at[0], sem).wait()
    pltpu.make_async_copy(yb_hbm.at[pl.ds(0, tm), :], buf.at[1], sem).wait()
    route = route_ref[...]
    g1 = _lane_pick(route, 2)
    g2 = _lane_pick(route, 3)
    y = h_ref[...] + (g1 * buf[0] + g2 * buf[1])
    if final:
        ms = jnp.mean(y * y, axis=-1, keepdims=True)
        y = y * lax.rsqrt(ms + EPS) * g_ref[...]
    o_ref[...] = y


def _combine(pos3, h, route, yb, g, block_off, n_rows, final):
    d = h.shape[1]
    tm = COMBINE_TM
    return pl.pallas_call(
        functools.partial(_combine_kernel, final=final),
        grid=(n_rows // tm,),
        in_specs=[
            pl.BlockSpec((1, 1, 2 * tm), lambda i: (i + block_off, 0, 0), memory_space=pltpu.SMEM),
            pl.BlockSpec((tm, d), lambda i: (i + block_off, 0)),
            pl.BlockSpec((tm, LANES), lambda i: (i + block_off, 0)),
            pl.BlockSpec(memory_space=pl.ANY),
            pl.BlockSpec((1, d), lambda i: (0, 0)),
        ],
        out_specs=pl.BlockSpec((tm, d), lambda i: (i, 0)),
        out_shape=jax.ShapeDtypeStruct((n_rows, d), F32),
        scratch_shapes=[pltpu.VMEM((2, tm, d), F32), pltpu.SemaphoreType.DMA(())],
        compiler_params=pltpu.CompilerParams(
            dimension_semantics=("arbitrary",), vmem_limit_bytes=VMEM_LIMIT),
    )(pos3, h, route, yb, g)


def _pad_lanes(a):
    return jnp.pad(a, [(0, 0)] * (a.ndim - 1) + [(0, LANES - a.shape[-1])])


def kernel(x_prompt, x_sample, cache_a_k, cache_a_v, cache_b_k, cache_b_v, cache_b_logf,
           cache_c_k, cache_c_v, norm1_g, norm2_g, w_in, b_f, rel_bias, w_o,
           w_rg, b_rg, w_re, b_re, w_gate, w_up, w_down, final_g):
    bp, sp, d = x_prompt.shape
    bs, ss, _ = x_sample.shape
    depth = w_in.shape[0]
    past = cache_a_k.shape[2]
    keep = cache_c_k.shape[2]
    n_p, n_s = bp * sp, bs * ss
    n_tot = n_p + n_s
    assert d == MIX and n_p % PROJ_TM == 0 and n_s % PROJ_TM == 0
    assert ss == CHUNK and keep == C_PAST

    t_a = min(256, sp)
    t_b = min(512, sp)
    r_p = min(256, sp)
    tp_a = min(256, past)
    tp_b = min(512, past)
    assert past % tp_a == 0 and past % tp_b == 0
    n_blocks = (2 * n_tot) // MOE_BLK + N_EXPERTS

    cb_k, cb_v = jnp.swapaxes(cache_b_k, 2, 3), jnp.swapaxes(cache_b_v, 2, 3)
    cc_k, cc_v = jnp.swapaxes(cache_c_k, 2, 3), jnp.swapaxes(cache_c_v, 2, 3)
    ca_k = cache_a_k.reshape(depth, bs, past * H_A, DH)
    ca_v = cache_a_v.reshape(depth, bs, past * H_A, DH)

    xs_p = (x_prompt.reshape(n_p, d), 0)
    xs_s = (x_sample.reshape(n_s, d), 0)
    stacks_p = stacks_s = None
    logf_p, logf_s = [], []
    y_prompt = y_sample = None

    for l in range(depth):
        w_bf = w_in[l, :, :3 * MIX].astype(BF16)
        wf_bf = _pad_lanes(w_in[l, :, 3 * MIX:]).astype(BF16)
        bfp = _pad_lanes(b_f[l][None, :].astype(F32))
        g1 = norm1_g[l][None, :].astype(F32)
        q_p, stacks_p, k16_p, v16_p, lf_p = _inproj(xs_p[0], xs_p[1] * (PROJ_TM // INPROJ_TM), bp, sp, g1,
                                                    w_bf, wf_bf, bfp, l, depth, stacks_p)
        q_s, stacks_s, _, _, lf_s = _inproj(xs_s[0], xs_s[1] * (PROJ_TM // INPROJ_TM), bs, ss, g1,
                                            w_bf, wf_bf, bfp, l, depth, stacks_s)
        logf_p.append(lf_p[:, :H_B].reshape(bp, sp, H_B))
        logf_s.append(lf_s[:, :H_B].reshape(bs, ss, H_B))
        q_p3 = q_p.reshape(bp, sp, MIX)
        q_s3 = q_s.reshape(bs, ss, MIX)
        k16_p, v16_p = k16_p[None], v16_p[None]
        ka_s = stacks_s["ka"].reshape(depth, bs, ss * H_A, DH)
        va_s = stacks_s["va"].reshape(depth, bs, ss * H_A, DH)

        oa_p = _attn_a(q_p3, [(k16_p, 0, 0), (v16_p, 0, 0)], None, t_a, t_a)
        oa_s = _attn_a(q_s3, [(ka_s, l, 0), (va_s, l, 0)], [(ca_k, l, 0), (ca_v, l, 0)], ss, tp_a)

        def row_layout(fcol):
            return (jnp.swapaxes(fcol[:, :, :H_B], 1, 2) * LOG2E)[:, :, None, :]

        fcol_p = _cumsum_seq(lf_p.reshape(bp, sp, LANES), min(512, sp))
        lf_cat = jnp.concatenate([_pad_lanes(cache_b_logf[l].astype(F32)),
                                  lf_s.reshape(bs, ss, LANES)], axis=1)
        t_cat = past + ss
        tc_s = t_cat // 3 if (t_cat % 24 == 0) else t_cat
        fcol_s = _cumsum_seq(lf_cat, tc_s)
        ob_p = _attn_b(q_p3, [(k16_p, 0, H_A), (v16_p, 0, H_A)], None, row_layout(fcol_p), t_b, t_b)
        ob_s = _attn_b(q_s3, [(stacks_s["kb"], l, 0), (stacks_s["vb"], l, 0)],
                       [(cb_k, l, 0), (cb_v, l, 0)], row_layout(fcol_s), ss, tp_b)

        oc_p = _attn_c(q_p3, [(k16_p, 0, H_A + H_B), (v16_p, 0, H_A + H_B)], None,
                       _band_table(rel_bias[l], r_p), r_p)
        oc_s = _attn_c(q_s3, [(stacks_s["kc"], l, 0), (stacks_s["vc"], l, 0)],
                       [(cc_k, l, 0), (cc_v, l, 0)], _band_table(rel_bias[l], ss), ss)

        wo_bf = w_o[l].astype(BF16)
        g2 = norm2_g[l][None, :].astype(F32)
        wr = _pad_lanes(jnp.concatenate([w_rg[l], w_re[l]], axis=1).astype(F32))
        wr1 = wr.astype(BF16)
        wr2 = jnp.stack([wr1, (wr - wr1.astype(F32)).astype(BF16)])
        br = _pad_lanes(jnp.concatenate([b_rg[l], b_re[l]])[None, :].astype(F32))
        *shared, cnt_p = _outproj(xs_p[0], xs_p[1], oa_p.reshape(n_p, -1), ob_p.reshape(n_p, -1),
                                  oc_p.reshape(n_p, -1), wo_bf, g2, wr2, br,
                                  jnp.zeros((1, LANES), F32), n_tot, 0, None)
        h, hn, route, cnt = _outproj(xs_s[0], xs_s[1], oa_s.reshape(n_s, -1), ob_s.reshape(n_s, -1),
                                     oc_s.reshape(n_s, -1), wo_bf, g2, wr2, br, cnt_p, n_tot,
                                     n_p // PROJ_TM, shared)

        eid = route[:, :2].astype(jnp.int32)
        rank = route[:, 4:6].astype(jnp.int32)
        counts = cnt[0, :N_EXPERTS].astype(jnp.int32)
        dest, pstart, block_e, nused = _dispatch(eid, rank, counts, n_blocks)
        xs = _scatter_rows(hn, dest, counts, pstart, n_blocks)
        yb = _moe(xs, block_e, nused, w_gate, w_up, w_down, l, n_blocks)
        pos3 = dest.reshape(n_tot // COMBINE_TM, 1, 2 * COMBINE_TM)
        fg = final_g[None, :].astype(F32)
        if l + 1 < depth:
            y = _combine(pos3, h, route, yb, fg, 0, n_tot, False)
            xs_p = (y, 0)
            xs_s = (y, n_p // PROJ_TM)
        else:
            y_prompt = _combine(pos3, h, route, yb, fg, 0, n_p, True).reshape(bp, sp, d)
            y_sample = _combine(pos3, h, route, yb, fg, n_p // COMBINE_TM, n_s, True).reshape(bs, ss, d)

    def tok_state(stacks, nm, b, s):
        return stacks[nm]

    def hm_state(stacks, nm):
        return jnp.swapaxes(stacks[nm], 2, 3)

    keep_p = min(C_PAST, sp)
    return (y_prompt, y_sample,
            tok_state(stacks_p, "ka", bp, sp), tok_state(stacks_p, "va", bp, sp),
            hm_state(stacks_p, "kb"), hm_state(stacks_p, "vb"),
            jnp.stack(logf_p, axis=0),
            hm_state(stacks_p, "kc")[:, :, sp - keep_p:], hm_state(stacks_p, "vc")[:, :, sp - keep_p:],
            tok_state(stacks_s, "ka", bs, ss), tok_state(stacks_s, "va", bs, ss),
            hm_state(stacks_s, "kb"), hm_state(stacks_s, "vb"),
            jnp.stack(logf_s, axis=0),
            hm_state(stacks_s, "kc"), hm_state(stacks_s, "vc"))
```
